```python
import math
import jax, jax.numpy as jnp
from jax import lax
import numpy as np

D_MODEL = 2048
BATCH = 4
SEQ = 2048
DEPTH = 2

PLE_DIM = 256
SGU_CHUNK = 128
SGU_GROUPS = 8
SGU_WIDTH = D_MODEL // 2
SGU_GROUP_DIM = SGU_WIDTH // SGU_GROUPS
DN_HEADS = 8
DN_HEAD_DIM = D_MODEL // 2 // DN_HEADS
DN_WIDTH = DN_HEADS * DN_HEAD_DIM
DN_CONV = 4
DN_CHUNK = 64
FFN_DIM = (8 * D_MODEL // 3 + 255) // 256 * 256
N_EXPERTS = 8
TOP_K = 2
EXPERT_DIM = 7 * D_MODEL // 2
N_DENSE_LAYERS = (DEPTH + 1) // 2
N_MOE_LAYERS = DEPTH // 2
IN_WIDTH = 2 * SGU_WIDTH + 4 * DN_WIDTH + 2 * DN_HEADS + 2 * D_MODEL
NORM_EPS = 1e-6

kernel_name = 'hybrid_sgu_gdn_moe_block'


def rms_norm(x, g):
    xf = x.astype(jnp.float32)
    y = xf * lax.rsqrt(jnp.mean(xf * xf, axis=-1, keepdims=True) + NORM_EPS)
    return (y * g.astype(jnp.float32)).astype(x.dtype)


def layer_norm(x, g, b):
    xf = x.astype(jnp.float32)
    mu = jnp.mean(xf, axis=-1, keepdims=True)
    xc = xf - mu
    y = xc * lax.rsqrt(jnp.mean(xc * xc, axis=-1, keepdims=True) + NORM_EPS)
    return (y * g.astype(jnp.float32) + b.astype(jnp.float32)).astype(x.dtype)


def l2_normalize(x):
    return x * lax.rsqrt(jnp.sum(x * x, axis=-1, keepdims=True) + NORM_EPS)


def causal_spatial_gate(v, w_s, b_s):
    bsz, seq, _ = v.shape
    n = seq // SGU_CHUNK
    mask = jnp.tril(jnp.ones((SGU_CHUNK, SGU_CHUNK), dtype=bool))
    w = jnp.where(mask, w_s, jnp.zeros_like(w_s)).astype(v.dtype)
    vc = v.reshape(bsz, n, SGU_CHUNK, SGU_GROUPS, SGU_GROUP_DIM)
    mixed = jnp.einsum('gts,bnsgc->bntgc', w, vc) + b_s.T[:, :, None].astype(v.dtype)
    return mixed.reshape(bsz, seq, SGU_WIDTH)


def causal_conv(x, w):
    k = w.shape[0]
    seq = x.shape[1]
    xp = jnp.pad(x, ((0, 0), (k - 1, 0), (0, 0)))
    return sum(w[j] * xp[:, j:j + seq] for j in range(k))


def gated_delta_rule(q, k, v, beta, g):
    bsz, seq, h, dk = q.shape
    dv = v.shape[-1]
    c = DN_CHUNK
    n = seq // c
    q = q * (dk ** -0.5)

    def to_chunks(t):
        return t.reshape(bsz, n, c, h, -1).transpose(0, 1, 3, 2, 4)

    q, k, v = to_chunks(q), to_chunks(k), to_chunks(v)
    beta = beta.reshape(bsz, n, c, h).transpose(0, 1, 3, 2)
    g = jnp.cumsum(g.reshape(bsz, n, c, h).transpose(0, 1, 3, 2), axis=-1)

    tri = jnp.tril(jnp.ones((c, c), dtype=bool))
    strict = jnp.tril(jnp.ones((c, c), dtype=bool), -1)
    decay = jnp.exp(jnp.where(tri, g[..., :, None] - g[..., None, :], -jnp.inf))

    k_beta = k * beta[..., None]
    v_beta = v * beta[..., None]
    a_mat = jnp.where(strict, jnp.einsum('bnhid,bnhjd->bnhij', k_beta, k) * decay, 0.0)
    lhs = a_mat + jnp.eye(c, dtype=jnp.float32)
    rhs = jnp.concatenate([v_beta, k_beta * jnp.exp(g)[..., None]], axis=-1)
    sol = lax.linalg.triangular_solve(lhs, rhs, left_side=True, lower=True, unit_diagonal=True)
    u_val, w_cum = sol[..., :dv], sol[..., dv:]

    attn_intra = jnp.where(tri, jnp.einsum('bnhid,bnhjd->bnhij', q, k) * decay, 0.0)
    q_g = q * jnp.exp(g)[..., None]
    g_last = g[..., -1]
    k_d = k * jnp.exp(g_last[..., None] - g)[..., None]
    decay_last = jnp.exp(g_last)

    def step(state, xs):
        qg_c, kd_c, u_c, w_c, attn_c, dl_c = xs
        v_new = u_c - jnp.einsum('bhck,bhkv->bhcv', w_c, state)
        o_c = (jnp.einsum('bhck,bhkv->bhcv', qg_c, state)
               + jnp.einsum('bhij,bhjv->bhiv', attn_c, v_new))
        state = state * dl_c[..., None, None] + jnp.einsum('bhck,bhcv->bhkv', kd_c, v_new)
        return state, o_c

    xs = tuple(jnp.moveaxis(t, 1, 0) for t in (q_g, k_d, u_val, w_cum, attn_intra, decay_last))
    state0 = jnp.zeros((bsz, h, dk, dv), jnp.float32)
    _, o = lax.scan(step, state0, xs)
    return o.transpose(1, 0, 3, 2, 4).reshape(bsz, seq, h, dv)


def hybrid_mixer(xn, w_in, sgu_ln_g, sgu_ln_b, sgu_w, sgu_b, dn_conv_w, dn_a_log,
                 dn_dt_bias, dn_norm_w, w_branch, w_out):
    bsz, seq, _ = xn.shape
    proj = xn @ w_in
    splits = np.cumsum([SGU_WIDTH, SGU_WIDTH, 3 * DN_WIDTH, DN_WIDTH, DN_HEADS, DN_HEADS]).tolist()
    u, v, qkv, z, b_raw, a_raw, gates = jnp.split(proj, splits, axis=-1)

    u = jax.nn.gelu(u)
    v = layer_norm(jax.nn.gelu(v), sgu_ln_g, sgu_ln_b)
    y_a = u * causal_spatial_gate(v, sgu_w, sgu_b)

    qkv = jax.nn.silu(causal_conv(qkv, dn_conv_w)).astype(jnp.float32)
    q, k, vv = jnp.split(qkv, 3, axis=-1)
    q = l2_normalize(q.reshape(bsz, seq, DN_HEADS, DN_HEAD_DIM))
    k = l2_normalize(k.reshape(bsz, seq, DN_HEADS, DN_HEAD_DIM))
    vv = vv.reshape(bsz, seq, DN_HEADS, DN_HEAD_DIM)
    beta = jax.nn.sigmoid(b_raw.astype(jnp.float32))
    g = -jnp.exp(dn_a_log.astype(jnp.float32)) * jax.nn.softplus(
        a_raw.astype(jnp.float32) + dn_dt_bias.astype(jnp.float32))
    o = gated_delta_rule(q, k, vv, beta, g)
    o = o * lax.rsqrt(jnp.mean(o * o, axis=-1, keepdims=True) + NORM_EPS)
    zf = z.astype(jnp.float32).reshape(bsz, seq, DN_HEADS, DN_HEAD_DIM)
    o = o * dn_norm_w.astype(jnp.float32) * jax.nn.silu(zf)
    y_b = o.reshape(bsz, seq, DN_WIDTH).astype(xn.dtype)

    gate_a, gate_b = jnp.split(gates, 2, axis=-1)
    merged = (jax.nn.sigmoid(gate_a) * (y_a @ w_branch[0])
              + jax.nn.sigmoid(gate_b) * (y_b @ w_branch[1]))
    return merged @ w_out


def swiglu(x, w_gate, w_up, w_down):
    return (jax.nn.silu(x @ w_gate) * (x @ w_up)) @ w_down


def moe_swiglu(x, w_router, w_gate, w_up, w_down):
    logits = jnp.einsum('bsd,de->bse', x, w_router).astype(jnp.float32)
    top_val, top_idx = lax.top_k(logits, TOP_K)
    top_w = jax.nn.softmax(top_val, axis=-1)
    combine = jnp.sum(top_w[..., None] * jax.nn.one_hot(top_idx, N_EXPERTS, dtype=jnp.float32),
                      axis=-2).astype(x.dtype)
    out = jnp.zeros_like(x)
    for e in range(N_EXPERTS):
        out = out + combine[..., e:e + 1] * swiglu(x, w_gate[e], w_up[e], w_down[e])
    return out


def setup_inputs(seed: int = 0) -> dict:
    key = jax.random.key(seed)
    ks = iter(jax.random.split(key, 40))

    def nrm(shape, scale):
        return jax.random.normal(next(ks), shape, jnp.float32) * scale

    def gain(shape):
        return 1.0 + 0.05 * jax.random.normal(next(ks), shape, jnp.float32)

    x = nrm((BATCH, SEQ, D_MODEL), 1.0)
    p = nrm((DEPTH, BATCH, SEQ, PLE_DIM), 1.0)
    norm_mix = gain((DEPTH, D_MODEL))
    w_in = nrm((DEPTH, D_MODEL, IN_WIDTH), D_MODEL ** -0.5)
    sgu_ln_g = gain((DEPTH, SGU_WIDTH))
    sgu_ln_b = nrm((DEPTH, SGU_WIDTH), 0.02)
    sgu_w = nrm((DEPTH, SGU_GROUPS, SGU_CHUNK, SGU_CHUNK), SGU_CHUNK ** -0.5)
    sgu_b = gain((DEPTH, SGU_GROUPS, SGU_CHUNK))
    dn_conv_w = nrm((DEPTH, DN_CONV, 3 * DN_WIDTH), DN_CONV ** -0.5)
    dn_a_log = jnp.log(jax.random.uniform(next(ks), (DEPTH, DN_HEADS), jnp.float32, 1.0, 16.0))
    dt = jnp.exp(jax.random.uniform(next(ks), (DEPTH, DN_HEADS), jnp.float32,
                                    math.log(1e-3), math.log(1e-1)))
    dn_dt_bias = dt + jnp.log(-jnp.expm1(-dt))
    dn_norm_w = gain((DEPTH, DN_HEAD_DIM))
    w_branch = nrm((DEPTH, 2, SGU_WIDTH, D_MODEL), SGU_WIDTH ** -0.5)
    w_out = nrm((DEPTH, D_MODEL, D_MODEL), D_MODEL ** -0.5)
    norm_ffn = gain((DEPTH, D_MODEL))
    ffn_w_gate = nrm((N_DENSE_LAYERS, D_MODEL, FFN_DIM), D_MODEL ** -0.5)
    ffn_w_up = nrm((N_DENSE_LAYERS, D_MODEL, FFN_DIM), D_MODEL ** -0.5)
    ffn_w_down = nrm((N_DENSE_LAYERS, FFN_DIM, D_MODEL), FFN_DIM ** -0.5)
    moe_router = nrm((N_MOE_LAYERS, D_MODEL, N_EXPERTS), D_MODEL ** -0.5)
    moe_w_gate = nrm((N_MOE_LAYERS, N_EXPERTS, D_MODEL, EXPERT_DIM), D_MODEL ** -0.5)
    moe_w_up = nrm((N_MOE_LAYERS, N_EXPERTS, D_MODEL, EXPERT_DIM), D_MODEL ** -0.5)
    moe_w_down = nrm((N_MOE_LAYERS, N_EXPERTS, EXPERT_DIM, D_MODEL), EXPERT_DIM ** -0.5)
    norm_ple = gain((DEPTH, D_MODEL))
    ple_w_gate = nrm((DEPTH, D_MODEL, D_MODEL), D_MODEL ** -0.5)
    ple_b_gate = nrm((DEPTH, D_MODEL), 0.02)
    ple_w_proj = nrm((DEPTH, PLE_DIM, D_MODEL), PLE_DIM ** -0.5)
    norm_final = gain((D_MODEL,))
    return {'x': x, 'p': p, 'norm_mix': norm_mix, 'w_in': w_in, 'sgu_ln_g': sgu_ln_g,
            'sgu_ln_b': sgu_ln_b, 'sgu_w': sgu_w, 'sgu_b': sgu_b, 'dn_conv_w': dn_conv_w,
            'dn_a_log': dn_a_log, 'dn_dt_bias': dn_dt_bias, 'dn_norm_w': dn_norm_w,
            'w_branch': w_branch, 'w_out': w_out, 'norm_ffn': norm_ffn,
            'ffn_w_gate': ffn_w_gate, 'ffn_w_up': ffn_w_up, 'ffn_w_down': ffn_w_down,
            'moe_router': moe_router, 'moe_w_gate': moe_w_gate, 'moe_w_up': moe_w_up,
            'moe_w_down': moe_w_down, 'norm_ple': norm_ple, 'ple_w_gate': ple_w_gate,
            'ple_b_gate': ple_b_gate, 'ple_w_proj': ple_w_proj, 'norm_final': norm_final}


def reference(x, p, norm_mix, w_in, sgu_ln_g, sgu_ln_b, sgu_w, sgu_b, dn_conv_w, dn_a_log,
              dn_dt_bias, dn_norm_w, w_branch, w_out, norm_ffn, ffn_w_gate, ffn_w_up,
              ffn_w_down, moe_router, moe_w_gate, moe_w_up, moe_w_down, norm_ple,
              ple_w_gate, ple_b_gate, ple_w_proj, norm_final):
    h = x
    for i in range(DEPTH):
        hn = rms_norm(h, norm_mix[i])
        h = h + hybrid_mixer(hn, w_in[i], sgu_ln_g[i], sgu_ln_b[i], sgu_w[i], sgu_b[i],
                             dn_conv_w[i], dn_a_log[i], dn_dt_bias[i], dn_norm_w[i],
                             w_branch[i], w_out[i])
        hn = rms_norm(h, norm_ffn[i])
        j = i // 2
        if i % 2 == 0:
            h = h + swiglu(hn, ffn_w_gate[j], ffn_w_up[j], ffn_w_down[j])
        else:
            h = h + moe_swiglu(hn, moe_router[j], moe_w_gate[j], moe_w_up[j], moe_w_down[j])
        hn = rms_norm(h, norm_ple[i])
        gate = jax.nn.sigmoid(hn @ ple_w_gate[i] + ple_b_gate[i])
        h = h + gate * (p[i] @ ple_w_proj[i])
    return rms_norm(h, norm_final)
```

```python
import functools
import math

import jax
import jax.numpy as jnp
from jax import lax
from jax.experimental import pallas as pl
from jax.experimental.pallas import tpu as pltpu

F32 = jnp.float32
BF16 = jnp.bfloat16

D_MODEL = 2048
SGU_CHUNK = 128
SGU_GROUPS = 8
SGU_WIDTH = 1024
DN_HEADS = 8
DN_HEAD_DIM = 128
DN_WIDTH = 1024
DN_CONV = 4
DN_CHUNK = 64
N_EXPERTS = 8
TOP_K = 2
NORM_EPS = 1e-6

V7X_VMEM_LIMIT = 56 * 1024 * 1024
CAST_ROWS = 256
MOE_TM = 256


def _params(n_axes):
    return pltpu.CompilerParams(
        dimension_semantics=("arbitrary",) * n_axes,
        vmem_limit_bytes=V7X_VMEM_LIMIT)


def _sigmoid(x):
    return 1.0 / (1.0 + jnp.exp(-x))


def _silu(x):
    return x * _sigmoid(x)


def _gelu_tanh(x):
    c = math.sqrt(2.0 / math.pi)
    return 0.5 * x * (1.0 + jnp.tanh(c * (x + 0.044715 * (x * x * x))))


def _cast_rows(w_ref, wb_ref):
    k = w_ref.shape[0]
    ch = min(CAST_ROWS, k)

    def body(c, carry):
        r = pl.multiple_of(c * ch, ch)
        wb_ref[pl.ds(r, ch), :] = w_ref[pl.ds(r, ch), :].astype(BF16)
        return carry

    lax.fori_loop(0, k // ch, body, 0)


def _rmsnorm_kernel(h_ref, g_ref, o_ref):
    x = h_ref[...]
    ms = jnp.mean(x * x, axis=-1, keepdims=True)
    o_ref[...] = (x * lax.rsqrt(ms + NORM_EPS) * g_ref[...]).astype(o_ref.dtype)


def rmsnorm(h, g, layer, out_dtype, tm=512):
    m, d = h.shape
    if g.ndim == 1:
        g = g.reshape(1, 1, d)
        layer = 0
    else:
        g = g.reshape(g.shape[0], 1, d)
    return pl.pallas_call(
        _rmsnorm_kernel,
        grid=(m // tm,),
        in_specs=[pl.BlockSpec((tm, d), lambda i: (i, 0)),
                  pl.BlockSpec((None, 1, d), lambda i: (layer, 0, 0))],
        out_specs=pl.BlockSpec((tm, d), lambda i: (i, 0)),
        out_shape=jax.ShapeDtypeStruct((m, d), out_dtype),
        compiler_params=_params(1),
        name="rmsnorm",
    )(h, g)


def _fused_mm_kernel(*refs, n_x, n_w, w_x, n_e, epilogue):
    x_refs = refs[:n_x]
    w_refs = refs[n_x:n_x + n_w]
    e_refs = refs[n_x + n_w:n_x + n_w + n_e]
    o_ref = refs[n_x + n_w + n_e]
    wb_refs = refs[n_x + n_w + n_e + 1:]

    @pl.when(pl.program_id(1) == 0)
    def _():
        for w_ref, wb_ref in zip(w_refs, wb_refs):
            _cast_rows(w_ref, wb_ref)

    xs = [x_ref[...].astype(BF16) for x_ref in x_refs]
    accs = [jnp.dot(xs[w_x[l]], wb_refs[l][...], preferred_element_type=F32)
            for l in range(n_w)]
    o_ref[...] = epilogue(accs, [e[...] for e in e_refs], pl.program_id(0)).astype(o_ref.dtype)


def fused_mm(xs, ws, extras, epilogue, n_out, out_dtype, tm, tn, name):
    m = xs[0][0].shape[-2]
    grid = (n_out // tn, m // tm)
    in_specs, args = [], []
    for arr, lead in xs:
        k = arr.shape[-1]
        if arr.ndim == 2:
            in_specs.append(pl.BlockSpec((tm, k), lambda j, i: (i, 0)))
        else:
            in_specs.append(pl.BlockSpec((None, tm, k), lambda j, i, lead=lead: (lead, i, 0)))
        args.append(arr)
    scratch = []
    for arr, lead, off, _ in ws:
        k = arr.shape[-2]
        nlead = arr.ndim - 2
        lead = tuple(lead) if nlead else ()
        in_specs.append(pl.BlockSpec(
            (None,) * nlead + (k, tn),
            lambda j, i, lead=lead, off=off: lead + (0, j + off)))
        args.append(arr)
        scratch.append(pltpu.VMEM((k, tn), BF16))
    for arr, lead, kind, off in extras:
        if kind == "tile":
            if arr.ndim == 2:
                in_specs.append(pl.BlockSpec((tm, tn), lambda j, i, off=off: (i, j + off)))
            else:
                in_specs.append(pl.BlockSpec(
                    (None, tm, tn), lambda j, i, lead=lead, off=off: (lead, i, j + off)))
        else:
            in_specs.append(pl.BlockSpec(
                (None, 1, tn), lambda j, i, lead=lead, off=off: (lead, 0, j + off)))
        args.append(arr)
    kern = functools.partial(
        _fused_mm_kernel, n_x=len(xs), n_w=len(ws), w_x=tuple(w[3] for w in ws),
        n_e=len(extras), epilogue=epilogue)
    return pl.pallas_call(
        kern,
        grid=grid,
        in_specs=in_specs,
        out_specs=pl.BlockSpec((tm, tn), lambda j, i: (i, j)),
        out_shape=jax.ShapeDtypeStruct((m, n_out), out_dtype),
        scratch_shapes=scratch,
        compiler_params=_params(2),
        name=name,
    )(*args)


def _ep_proj(tn):
    n_gelu = (2 * SGU_WIDTH) // tn

    def ep(accs, extras, j):
        a = accs[0]
        return jnp.where(j < n_gelu, _gelu_tanh(a), a)
    return ep


def _ep_sigmoid(accs, extras, j):
    return _sigmoid(accs[0])


def _ep_plain(accs, extras, j):
    return accs[0]


def _ep_merge(accs, extras, j):
    return extras[0].astype(F32) * accs[0] + extras[1].astype(F32) * accs[1]


def _ep_residual(accs, extras, j):
    return extras[0] + accs[0]


def _ep_swiglu(accs, extras, j):
    return _silu(accs[0]) * accs[1]


def _ep_ple(accs, extras, j):
    return extras[0] + _sigmoid(accs[0] + extras[1]) * accs[1]


def _sgu_kernel(u_ref, v_ref, lng_ref, lnb_ref, w_ref, bt_ref, o_ref, wm_ref):
    c = SGU_CHUNK

    @pl.when(pl.program_id(0) == 0)
    def _():
        ti = lax.broadcasted_iota(jnp.int32, (c, c), 0)
        si = lax.broadcasted_iota(jnp.int32, (c, c), 1)
        for g in range(SGU_GROUPS):
            wm_ref[g] = jnp.where(si <= ti, w_ref[g], 0.0).astype(BF16)

    v = v_ref[...].astype(F32)
    mu = jnp.mean(v, axis=-1, keepdims=True)
    vc = v - mu
    var = jnp.mean(vc * vc, axis=-1, keepdims=True)
    vln = (vc * lax.rsqrt(var + NORM_EPS) * lng_ref[...] + lnb_ref[...]).astype(BF16)
    bt = bt_ref[...]
    for g in range(SGU_GROUPS):
        sl = slice(g * c, (g + 1) * c)
        mixed = jnp.dot(wm_ref[g], vln[:, sl], preferred_element_type=F32) + bt[:, g:g + 1]
        o_ref[:, sl] = (u_ref[:, sl].astype(F32) * mixed).astype(o_ref.dtype)


def sgu(proj, ln_g, ln_b, w_s, b_s_t, layer):
    m = proj.shape[0]
    c = SGU_CHUNK
    return pl.pallas_call(
        _sgu_kernel,
        grid=(m // c,),
        in_specs=[
            pl.BlockSpec((c, SGU_WIDTH), lambda i: (i, 0)),
            pl.BlockSpec((c, SGU_WIDTH), lambda i: (i, 1)),
            pl.BlockSpec((None, 1, SGU_WIDTH), lambda i: (layer, 0, 0)),
            pl.BlockSpec((None, 1, SGU_WIDTH), lambda i: (layer, 0, 0)),
            pl.BlockSpec((None, SGU_GROUPS, c, c), lambda i: (layer, 0, 0, 0)),
            pl.BlockSpec((None, c, SGU_GROUPS), lambda i: (layer, 0, 0)),
        ],
        out_specs=pl.BlockSpec((c, SGU_WIDTH), lambda i: (i, 0)),
        out_shape=jax.ShapeDtypeStruct((m, SGU_WIDTH), BF16),
        scratch_shapes=[pltpu.VMEM((SGU_GROUPS, c, c), BF16)],
        compiler_params=_params(1),
        name="sgu",
    )(proj, proj, ln_g, ln_b, w_s, b_s_t)


def _bdot(a, b):
    return jnp.dot(a.astype(BF16), b.astype(BF16), preferred_element_type=F32)


def _bdot_nt(a, b):
    return lax.dot_general(a.astype(BF16), b.astype(BF16), (((1,), (1,)), ((), ())),
                           preferred_element_type=F32)


def _gdn_kernel(q_ref, k_ref, v_ref, z_ref, sm_ref, cw_ref, alog_ref, dtb_ref, nw_ref,
                o_ref, s_ref, prev_ref):
    c = DN_CHUNK
    hd = DN_HEAD_DIM
    w = DN_WIDTH

    @pl.when(pl.program_id(1) == 0)
    def _():
        s_ref[...] = jnp.zeros(s_ref.shape, F32)
        prev_ref[...] = jnp.zeros(prev_ref.shape, F32)

    row_w = lax.broadcasted_iota(jnp.int32, (c, w), 0)

    def conv_silu(x_ref, sec):
        x = x_ref[...].astype(F32)
        prev = prev_ref[sec]
        cw = cw_ref[:, sec * w:(sec + 1) * w]
        acc = cw[DN_CONV - 1:DN_CONV, :] * x
        for s in range(1, DN_CONV):
            shifted = jnp.where(row_w < s, pltpu.roll(prev, s, 0), pltpu.roll(x, s, 0))
            acc = acc + cw[DN_CONV - 1 - s:DN_CONV - s, :] * shifted
        prev_ref[sec] = x
        return _silu(acc)

    q_all = conv_silu(q_ref, 0)
    k_all = conv_silu(k_ref, 1)
    v_all = conv_silu(v_ref, 2)

    sm = sm_ref[...]
    beta_all = _sigmoid(sm)
    xa = sm + dtb_ref[...]
    softplus = jnp.maximum(xa, 0.0) + jnp.log(1.0 + jnp.exp(-jnp.abs(xa)))
    g = -jnp.exp(alog_ref[...]) * softplus
    row_s = lax.broadcasted_iota(jnp.int32, (c, 128), 0)
    s = 1
    while s < c:
        g = g + jnp.where(row_s >= s, pltpu.roll(g, s, 0), 0.0)
        s *= 2
    g_t = g.T

    ii = lax.broadcasted_iota(jnp.int32, (c, c), 0)
    jj = lax.broadcasted_iota(jnp.int32, (c, c), 1)
    tri = ii >= jj
    strict = ii > jj
    eye = (ii == jj).astype(F32)
    nw = nw_ref[...]
    scale = hd ** -0.5

    for h in range(DN_HEADS):
        sl = slice(h * hd, (h + 1) * hd)
        q = q_all[:, sl]
        k = k_all[:, sl]
        v = v_all[:, sl]
        q = q * (lax.rsqrt(jnp.sum(q * q, axis=-1, keepdims=True) + NORM_EPS) * scale)
        k = k * lax.rsqrt(jnp.sum(k * k, axis=-1, keepdims=True) + NORM_EPS)
        beta = beta_all[:, h:h + 1]
        gcol = g[:, DN_HEADS + h:DN_HEADS + h + 1]
        grow = g_t[DN_HEADS + h:DN_HEADS + h + 1, :]
        glast = gcol[c - 1:c, :]
        decay = jnp.exp(jnp.where(tri, gcol - grow, -1e30))
        egc = jnp.exp(gcol)
        kb = k * beta
        vb = v * beta
        a_mat = jnp.where(strict, _bdot_nt(kb, k) * decay, 0.0)
        attn = jnp.where(tri, _bdot_nt(q, k) * decay, 0.0)

        bk = -a_mat
        qm = eye + bk
        bk = _bdot(bk, bk)
        for lvl in range(1, 6):
            if lvl < 5:
                r = _bdot(jnp.concatenate([qm, bk], axis=0), bk)
                qm = qm + r[:c]
                bk = r[c:]
            else:
                qm = qm + _bdot(qm, bk)

        sol = _bdot(qm, jnp.concatenate([vb, kb * egc], axis=1))
        u_val = sol[:, :hd]
        w_cum = sol[:, hd:]
        q_g = q * egc
        k_d = k * jnp.exp(glast - gcol)
        state = s_ref[h]
        r2 = _bdot(jnp.concatenate([w_cum, q_g], axis=0), state)
        v_new = u_val - r2[:c]
        o = r2[c:] + _bdot(attn, v_new)
        s_ref[h] = state * jnp.exp(glast) + _bdot(k_d.T, v_new)

        o = o * lax.rsqrt(jnp.mean(o * o, axis=-1, keepdims=True) + NORM_EPS)
        o = o * nw * _silu(z_ref[:, sl].astype(F32))
        o_ref[:, sl] = o.astype(o_ref.dtype)


def gdn(proj, small, conv_w, alog_row, dtb_row, norm_w, layer, bsz, seq):
    m = proj.shape[0]
    c = DN_CHUNK
    n = seq // c
    w = DN_WIDTH
    qkv0 = (2 * SGU_WIDTH) // w

    def rows(b, t):
        return b * n + t

    return pl.pallas_call(
        _gdn_kernel,
        grid=(bsz, n),
        in_specs=[
            pl.BlockSpec((c, w), lambda b, t: (rows(b, t), qkv0)),
            pl.BlockSpec((c, w), lambda b, t: (rows(b, t), qkv0 + 1)),
            pl.BlockSpec((c, w), lambda b, t: (rows(b, t), qkv0 + 2)),
            pl.BlockSpec((c, w), lambda b, t: (rows(b, t), qkv0 + 3)),
            pl.BlockSpec((c, 128), lambda b, t: (rows(b, t), 0)),
            pl.BlockSpec((None, DN_CONV, 3 * w), lambda b, t: (layer, 0, 0)),
            pl.BlockSpec((None, 1, 128), lambda b, t: (layer, 0, 0)),
            pl.BlockSpec((None, 1, 128), lambda b, t: (layer, 0, 0)),
            pl.BlockSpec((None, 1, DN_HEAD_DIM), lambda b, t: (layer, 0, 0)),
        ],
        out_specs=pl.BlockSpec((c, w), lambda b, t: (rows(b, t), 0)),
        out_shape=jax.ShapeDtypeStruct((m, w), BF16),
        scratch_shapes=[pltpu.VMEM((DN_HEADS, DN_HEAD_DIM, DN_HEAD_DIM), F32),
                        pltpu.VMEM((3, c, w), F32)],
        compiler_params=_params(2),
        name="gdn",
    )(proj, proj, proj, proj, small, conv_w, alog_row, dtb_row, norm_w)


def _route_kernel(h_ref, g_ref, r_ref, hn_ref, ti_ref, tw_ref):
    x = h_ref[...]
    ms = jnp.mean(x * x, axis=-1, keepdims=True)
    hn = x * lax.rsqrt(ms + NORM_EPS) * g_ref[...]
    hn_ref[...] = hn.astype(hn_ref.dtype)
    logits = jnp.dot(hn, r_ref[...], preferred_element_type=F32,
                     precision=lax.Precision.HIGHEST)
    lane = lax.broadcasted_iota(jnp.int32, logits.shape, 1)
    neg = jnp.float32(-3.0e38)
    logits = jnp.where(lane < N_EXPERTS, logits, neg)
    m1 = jnp.max(logits, axis=-1, keepdims=True)
    i1 = jnp.min(jnp.where(logits == m1, lane, 128), axis=-1, keepdims=True)
    rest = jnp.where(lane == i1, neg, logits)
    m2 = jnp.max(rest, axis=-1, keepdims=True)
    i2 = jnp.min(jnp.where(rest == m2, lane, 128), axis=-1, keepdims=True)
    e2 = jnp.exp(m2 - m1)
    w1 = 1.0 / (1.0 + e2)
    w2 = e2 / (1.0 + e2)
    ti_ref[...] = jnp.where(lane == 0, i1, jnp.where(lane == 1, i2, 0))
    tw_ref[...] = jnp.where(lane == 0, w1, jnp.where(lane == 1, w2, 0.0))


def route(h, g, layer, router_pad, j, tm=256):
    m, d = h.shape
    g = g.reshape(g.shape[0], 1, d)
    return pl.pallas_call(
        _route_kernel,
        grid=(m // tm,),
        in_specs=[pl.BlockSpec((tm, d), lambda i: (i, 0)),
                  pl.BlockSpec((None, 1, d), lambda i: (layer, 0, 0)),
                  pl.BlockSpec((None, d, 128), lambda i: (j, 0, 0))],
        out_specs=[pl.BlockSpec((tm, d), lambda i: (i, 0)),
                   pl.BlockSpec((tm, 128), lambda i: (i, 0)),
                   pl.BlockSpec((tm, 128), lambda i: (i, 0))],
        out_shape=[jax.ShapeDtypeStruct((m, d), BF16),
                   jax.ShapeDtypeStruct((m, 128), jnp.int32),
                   jax.ShapeDtypeStruct((m, 128), F32)],
        compiler_params=_params(1),
        name="route",
    )(h, g, router_pad)


def _expert_changed(te_ref, r):
    prev = te_ref[jnp.maximum(r - 1, 0)]
    return jnp.logical_or(r == 0, te_ref[r] != prev)


def _moe_up_kernel(te_ref, na_ref, x_ref, wg_ref, wu_ref, o_ref, wgb_ref, wub_ref):
    r = pl.program_id(1)

    @pl.when(_expert_changed(te_ref, r))
    def _():
        _cast_rows(wg_ref, wgb_ref)
        _cast_rows(wu_ref, wub_ref)

    @pl.when(r < na_ref[0])
    def _():
        x = x_ref[...]
        a = jnp.dot(x, wgb_ref[...], preferred_element_type=F32)
        b = jnp.dot(x, wub_ref[...], preferred_element_type=F32)
        o_ref[...] = (_silu(a) * b).astype(o_ref.dtype)

    @pl.when(r >= na_ref[0])
    def _():
        o_ref[...] = jnp.zeros(o_ref.shape, o_ref.dtype)


def _moe_down_kernel(te_ref, na_ref, x_ref, wd_ref, rw_ref, o_ref, wdb_ref):
    r = pl.program_id(1)

    @pl.when(_expert_changed(te_ref, r))
    def _():
        _cast_rows(wd_ref, wdb_ref)

    @pl.when(r < na_ref[0])
    def _():
        y = jnp.dot(x_ref[...], wdb_ref[...], preferred_element_type=F32)
        o_ref[...] = (y * rw_ref[...]).astype(o_ref.dtype)

    @pl.when(r >= na_ref[0])
    def _():
        o_ref[...] = jnp.zeros(o_ref.shape, o_ref.dtype)


def moe_experts(xs, tile_expert, n_active, row_w, w_gate, w_up, w_down, j, tf=512, tn=512):
    rows, d = xs.shape
    tm = MOE_TM
    n_tiles = rows // tm
    f = w_gate.shape[-1]
    hmid = pl.pallas_call(
        _moe_up_kernel,
        grid_spec=pltpu.PrefetchScalarGridSpec(
            num_scalar_prefetch=2,
            grid=(f // tf, n_tiles),
            in_specs=[
                pl.BlockSpec((tm, d), lambda c, r, te, na: (r, 0)),
                pl.BlockSpec((None, None, d, tf), lambda c, r, te, na: (j, te[r], 0, c)),
                pl.BlockSpec((None, None, d, tf), lambda c, r, te, na: (j, te[r], 0, c)),
            ],
            out_specs=pl.BlockSpec((tm, tf), lambda c, r, te, na: (r, c)),
            scratch_shapes=[pltpu.VMEM((d, tf), BF16), pltpu.VMEM((d, tf), BF16)],
        ),
        out_shape=jax.ShapeDtypeStruct((rows, f), BF16),
        compiler_params=_params(2),
        name="moe_up",
    )(tile_expert, n_active, xs, w_gate, w_up)
    return pl.pallas_call(
        _moe_down_kernel,
        grid_spec=pltpu.PrefetchScalarGridSpec(
            num_scalar_prefetch=2,
            grid=(d // tn, n_tiles),
            in_specs=[
                pl.BlockSpec((tm, f), lambda c, r, te, na: (r, 0)),
                pl.BlockSpec((None, None, f, tn), lambda c, r, te, na: (j, te[r], 0, c)),
                pl.BlockSpec((tm, 1), lambda c, r, te, na: (r, 0)),
            ],
            out_specs=pl.BlockSpec((tm, tn), lambda c, r, te, na: (r, c)),
            scratch_shapes=[pltpu.VMEM((f, tn), BF16)],
        ),
        out_shape=jax.ShapeDtypeStruct((rows, d), F32),
        compiler_params=_params(2),
        name="moe_down",
    )(tile_expert, n_active, hmid, w_down, row_w)


def moe_layer(h, norm_g, layer, router_pad, w_gate, w_up, w_down, j):
    m, d = h.shape
    tm = MOE_TM
    hn, ti, tw = route(h, norm_g, layer, router_pad, j)
    top_i = ti[:, :TOP_K].reshape(-1)
    top_w = tw[:, :TOP_K].reshape(-1)
    n_pairs = m * TOP_K
    n_tiles = n_pairs // tm + N_EXPERTS
    onehot = (top_i[:, None] == jnp.arange(N_EXPERTS, dtype=jnp.int32)[None, :]).astype(jnp.int32)
    csum = jnp.cumsum(onehot, axis=0)
    rank = jnp.sum((csum - onehot) * onehot, axis=1)
    counts = csum[-1]
    tiles_per = (counts + tm - 1) // tm
    tile_end = jnp.cumsum(tiles_per)
    tile_start = tile_end - tiles_per
    pos = jnp.sum(onehot * tile_start[None, :], axis=1) * tm + rank
    n_active = tile_end[-1:].astype(jnp.int32)
    tile_ids = jnp.arange(n_tiles, dtype=jnp.int32)
    tile_expert = jnp.sum((tile_ids[:, None] >= tile_end[None, :]).astype(jnp.int32), axis=1)
    last_expert = tile_expert[jnp.maximum(n_active[0] - 1, 0)]
    tile_expert = jnp.where(tile_ids < n_active[0], tile_expert, last_expert).astype(jnp.int32)
    rows = n_tiles * tm
    pair_tok = jnp.arange(n_pairs, dtype=jnp.int32) // TOP_K
    row_tok = jnp.zeros((rows,), jnp.int32).at[pos].set(pair_tok)
    row_w = jnp.zeros((rows,), F32).at[pos].set(top_w)
    xs = jnp.take(hn, row_tok, axis=0)
    ys = moe_experts(xs, tile_expert, n_active, row_w.reshape(rows, 1), w_gate, w_up, w_down, j)
    pos2 = pos.reshape(m, TOP_K)
    return h + jnp.take(ys, pos2[:, 0], axis=0) + jnp.take(ys, pos2[:, 1], axis=0)


def kernel(x, p, norm_mix, w_in, sgu_ln_g, sgu_ln_b, sgu_w, sgu_b, dn_conv_w, dn_a_log,
           dn_dt_bias, dn_norm_w, w_branch, w_out, norm_ffn, ffn_w_gate, ffn_w_up,
           ffn_w_down, moe_router, moe_w_gate, moe_w_up, moe_w_down, norm_ple,
           ple_w_gate, ple_b_gate, ple_w_proj, norm_final):
    bsz, seq, d = x.shape
    depth = p.shape[0]
    m = bsz * seq
    h = x.reshape(m, d)
    p2 = p.reshape(depth, m, p.shape[-1])
    main_w = 2 * SGU_WIDTH + 4 * DN_WIDTH
    small0 = main_w
    gates0 = main_w + 2 * DN_HEADS
    pad_h = 128 - 2 * DN_HEADS
    w_small = jnp.pad(w_in[:, :, small0:gates0], ((0, 0), (0, 0), (0, pad_h)))
    w_gates = w_in[:, :, gates0:]
    alog_row = jnp.pad(dn_a_log, ((0, 0), (DN_HEADS, 128 - 2 * DN_HEADS))).reshape(depth, 1, 128)
    dtb_row = jnp.pad(dn_dt_bias, ((0, 0), (DN_HEADS, 128 - 2 * DN_HEADS))).reshape(depth, 1, 128)
    ln_g = sgu_ln_g.reshape(depth, 1, SGU_WIDTH)
    ln_b = sgu_ln_b.reshape(depth, 1, SGU_WIDTH)
    sgu_b_t = jnp.swapaxes(sgu_b, 1, 2)
    nw = dn_norm_w.reshape(depth, 1, DN_HEAD_DIM)
    router_pad = jnp.pad(moe_router, ((0, 0), (0, 0), (0, 128 - N_EXPERTS)))
    ple_b = ple_b_gate.reshape(depth, 1, d)

    for i in range(depth):
        hn = rmsnorm(h, norm_mix, i, BF16)
        proj = fused_mm([(hn, 0)], [(w_in, (i,), 0, 0)], [], _ep_proj(512), main_w, BF16,
                        1024, 512, "proj_main")
        gates = fused_mm([(hn, 0)], [(w_gates, (i,), 0, 0)], [], _ep_sigmoid, 2 * d, BF16,
                         1024, 512, "proj_gates")
        small = fused_mm([(hn, 0)], [(w_small, (i,), 0, 0)], [], _ep_plain, 128, F32,
                         1024, 128, "proj_small")
        y_a = sgu(proj, ln_g, ln_b, sgu_w, sgu_b_t, i)
        y_b = gdn(proj, small, dn_conv_w, alog_row, dtb_row, nw, i, bsz, seq)
        merged = fused_mm([(y_a, 0), (y_b, 0)],
                          [(w_branch, (i, 0), 0, 0), (w_branch, (i, 1), 0, 1)],
                          [(gates, 0, "tile", 0), (gates, 0, "tile", d // 512)],
                          _ep_merge, d, BF16, 1024, 512, "merge")
        h = fused_mm([(merged, 0)], [(w_out, (i,), 0, 0)], [(h, 0, "tile", 0)],
                     _ep_residual, d, F32, 1024, 512, "out_proj")
        j = i // 2
        if i % 2 == 0:
            hn = rmsnorm(h, norm_ffn, i, BF16)
            hmid = fused_mm([(hn, 0)], [(ffn_w_gate, (j,), 0, 0), (ffn_w_up, (j,), 0, 0)], [],
                            _ep_swiglu, ffn_w_gate.shape[-1], BF16, 1024, 512, "ffn_up")
            h = fused_mm([(hmid, 0)], [(ffn_w_down, (j,), 0, 0)], [(h, 0, "tile", 0)],
                         _ep_residual, d, F32, 512, 256, "ffn_down")
        else:
            h = moe_layer(h, norm_ffn, i, router_pad, moe_w_gate, moe_w_up, moe_w_down, j)
        hn = rmsnorm(h, norm_ple, i, BF16)
        h = fused_mm([(hn, 0), (p2, i)],
                     [(ple_w_gate, (i,), 0, 0), (ple_w_proj, (i,), 0, 1)],
                     [(h, 0, "tile", 0), (ple_b, i, "row", 0)],
                     _ep_ple, d, F32, 1024, 512, "ple")
    out = rmsnorm(h, norm_final, 0, F32)
    return out.reshape(bsz, seq, d)
```

```python
import functools
import math

import jax
import jax.numpy as jnp
from jax import lax
from jax.experimental import pallas as pl
from jax.experimental.pallas import tpu as pltpu

F32 = jnp.float32
BF16 = jnp.bfloat16

D_MODEL = 2048
SGU_CHUNK = 128
SGU_GROUPS = 8
SGU_WIDTH = 1024
DN_HEADS = 8
DN_HEAD_DIM = 128
DN_WIDTH = 1024
DN_CONV = 4
DN_CHUNK = 64
N_EXPERTS = 8
TOP_K = 2
NORM_EPS = 1e-6

V7X_VMEM_LIMIT = 56 * 1024 * 1024
CAST_ROWS = 256
MOE_TM = 256
MOE_DMA_ROWS = 512
MOE_COMBINE_TM = 256


def _params(n_axes):
    return pltpu.CompilerParams(
        dimension_semantics=("arbitrary",) * n_axes,
        vmem_limit_bytes=V7X_VMEM_LIMIT)


def _sigmoid(x):
    return 1.0 / (1.0 + jnp.exp(-x))


def _silu(x):
    return x * _sigmoid(x)


def _gelu_tanh(x):
    c = math.sqrt(2.0 / math.pi)
    return 0.5 * x * (1.0 + jnp.tanh(c * (x + 0.044715 * (x * x * x))))


def _cast_rows(w_ref, wb_ref):
    k = w_ref.shape[0]
    ch = min(CAST_ROWS, k)

    def body(c, carry):
        r = pl.multiple_of(c * ch, ch)
        wb_ref[pl.ds(r, ch), :] = w_ref[pl.ds(r, ch), :].astype(BF16)
        return carry

    lax.fori_loop(0, k // ch, body, 0)


def _rmsnorm_kernel(h_ref, g_ref, o_ref):
    x = h_ref[...]
    ms = jnp.mean(x * x, axis=-1, keepdims=True)
    o_ref[...] = (x * lax.rsqrt(ms + NORM_EPS) * g_ref[...]).astype(o_ref.dtype)


def rmsnorm(h, g, layer, out_dtype, tm=512):
    m, d = h.shape
    if g.ndim == 1:
        g = g.reshape(1, 1, d)
        layer = 0
    else:
        g = g.reshape(g.shape[0], 1, d)
    return pl.pallas_call(
        _rmsnorm_kernel,
        grid=(m // tm,),
        in_specs=[pl.BlockSpec((tm, d), lambda i: (i, 0)),
                  pl.BlockSpec((None, 1, d), lambda i: (layer, 0, 0))],
        out_specs=pl.BlockSpec((tm, d), lambda i: (i, 0)),
        out_shape=jax.ShapeDtypeStruct((m, d), out_dtype),
        compiler_params=_params(1),
        name="rmsnorm",
    )(h, g)


def _fused_mm_kernel(*refs, n_x, n_w, w_x, n_e, epilogue):
    x_refs = refs[:n_x]
    w_refs = refs[n_x:n_x + n_w]
    e_refs = refs[n_x + n_w:n_x + n_w + n_e]
    o_ref = refs[n_x + n_w + n_e]
    wb_refs = refs[n_x + n_w + n_e + 1:]

    @pl.when(pl.program_id(1) == 0)
    def _():
        for w_ref, wb_ref in zip(w_refs, wb_refs):
            _cast_rows(w_ref, wb_ref)

    xs = [x_ref[...].astype(BF16) for x_ref in x_refs]
    accs = [jnp.dot(xs[w_x[l]], wb_refs[l][...], preferred_element_type=F32)
            for l in range(n_w)]
    o_ref[...] = epilogue(accs, [e[...] for e in e_refs], pl.program_id(0)).astype(o_ref.dtype)


def fused_mm(xs, ws, extras, epilogue, n_out, out_dtype, tm, tn, name):
    m = xs[0][0].shape[-2]
    grid = (n_out // tn, m // tm)
    in_specs, args = [], []
    for arr, lead in xs:
        k = arr.shape[-1]
        if arr.ndim == 2:
            in_specs.append(pl.BlockSpec((tm, k), lambda j, i: (i, 0)))
        else:
            in_specs.append(pl.BlockSpec((None, tm, k), lambda j, i, lead=lead: (lead, i, 0)))
        args.append(arr)
    scratch = []
    for arr, lead, off, _ in ws:
        k = arr.shape[-2]
        nlead = arr.ndim - 2
        lead = tuple(lead) if nlead else ()
        in_specs.append(pl.BlockSpec(
            (None,) * nlead + (k, tn),
            lambda j, i, lead=lead, off=off: lead + (0, j + off)))
        args.append(arr)
        scratch.append(pltpu.VMEM((k, tn), BF16))
    for arr, lead, kind, off in extras:
        if kind == "tile":
            if arr.ndim == 2:
                in_specs.append(pl.BlockSpec((tm, tn), lambda j, i, off=off: (i, j + off)))
            else:
                in_specs.append(pl.BlockSpec(
                    (None, tm, tn), lambda j, i, lead=lead, off=off: (lead, i, j + off)))
        else:
            in_specs.append(pl.BlockSpec(
                (None, 1, tn), lambda j, i, lead=lead, off=off: (lead, 0, j + off)))
        args.append(arr)
    kern = functools.partial(
        _fused_mm_kernel, n_x=len(xs), n_w=len(ws), w_x=tuple(w[3] for w in ws),
        n_e=len(extras), epilogue=epilogue)
    return pl.pallas_call(
        kern,
        grid=grid,
        in_specs=in_specs,
        out_specs=pl.BlockSpec((tm, tn), lambda j, i: (i, j)),
        out_shape=jax.ShapeDtypeStruct((m, n_out), out_dtype),
        scratch_shapes=scratch,
        compiler_params=_params(2),
        name=name,
    )(*args)


def _ep_proj(tn):
    n_gelu = (2 * SGU_WIDTH) // tn

    def ep(accs, extras, j):
        a = accs[0]
        return jnp.where(j < n_gelu, _gelu_tanh(a), a)
    return ep


def _ep_sigmoid(accs, extras, j):
    return _sigmoid(accs[0])


def _ep_plain(accs, extras, j):
    return accs[0]


def _ep_merge(accs, extras, j):
    return extras[0].astype(F32) * accs[0] + extras[1].astype(F32) * accs[1]


def _ep_residual(accs, extras, j):
    return extras[0] + accs[0]


def _ep_swiglu(accs, extras, j):
    return _silu(accs[0]) * accs[1]


def _ep_ple(accs, extras, j):
    return extras[0] + _sigmoid(accs[0] + extras[1]) * accs[1]


def _sgu_kernel(u_ref, v_ref, lng_ref, lnb_ref, w_ref, bt_ref, o_ref, wm_ref):
    c = SGU_CHUNK

    @pl.when(pl.program_id(0) == 0)
    def _():
        ti = lax.broadcasted_iota(jnp.int32, (c, c), 0)
        si = lax.broadcasted_iota(jnp.int32, (c, c), 1)
        for g in range(SGU_GROUPS):
            wm_ref[g] = jnp.where(si <= ti, w_ref[g], 0.0).astype(BF16)

    v = v_ref[...].astype(F32)
    mu = jnp.mean(v, axis=-1, keepdims=True)
    vc = v - mu
    var = jnp.mean(vc * vc, axis=-1, keepdims=True)
    vln = (vc * lax.rsqrt(var + NORM_EPS) * lng_ref[...] + lnb_ref[...]).astype(BF16)
    bt = bt_ref[...]
    for g in range(SGU_GROUPS):
        sl = slice(g * c, (g + 1) * c)
        mixed = jnp.dot(wm_ref[g], vln[:, sl], preferred_element_type=F32) + bt[:, g:g + 1]
        o_ref[:, sl] = (u_ref[:, sl].astype(F32) * mixed).astype(o_ref.dtype)


def sgu(proj, ln_g, ln_b, w_s, b_s_t, layer):
    m = proj.shape[0]
    c = SGU_CHUNK
    return pl.pallas_call(
        _sgu_kernel,
        grid=(m // c,),
        in_specs=[
            pl.BlockSpec((c, SGU_WIDTH), lambda i: (i, 0)),
            pl.BlockSpec((c, SGU_WIDTH), lambda i: (i, 1)),
            pl.BlockSpec((None, 1, SGU_WIDTH), lambda i: (layer, 0, 0)),
            pl.BlockSpec((None, 1, SGU_WIDTH), lambda i: (layer, 0, 0)),
            pl.BlockSpec((None, SGU_GROUPS, c, c), lambda i: (layer, 0, 0, 0)),
            pl.BlockSpec((None, c, SGU_GROUPS), lambda i: (layer, 0, 0)),
        ],
        out_specs=pl.BlockSpec((c, SGU_WIDTH), lambda i: (i, 0)),
        out_shape=jax.ShapeDtypeStruct((m, SGU_WIDTH), BF16),
        scratch_shapes=[pltpu.VMEM((SGU_GROUPS, c, c), BF16)],
        compiler_params=_params(1),
        name="sgu",
    )(proj, proj, ln_g, ln_b, w_s, b_s_t)


def _bdot(a, b):
    return jnp.dot(a.astype(BF16), b.astype(BF16), preferred_element_type=F32)


def _bdot_nt(a, b):
    return lax.dot_general(a.astype(BF16), b.astype(BF16), (((1,), (1,)), ((), ())),
                           preferred_element_type=F32)


def _gdn_kernel(q_ref, k_ref, v_ref, z_ref, sm_ref, cw_ref, alog_ref, dtb_ref, nw_ref,
                o_ref, s_ref, prev_ref, *, nb):
    c = DN_CHUNK
    hd = DN_HEAD_DIM
    w = DN_WIDTH
    pw = 2 * hd
    n_pairs = DN_HEADS // 2

    @pl.when(pl.program_id(1) == 0)
    def _():
        s_ref[...] = jnp.zeros(s_ref.shape, F32)
        prev_ref[...] = jnp.zeros(prev_ref.shape, F32)

    row_w = lax.broadcasted_iota(jnp.int32, (c, w), 0)
    row_s = lax.broadcasted_iota(jnp.int32, (c, 128), 0)
    lane = lax.broadcasted_iota(jnp.int32, (c, 128), 1)
    left = lane < c
    jmod = jnp.where(left, lane, lane - c)
    tri = row_s >= jmod
    strict = row_s > jmod
    eye = (row_s == jmod).astype(F32)
    r256 = lax.broadcasted_iota(jnp.int32, (pw, pw), 0)
    c256 = lax.broadcasted_iota(jnp.int32, (pw, pw), 1)
    bd_mask = (r256 < hd) == (c256 < hd)
    zeros_h = jnp.zeros((c, hd), F32)
    nw = nw_ref[...]
    scale = hd ** -0.5

    def bcast2(col_a, col_b):
        return jnp.concatenate([jnp.broadcast_to(col_a, (c, hd)),
                                jnp.broadcast_to(col_b, (c, hd))], axis=1)

    def blockdiag_rows(x):
        return jnp.concatenate([jnp.where(left, x, 0.0), jnp.where(left, 0.0, x)], axis=0)

    def blockdiag_heads(x):
        return jnp.concatenate(
            [jnp.concatenate([x[:, :hd], zeros_h], axis=1),
             jnp.concatenate([zeros_h, x[:, hd:]], axis=1)], axis=0)

    def inv_norm(x):
        ss_a = jnp.sum(x[:, :hd] * x[:, :hd], axis=-1, keepdims=True)
        ss_b = jnp.sum(x[:, hd:] * x[:, hd:], axis=-1, keepdims=True)
        return bcast2(lax.rsqrt(ss_a + NORM_EPS), lax.rsqrt(ss_b + NORM_EPS))

    chains = []
    for b in range(nb):
        def conv_silu(x_ref, sec, b=b):
            x = x_ref[b].astype(F32)
            prev = prev_ref[b * 3 + sec]
            cw = cw_ref[:, sec * w:(sec + 1) * w]
            acc = cw[DN_CONV - 1:DN_CONV, :] * x
            for s in range(1, DN_CONV):
                shifted = jnp.where(row_w < s, pltpu.roll(prev, s, 0), pltpu.roll(x, s, 0))
                acc = acc + cw[DN_CONV - 1 - s:DN_CONV - s, :] * shifted
            prev_ref[b * 3 + sec] = x
            return _silu(acc)

        q_all = conv_silu(q_ref, 0)
        k_all = conv_silu(k_ref, 1)
        v_all = conv_silu(v_ref, 2)

        sm = sm_ref[b]
        beta_all = _sigmoid(sm)
        xa = sm + dtb_ref[...]
        softplus = jnp.maximum(xa, 0.0) + jnp.log(1.0 + jnp.exp(-jnp.abs(xa)))
        g = -jnp.exp(alog_ref[...]) * softplus
        s = 1
        while s < c:
            g = g + jnp.where(row_s >= s, pltpu.roll(g, s, 0), 0.0)
            s *= 2
        g_t = jnp.concatenate([g, pltpu.roll(g, 127, 1)], axis=0).T

        for p in range(n_pairs):
            a = 2 * p
            sl = slice(p * pw, (p + 1) * pw)
            q = q_all[:, sl]
            k = k_all[:, sl]
            v = v_all[:, sl]
            q = q * (inv_norm(q) * scale)
            k = k * inv_norm(k)
            gc_a = g[:, DN_HEADS + a:DN_HEADS + a + 1]
            gc_b = g[:, DN_HEADS + a + 1:DN_HEADS + a + 2]
            gl_a = gc_a[c - 1:c, :]
            gl_b = gc_b[c - 1:c, :]
            grow = g_t[DN_HEADS + a:DN_HEADS + a + 1, :]
            gcol = jnp.where(left, gc_a, gc_b)
            decay = jnp.exp(jnp.where(tri, gcol - grow, -1e30))
            beta = bcast2(beta_all[:, a:a + 1], beta_all[:, a + 1:a + 2])
            egc = bcast2(jnp.exp(gc_a), jnp.exp(gc_b))
            kb = k * beta
            chains.append(dict(
                b=b, p=p, q=q, k=k, kb=kb, vb=v * beta, kbg=kb * egc, qg=q * egc,
                kd=k * bcast2(jnp.exp(gl_a - gc_a), jnp.exp(gl_b - gc_b)),
                dl=jnp.concatenate([jnp.broadcast_to(jnp.exp(gl_a), (1, hd)),
                                    jnp.broadcast_to(jnp.exp(gl_b), (1, hd))], axis=1),
                decay=decay))

    for ch in chains:
        kq = _bdot_nt(jnp.concatenate([ch["kb"], ch["q"]], axis=0), blockdiag_heads(ch["k"]))
        ch["bk"] = -jnp.where(strict, kq[:c] * ch["decay"], 0.0)
        ch["attn"] = jnp.where(tri, kq[c:] * ch["decay"], 0.0)
        ch["qm"] = eye + ch["bk"]
    for ch in chains:
        ch["bk"] = _bdot(ch["bk"], blockdiag_rows(ch["bk"]))
    for lvl in range(1, 6):
        for ch in chains:
            if lvl < 5:
                r = _bdot(jnp.concatenate([ch["qm"], ch["bk"]], axis=0), blockdiag_rows(ch["bk"]))
                ch["qm"] = ch["qm"] + r[:c]
                ch["bk"] = r[c:]
            else:
                ch["qm"] = ch["qm"] + _bdot(ch["qm"], blockdiag_rows(ch["bk"]))
    for ch in chains:
        vb, kbg = ch["vb"], ch["kbg"]
        rhs = jnp.concatenate(
            [jnp.concatenate([vb[:, :hd], kbg[:, :hd], zeros_h, zeros_h], axis=1),
             jnp.concatenate([zeros_h, zeros_h, vb[:, hd:], kbg[:, hd:]], axis=1)], axis=0)
        sol = _bdot(ch["qm"], rhs)
        ch["u"] = jnp.concatenate([sol[:, :hd], sol[:, 2 * hd:3 * hd]], axis=1)
        ch["w"] = jnp.concatenate([sol[:, hd:2 * hd], sol[:, 3 * hd:]], axis=1)
    for ch in chains:
        ch["state"] = s_ref[ch["b"] * n_pairs + ch["p"]]
        r2 = _bdot(jnp.concatenate([ch["w"], ch["qg"]], axis=0), ch["state"])
        ch["vn"] = ch["u"] - r2[:c]
        ch["o"] = r2[c:]
    for ch in chains:
        ch["o"] = ch["o"] + _bdot(ch["attn"], blockdiag_heads(ch["vn"]))
    for ch in chains:
        upd = ch["state"] * ch["dl"] + _bdot(ch["kd"].T, ch["vn"])
        s_ref[ch["b"] * n_pairs + ch["p"]] = jnp.where(bd_mask, upd, 0.0)
    for ch in chains:
        b, p = ch["b"], ch["p"]
        for half in range(2):
            o = ch["o"][:, half * hd:(half + 1) * hd]
            sl = slice(p * pw + half * hd, p * pw + (half + 1) * hd)
            o = o * lax.rsqrt(jnp.mean(o * o, axis=-1, keepdims=True) + NORM_EPS)
            o = o * nw * _silu(z_ref[b, :, sl].astype(F32))
            o_ref[b, :, sl] = o.astype(o_ref.dtype)


def gdn(proj, small, conv_w, alog_row, dtb_row, norm_w, layer, bsz, seq, nb=1):
    m = proj.shape[0]
    c = DN_CHUNK
    n = seq // c
    w = DN_WIDTH
    qkv0 = (2 * SGU_WIDTH) // w
    proj3 = proj.reshape(bsz, seq, proj.shape[-1])
    small3 = small.reshape(bsz, seq, small.shape[-1])
    pw = 2 * DN_HEAD_DIM
    out = pl.pallas_call(
        functools.partial(_gdn_kernel, nb=nb),
        grid=(bsz // nb, n),
        in_specs=[
            pl.BlockSpec((nb, c, w), lambda b, t: (b, t, qkv0)),
            pl.BlockSpec((nb, c, w), lambda b, t: (b, t, qkv0 + 1)),
            pl.BlockSpec((nb, c, w), lambda b, t: (b, t, qkv0 + 2)),
            pl.BlockSpec((nb, c, w), lambda b, t: (b, t, qkv0 + 3)),
            pl.BlockSpec((nb, c, 128), lambda b, t: (b, t, 0)),
            pl.BlockSpec((None, DN_CONV, 3 * w), lambda b, t: (layer, 0, 0)),
            pl.BlockSpec((None, 1, 128), lambda b, t: (layer, 0, 0)),
            pl.BlockSpec((None, 1, 128), lambda b, t: (layer, 0, 0)),
            pl.BlockSpec((None, 1, DN_HEAD_DIM), lambda b, t: (layer, 0, 0)),
        ],
        out_specs=pl.BlockSpec((nb, c, w), lambda b, t: (b, t, 0)),
        out_shape=jax.ShapeDtypeStruct((bsz, seq, w), BF16),
        scratch_shapes=[pltpu.VMEM((nb * (DN_HEADS // 2), pw, pw), F32),
                        pltpu.VMEM((nb * 3, c, w), F32)],
        compiler_params=_params(2),
        name="gdn",
    )(proj3, proj3, proj3, proj3, small3, conv_w, alog_row, dtb_row, norm_w)
    return out.reshape(m, w)


def _route_kernel(h_ref, g_ref, r_ref, hn_ref, ti_ref, tw_ref):
    x = h_ref[...]
    ms = jnp.mean(x * x, axis=-1, keepdims=True)
    hn = x * lax.rsqrt(ms + NORM_EPS) * g_ref[...]
    hn_ref[...] = hn
    logits = jnp.dot(hn, r_ref[...], preferred_element_type=F32,
                     precision=lax.Precision.HIGHEST)
    lane = lax.broadcasted_iota(jnp.int32, logits.shape, 1)
    neg = jnp.float32(-3.0e38)
    logits = jnp.where(lane < N_EXPERTS, logits, neg)
    m1 = jnp.max(logits, axis=-1, keepdims=True)
    i1 = jnp.min(jnp.where(logits == m1, lane, 128), axis=-1, keepdims=True)
    rest = jnp.where(lane == i1, neg, logits)
    m2 = jnp.max(rest, axis=-1, keepdims=True)
    i2 = jnp.min(jnp.where(rest == m2, lane, 128), axis=-1, keepdims=True)
    e2 = jnp.exp(m2 - m1)
    w1 = 1.0 / (1.0 + e2)
    w2 = e2 / (1.0 + e2)
    ti_ref[...] = jnp.where(lane == 0, i1, jnp.where(lane == 1, i2, 0))
    tw_ref[...] = jnp.where(lane == 0, w1, jnp.where(lane == 1, w2, 0.0))


def route(h, g, layer, router_pad, j, tm=256):
    m, d = h.shape
    g = g.reshape(g.shape[0], 1, d)
    return pl.pallas_call(
        _route_kernel,
        grid=(m // tm,),
        in_specs=[pl.BlockSpec((tm, d), lambda i: (i, 0)),
                  pl.BlockSpec((None, 1, d), lambda i: (layer, 0, 0)),
                  pl.BlockSpec((None, d, 128), lambda i: (j, 0, 0))],
        out_specs=[pl.BlockSpec((tm, d), lambda i: (i, 0)),
                   pl.BlockSpec((tm, 128), lambda i: (i, 0)),
                   pl.BlockSpec((tm, 128), lambda i: (i, 0))],
        out_shape=[jax.ShapeDtypeStruct((m, d), F32),
                   jax.ShapeDtypeStruct((m, 128), jnp.int32),
                   jax.ShapeDtypeStruct((m, 128), F32)],
        compiler_params=_params(1),
        name="route",
    )(h, g, router_pad)


def _row_copy(src_ref, src_row, dst_ref, dst_row, sem):
    return pltpu.make_async_copy(src_ref.at[pl.ds(src_row, 1)], dst_ref.at[pl.ds(dst_row, 1)], sem)


def _moe_gather_kernel(pos_ref, tstart_ref, tend_ref, hn_ref, xs_ref, fill_sem, row_sem):
    step = pl.program_id(0)
    tm = MOE_TM

    n_tiles = xs_ref.shape[0] // tm
    n_active = tend_ref[N_EXPERTS - 1]

    def fill_tile(t):
        dst = pl.multiple_of(t * tm, tm)
        return pltpu.make_async_copy(hn_ref.at[pl.ds(0, tm)], xs_ref.at[pl.ds(dst, tm)], fill_sem)

    def fills(act):
        for e in range(N_EXPERTS):
            @pl.when(tend_ref[e] > tstart_ref[e])
            def _():
                act(fill_tile(tend_ref[e] - 1))
        for t in range(N_EXPERTS):
            @pl.when(n_active + t < n_tiles)
            def _():
                act(fill_tile(n_active + t))

    @pl.when(step == 0)
    def _():
        fills(lambda cp: cp.start())
        fills(lambda cp: cp.wait())

    base = step * MOE_DMA_ROWS

    def issue(i, carry):
        p = base + i
        _row_copy(hn_ref, p // TOP_K, xs_ref, pos_ref[p], row_sem).start()
        return carry

    lax.fori_loop(0, MOE_DMA_ROWS, issue, 0, unroll=8)

    def drain(i, carry):
        _row_copy(hn_ref, 0, xs_ref, 0, row_sem).wait()
        return carry

    lax.fori_loop(0, MOE_DMA_ROWS, drain, 0)


def moe_gather(hn, pos, tile_start, tile_end, rows):
    m, d = hn.shape
    n_pairs = pos.shape[0]
    return pl.pallas_call(
        _moe_gather_kernel,
        grid_spec=pltpu.PrefetchScalarGridSpec(
            num_scalar_prefetch=3,
            grid=(n_pairs // MOE_DMA_ROWS,),
            in_specs=[pl.BlockSpec(memory_space=pl.ANY)],
            out_specs=pl.BlockSpec(memory_space=pl.ANY),
            scratch_shapes=[pltpu.SemaphoreType.DMA(()), pltpu.SemaphoreType.DMA(())],
        ),
        out_shape=jax.ShapeDtypeStruct((rows, d), F32),
        compiler_params=_params(1),
        name="moe_gather",
    )(pos, tile_start, tile_end, hn)


def _expert_changed(te_ref, r):
    prev = te_ref[jnp.maximum(r - 1, 0)]
    return jnp.logical_or(r == 0, te_ref[r] != prev)


def _moe_up_kernel(te_ref, na_ref, x_ref, wg_ref, wu_ref, o_ref, wgb_ref, wub_ref):
    r = pl.program_id(1)

    @pl.when(_expert_changed(te_ref, r))
    def _():
        _cast_rows(wg_ref, wgb_ref)
        _cast_rows(wu_ref, wub_ref)

    @pl.when(r < na_ref[0])
    def _():
        x = x_ref[...].astype(BF16)
        a = jnp.dot(x, wgb_ref[...], preferred_element_type=F32)
        b = jnp.dot(x, wub_ref[...], preferred_element_type=F32)
        o_ref[...] = (_silu(a) * b).astype(o_ref.dtype)

    @pl.when(r >= na_ref[0])
    def _():
        o_ref[...] = jnp.zeros(o_ref.shape, o_ref.dtype)


def _moe_down_kernel(te_ref, na_ref, x_ref, wd_ref, o_ref, wdb_ref):
    r = pl.program_id(1)

    @pl.when(_expert_changed(te_ref, r))
    def _():
        _cast_rows(wd_ref, wdb_ref)

    @pl.when(r < na_ref[0])
    def _():
        o_ref[...] = jnp.dot(x_ref[...], wdb_ref[...], preferred_element_type=F32)

    @pl.when(r >= na_ref[0])
    def _():
        o_ref[...] = jnp.zeros(o_ref.shape, o_ref.dtype)


def moe_experts(xs, tile_expert, n_active, w_gate, w_up, w_down, j, tf=1024, tn=512):
    rows, d = xs.shape
    tm = MOE_TM
    n_tiles = rows // tm
    f = w_gate.shape[-1]

    def active_tile(r, na):
        return jnp.where(r < na[0], r, 0)

    hmid = pl.pallas_call(
        _moe_up_kernel,
        grid_spec=pltpu.PrefetchScalarGridSpec(
            num_scalar_prefetch=2,
            grid=(f // tf, n_tiles),
            in_specs=[
                pl.BlockSpec((tm, d), lambda c, r, te, na: (active_tile(r, na), 0)),
                pl.BlockSpec((None, None, d, tf), lambda c, r, te, na: (j, te[r], 0, c)),
                pl.BlockSpec((None, None, d, tf), lambda c, r, te, na: (j, te[r], 0, c)),
            ],
            out_specs=pl.BlockSpec((tm, tf), lambda c, r, te, na: (r, c)),
            scratch_shapes=[pltpu.VMEM((d, tf), BF16), pltpu.VMEM((d, tf), BF16)],
        ),
        out_shape=jax.ShapeDtypeStruct((rows, f), BF16),
        compiler_params=_params(2),
        name="moe_up",
    )(tile_expert, n_active, xs, w_gate, w_up)
    return pl.pallas_call(
        _moe_down_kernel,
        grid_spec=pltpu.PrefetchScalarGridSpec(
            num_scalar_prefetch=2,
            grid=(d // tn, n_tiles),
            in_specs=[
                pl.BlockSpec((tm, f), lambda c, r, te, na: (r, 0)),
                pl.BlockSpec((None, None, f, tn), lambda c, r, te, na: (j, te[r], 0, c)),
            ],
            out_specs=pl.BlockSpec((tm, tn), lambda c, r, te, na: (r, c)),
            scratch_shapes=[pltpu.VMEM((f, tn), BF16)],
        ),
        out_shape=jax.ShapeDtypeStruct((rows, d), F32),
        compiler_params=_params(2),
        name="moe_down",
    )(tile_expert, n_active, hmid, w_down)


def _moe_combine_kernel(pos_ref, ys_ref, h_ref, tw_ref, g_ref, ho_ref, hn_ref, buf_ref, sem):
    tm = MOE_COMBINE_TM
    base = pl.program_id(0) * tm * TOP_K

    def issue(i, carry):
        for slot in range(TOP_K):
            _row_copy(ys_ref, pos_ref[base + i * TOP_K + slot], buf_ref.at[slot], i, sem).start()
        return carry

    lax.fori_loop(0, tm, issue, 0, unroll=4)

    def drain(i, carry):
        _row_copy(ys_ref, 0, buf_ref.at[0], 0, sem).wait()
        return carry

    lax.fori_loop(0, tm * TOP_K, drain, 0)

    tw = tw_ref[...]
    x = h_ref[...] + tw[:, 0:1] * buf_ref[0] + tw[:, 1:2] * buf_ref[1]
    ho_ref[...] = x
    ms = jnp.mean(x * x, axis=-1, keepdims=True)
    hn_ref[...] = (x * lax.rsqrt(ms + NORM_EPS) * g_ref[...]).astype(hn_ref.dtype)


def moe_combine(ys, pos, h, tw, norm_g, layer):
    m, d = h.shape
    tm = MOE_COMBINE_TM
    norm_g = norm_g.reshape(norm_g.shape[0], 1, d)
    return pl.pallas_call(
        _moe_combine_kernel,
        grid_spec=pltpu.PrefetchScalarGridSpec(
            num_scalar_prefetch=1,
            grid=(m // tm,),
            in_specs=[
                pl.BlockSpec(memory_space=pl.ANY),
                pl.BlockSpec((tm, d), lambda i, pos: (i, 0)),
                pl.BlockSpec((tm, 128), lambda i, pos: (i, 0)),
                pl.BlockSpec((None, 1, d), lambda i, pos: (layer, 0, 0)),
            ],
            out_specs=[pl.BlockSpec((tm, d), lambda i, pos: (i, 0)),
                       pl.BlockSpec((tm, d), lambda i, pos: (i, 0))],
            scratch_shapes=[pltpu.VMEM((TOP_K, tm, d), F32), pltpu.SemaphoreType.DMA(())],
        ),
        out_shape=[jax.ShapeDtypeStruct((m, d), F32), jax.ShapeDtypeStruct((m, d), BF16)],
        compiler_params=_params(1),
        name="moe_combine",
    )(pos, ys, h, tw, norm_g)


def moe_layer(h, norm_g, layer, router_pad, w_gate, w_up, w_down, j, next_norm_g):
    m, d = h.shape
    tm = MOE_TM
    hn, ti, tw = route(h, norm_g, layer, router_pad, j)
    top_i = ti[:, :TOP_K].reshape(-1)
    n_pairs = m * TOP_K
    n_tiles = n_pairs // tm + N_EXPERTS
    onehot = (top_i[:, None] == jnp.arange(N_EXPERTS, dtype=jnp.int32)[None, :]).astype(jnp.int32)
    csum = jnp.cumsum(onehot, axis=0)
    rank = jnp.sum((csum - onehot) * onehot, axis=1)
    counts = csum[-1]
    tiles_per = (counts + tm - 1) // tm
    tile_end = jnp.cumsum(tiles_per).astype(jnp.int32)
    tile_start = (tile_end - tiles_per).astype(jnp.int32)
    pos = (jnp.sum(onehot * tile_start[None, :], axis=1) * tm + rank).astype(jnp.int32)
    n_active = tile_end[-1:]
    tile_ids = jnp.arange(n_tiles, dtype=jnp.int32)
    tile_expert = jnp.sum((tile_ids[:, None] >= tile_end[None, :]).astype(jnp.int32), axis=1)
    last_expert = tile_expert[jnp.maximum(n_active[0] - 1, 0)]
    tile_expert = jnp.where(tile_ids < n_active[0], tile_expert, last_expert).astype(jnp.int32)

    xs = moe_gather(hn, pos, tile_start, tile_end, n_tiles * tm)
    ys = moe_experts(xs, tile_expert, n_active, w_gate, w_up, w_down, j)
    return moe_combine(ys, pos, h, tw, next_norm_g, layer)


def kernel(x, p, norm_mix, w_in, sgu_ln_g, sgu_ln_b, sgu_w, sgu_b, dn_conv_w, dn_a_log,
           dn_dt_bias, dn_norm_w, w_branch, w_out, norm_ffn, ffn_w_gate, ffn_w_up,
           ffn_w_down, moe_router, moe_w_gate, moe_w_up, moe_w_down, norm_ple,
           ple_w_gate, ple_b_gate, ple_w_proj, norm_final):
    bsz, seq, d = x.shape
    depth = p.shape[0]
    m = bsz * seq
    h = x.reshape(m, d)
    p2 = p.reshape(depth, m, p.shape[-1])
    main_w = 2 * SGU_WIDTH + 4 * DN_WIDTH
    small0 = main_w
    gates0 = main_w + 2 * DN_HEADS
    pad_h = 128 - 2 * DN_HEADS
    w_small = jnp.pad(w_in[:, :, small0:gates0], ((0, 0), (0, 0), (0, pad_h)))
    w_gates = w_in[:, :, gates0:]
    alog_row = jnp.pad(dn_a_log, ((0, 0), (DN_HEADS, pad_h))).reshape(depth, 1, 128)
    dtb_row = jnp.pad(dn_dt_bias, ((0, 0), (DN_HEADS, pad_h))).reshape(depth, 1, 128)
    ln_g = sgu_ln_g.reshape(depth, 1, SGU_WIDTH)
    ln_b = sgu_ln_b.reshape(depth, 1, SGU_WIDTH)
    sgu_b_t = jnp.swapaxes(sgu_b, 1, 2)
    nw = dn_norm_w.reshape(depth, 1, DN_HEAD_DIM)
    router_pad = jnp.pad(moe_router, ((0, 0), (0, 0), (0, 128 - N_EXPERTS)))
    ple_b = ple_b_gate.reshape(depth, 1, d)

    for i in range(depth):
        hn = rmsnorm(h, norm_mix, i, BF16)
        proj = fused_mm([(hn, 0)], [(w_in, (i,), 0, 0)], [], _ep_proj(512), main_w, BF16,
                        1024, 512, "proj_main")
        gates = fused_mm([(hn, 0)], [(w_gates, (i,), 0, 0)], [], _ep_sigmoid, 2 * d, BF16,
                         1024, 512, "proj_gates")
        small = fused_mm([(hn, 0)], [(w_small, (i,), 0, 0)], [], _ep_plain, 128, F32,
                         1024, 128, "proj_small")
        y_a = sgu(proj, ln_g, ln_b, sgu_w, sgu_b_t, i)
        y_b = gdn(proj, small, dn_conv_w, alog_row, dtb_row, nw, i, bsz, seq)
        merged = fused_mm([(y_a, 0), (y_b, 0)],
                          [(w_branch, (i, 0), 0, 0), (w_branch, (i, 1), 0, 1)],
                          [(gates, 0, "tile", 0), (gates, 0, "tile", d // 512)],
                          _ep_merge, d, BF16, 1024, 512, "merge")
        h = fused_mm([(merged, 0)], [(w_out, (i,), 0, 0)], [(h, 0, "tile", 0)],
                     _ep_residual, d, F32, 1024, 512, "out_proj")
        j = i // 2
        if i % 2 == 0:
            hn = rmsnorm(h, norm_ffn, i, BF16)
            hmid = fused_mm([(hn, 0)], [(ffn_w_gate, (j,), 0, 0), (ffn_w_up, (j,), 0, 0)], [],
                            _ep_swiglu, ffn_w_gate.shape[-1], BF16, 1024, 512, "ffn_up")
            h = fused_mm([(hmid, 0)], [(ffn_w_down, (j,), 0, 0)], [(h, 0, "tile", 0)],
                         _ep_residual, d, F32, 512, 512, "ffn_down")
            hn = rmsnorm(h, norm_ple, i, BF16)
        else:
            h, hn = moe_layer(h, norm_ffn, i, router_pad, moe_w_gate, moe_w_up, moe_w_down, j,
                              norm_ple)
        h = fused_mm([(hn, 0), (p2, i)],
                     [(ple_w_gate, (i,), 0, 0), (ple_w_proj, (i,), 0, 1)],
                     [(h, 0, "tile", 0), (ple_b, i, "row", 0)],
                     _ep_ple, d, F32, 1024, 512, "ple")
    out = rmsnorm(h, norm_final, 0, F32)
    return out.reshape(bsz, seq, d)
```

```python
import functools
import math

import jax
import jax.numpy as jnp
from jax import lax
from jax.experimental import pallas as pl
from jax.experimental.pallas import tpu as pltpu

F32 = jnp.float32
BF16 = jnp.bfloat16

D_MODEL = 2048
SGU_CHUNK = 128
SGU_GROUPS = 8
SGU_WIDTH = 1024
DN_HEADS = 8
DN_HEAD_DIM = 128
DN_WIDTH = 1024
DN_CONV = 4
DN_CHUNK = 64
N_EXPERTS = 8
TOP_K = 2
NORM_EPS = 1e-6

V7X_VMEM_LIMIT = 56 * 1024 * 1024
CAST_ROWS = 256
MOE_TM = 256
MOE_COMBINE_TM = 256


def _params(n_axes):
    return pltpu.CompilerParams(
        dimension_semantics=("arbitrary",) * n_axes,
        vmem_limit_bytes=V7X_VMEM_LIMIT)


def _sigmoid(x):
    return 1.0 / (1.0 + jnp.exp(-x))


def _silu(x):
    return x * _sigmoid(x)


def _gelu_tanh(x):
    c = math.sqrt(2.0 / math.pi)
    return 0.5 * x * (1.0 + jnp.tanh(c * (x + 0.044715 * (x * x * x))))


def _cast_rows(w_ref, wb_ref):
    k = w_ref.shape[0]
    ch = min(CAST_ROWS, k)

    def body(c, carry):
        r = pl.multiple_of(c * ch, ch)
        wb_ref[pl.ds(r, ch), :] = w_ref[pl.ds(r, ch), :].astype(BF16)
        return carry

    lax.fori_loop(0, k // ch, body, 0)


def _rmsnorm_kernel(h_ref, g_ref, o_ref):
    x = h_ref[...]
    ms = jnp.mean(x * x, axis=-1, keepdims=True)
    o_ref[...] = (x * lax.rsqrt(ms + NORM_EPS) * g_ref[...]).astype(o_ref.dtype)


def rmsnorm(h, g, layer, out_dtype, tm=512):
    m, d = h.shape
    if g.ndim == 1:
        g = g.reshape(1, 1, d)
        layer = 0
    else:
        g = g.reshape(g.shape[0], 1, d)
    return pl.pallas_call(
        _rmsnorm_kernel,
        grid=(m // tm,),
        in_specs=[pl.BlockSpec((tm, d), lambda i: (i, 0)),
                  pl.BlockSpec((None, 1, d), lambda i: (layer, 0, 0))],
        out_specs=pl.BlockSpec((tm, d), lambda i: (i, 0)),
        out_shape=jax.ShapeDtypeStruct((m, d), out_dtype),
        compiler_params=_params(1),
        name="rmsnorm",
    )(h, g)


def _fused_mm_kernel(*refs, n_x, n_w, w_x, n_e, epilogue):
    x_refs = refs[:n_x]
    w_refs = refs[n_x:n_x + n_w]
    e_refs = refs[n_x + n_w:n_x + n_w + n_e]
    o_ref = refs[n_x + n_w + n_e]
    wb_refs = refs[n_x + n_w + n_e + 1:]

    @pl.when(pl.program_id(1) == 0)
    def _():
        for w_ref, wb_ref in zip(w_refs, wb_refs):
            _cast_rows(w_ref, wb_ref)

    xs = [x_ref[...].astype(BF16) for x_ref in x_refs]
    accs = [jnp.dot(xs[w_x[l]], wb_refs[l][...], preferred_element_type=F32)
            for l in range(n_w)]
    o_ref[...] = epilogue(accs, [e[...] for e in e_refs], pl.program_id(0)).astype(o_ref.dtype)


def fused_mm(xs, ws, extras, epilogue, n_out, out_dtype, tm, tn, name):
    m = xs[0][0].shape[-2]
    grid = (n_out // tn, m // tm)
    in_specs, args = [], []
    for arr, lead in xs:
        k = arr.shape[-1]
        if arr.ndim == 2:
            in_specs.append(pl.BlockSpec((tm, k), lambda j, i: (i, 0)))
        else:
            in_specs.append(pl.BlockSpec((None, tm, k), lambda j, i, lead=lead: (lead, i, 0)))
        args.append(arr)
    scratch = []
    for arr, lead, off, _ in ws:
        k = arr.shape[-2]
        nlead = arr.ndim - 2
        lead = tuple(lead) if nlead else ()
        in_specs.append(pl.BlockSpec(
            (None,) * nlead + (k, tn),
            lambda j, i, lead=lead, off=off: lead + (0, j + off)))
        args.append(arr)
        scratch.append(pltpu.VMEM((k, tn), BF16))
    for arr, lead, kind, off in extras:
        if kind == "tile":
            if arr.ndim == 2:
                in_specs.append(pl.BlockSpec((tm, tn), lambda j, i, off=off: (i, j + off)))
            else:
                in_specs.append(pl.BlockSpec(
                    (None, tm, tn), lambda j, i, lead=lead, off=off: (lead, i, j + off)))
        else:
            in_specs.append(pl.BlockSpec(
                (None, 1, tn), lambda j, i, lead=lead, off=off: (lead, 0, j + off)))
        args.append(arr)
    kern = functools.partial(
        _fused_mm_kernel, n_x=len(xs), n_w=len(ws), w_x=tuple(w[3] for w in ws),
        n_e=len(extras), epilogue=epilogue)
    return pl.pallas_call(
        kern,
        grid=grid,
        in_specs=in_specs,
        out_specs=pl.BlockSpec((tm, tn), lambda j, i: (i, j)),
        out_shape=jax.ShapeDtypeStruct((m, n_out), out_dtype),
        scratch_shapes=scratch,
        compiler_params=_params(2),
        name=name,
    )(*args)


def _shifted_mm_kernel(x_ref, wa_ref, wn_ref, o_ref, wb_ref, *, shift):
    tn = wa_ref.shape[1]

    @pl.when(pl.program_id(1) == 0)
    def _():
        def body(c, carry):
            r = pl.multiple_of(c * CAST_ROWS, CAST_ROWS)
            wide = jnp.concatenate([wa_ref[pl.ds(r, CAST_ROWS), :],
                                    wn_ref[pl.ds(r, CAST_ROWS), :]], axis=1)
            wb_ref[pl.ds(r, CAST_ROWS), :] = pltpu.roll(wide, tn + 128 - shift, 1)[:, :tn].astype(BF16)
            return carry

        lax.fori_loop(0, wa_ref.shape[0] // CAST_ROWS, body, 0)

    acc = jnp.dot(x_ref[...], wb_ref[...], preferred_element_type=F32)
    o_ref[...] = _sigmoid(acc).astype(o_ref.dtype)


def shifted_sigmoid_mm(x, w, layer, col0, shift, n_out, tm, tn, name):
    m, k = x.shape
    base = col0 // tn
    return pl.pallas_call(
        functools.partial(_shifted_mm_kernel, shift=shift),
        grid=(n_out // tn, m // tm),
        in_specs=[
            pl.BlockSpec((tm, k), lambda j, i: (i, 0)),
            pl.BlockSpec((None, k, tn), lambda j, i: (layer, 0, base + j)),
            pl.BlockSpec((None, k, 128), lambda j, i: (layer, 0, (base + j + 1) * (tn // 128))),
        ],
        out_specs=pl.BlockSpec((tm, tn), lambda j, i: (i, j)),
        out_shape=jax.ShapeDtypeStruct((m, n_out), BF16),
        scratch_shapes=[pltpu.VMEM((k, tn), BF16)],
        compiler_params=_params(2),
        name=name,
    )(x, w, w)


def _ep_gelu(accs, extras, j):
    return _gelu_tanh(accs[0])


def _ep_sigmoid(accs, extras, j):
    return _sigmoid(accs[0])


def _ep_plain(accs, extras, j):
    return accs[0]


def _ep_merge(accs, extras, j):
    return extras[0].astype(F32) * accs[0] + extras[1].astype(F32) * accs[1]


def _ep_residual(accs, extras, j):
    return extras[0] + accs[0]


def _ep_swiglu(accs, extras, j):
    return _silu(accs[0]) * accs[1]


def _ep_ple(accs, extras, j):
    return extras[0] + _sigmoid(accs[0] + extras[1]) * accs[1]


def _sgu_kernel(u_ref, v_ref, lng_ref, lnb_ref, w_ref, bt_ref, o_ref, wm_ref):
    c = SGU_CHUNK

    @pl.when(pl.program_id(0) == 0)
    def _():
        ti = lax.broadcasted_iota(jnp.int32, (c, c), 0)
        si = lax.broadcasted_iota(jnp.int32, (c, c), 1)
        for g in range(SGU_GROUPS):
            wm_ref[g] = jnp.where(si <= ti, w_ref[g], 0.0).astype(BF16)

    v = v_ref[...].astype(F32)
    mu = jnp.mean(v, axis=-1, keepdims=True)
    vc = v - mu
    var = jnp.mean(vc * vc, axis=-1, keepdims=True)
    vln = (vc * lax.rsqrt(var + NORM_EPS) * lng_ref[...] + lnb_ref[...]).astype(BF16)
    bt = bt_ref[...]
    for g in range(SGU_GROUPS):
        sl = slice(g * c, (g + 1) * c)
        mixed = jnp.dot(wm_ref[g], vln[:, sl], preferred_element_type=F32) + bt[:, g:g + 1]
        o_ref[:, sl] = (u_ref[:, sl].astype(F32) * mixed).astype(o_ref.dtype)


def sgu(proj, ln_g, ln_b, w_s, b_s_t, layer):
    m = proj.shape[0]
    c = SGU_CHUNK
    return pl.pallas_call(
        _sgu_kernel,
        grid=(m // c,),
        in_specs=[
            pl.BlockSpec((c, SGU_WIDTH), lambda i: (i, 0)),
            pl.BlockSpec((c, SGU_WIDTH), lambda i: (i, 1)),
            pl.BlockSpec((None, 1, SGU_WIDTH), lambda i: (layer, 0, 0)),
            pl.BlockSpec((None, 1, SGU_WIDTH), lambda i: (layer, 0, 0)),
            pl.BlockSpec((None, SGU_GROUPS, c, c), lambda i: (layer, 0, 0, 0)),
            pl.BlockSpec((None, c, SGU_GROUPS), lambda i: (layer, 0, 0)),
        ],
        out_specs=pl.BlockSpec((c, SGU_WIDTH), lambda i: (i, 0)),
        out_shape=jax.ShapeDtypeStruct((m, SGU_WIDTH), BF16),
        scratch_shapes=[pltpu.VMEM((SGU_GROUPS, c, c), BF16)],
        compiler_params=_params(1),
        name="sgu",
    )(proj, proj, ln_g, ln_b, w_s, b_s_t)


def _bdot(a, b):
    return jnp.dot(a.astype(BF16), b.astype(BF16), preferred_element_type=F32)


def _bdot_nt(a, b):
    return lax.dot_general(a.astype(BF16), b.astype(BF16), (((1,), (1,)), ((), ())),
                           preferred_element_type=F32)


def _gdn_kernel(q_ref, k_ref, v_ref, z_ref, sm_ref, cw_ref, alog_ref, dtb_ref, nw_ref,
                o_ref, s_ref, prev_ref, *, nb):
    c = DN_CHUNK
    hd = DN_HEAD_DIM
    w = DN_WIDTH
    pw = 2 * hd
    n_pairs = DN_HEADS // 2

    @pl.when(pl.program_id(1) == 0)
    def _():
        s_ref[...] = jnp.zeros(s_ref.shape, F32)
        prev_ref[...] = jnp.zeros(prev_ref.shape, F32)

    row_w = lax.broadcasted_iota(jnp.int32, (c, w), 0)
    row_s = lax.broadcasted_iota(jnp.int32, (c, 128), 0)
    lane = lax.broadcasted_iota(jnp.int32, (c, 128), 1)
    left = lane < c
    jmod = jnp.where(left, lane, lane - c)
    tri = row_s >= jmod
    strict = row_s > jmod
    eye = (row_s == jmod).astype(F32)
    r256 = lax.broadcasted_iota(jnp.int32, (pw, pw), 0)
    c256 = lax.broadcasted_iota(jnp.int32, (pw, pw), 1)
    bd_mask = (r256 < hd) == (c256 < hd)
    zeros_h = jnp.zeros((c, hd), F32)
    nw = nw_ref[...]
    scale = hd ** -0.5

    def bcast2(col_a, col_b):
        return jnp.concatenate([jnp.broadcast_to(col_a, (c, hd)),
                                jnp.broadcast_to(col_b, (c, hd))], axis=1)

    def blockdiag_rows(x):
        return jnp.concatenate([jnp.where(left, x, 0.0), jnp.where(left, 0.0, x)], axis=0)

    def blockdiag_heads(x):
        return jnp.concatenate(
            [jnp.concatenate([x[:, :hd], zeros_h], axis=1),
             jnp.concatenate([zeros_h, x[:, hd:]], axis=1)], axis=0)

    def inv_norm(x):
        ss_a = jnp.sum(x[:, :hd] * x[:, :hd], axis=-1, keepdims=True)
        ss_b = jnp.sum(x[:, hd:] * x[:, hd:], axis=-1, keepdims=True)
        return bcast2(lax.rsqrt(ss_a + NORM_EPS), lax.rsqrt(ss_b + NORM_EPS))

    chains = []
    for b in range(nb):
        def conv_silu(x_ref, sec, b=b):
            x = x_ref[b].astype(F32)
            prev = prev_ref[b * 3 + sec]
            cw = cw_ref[:, sec * w:(sec + 1) * w]
            acc = cw[DN_CONV - 1:DN_CONV, :] * x
            for s in range(1, DN_CONV):
                shifted = jnp.where(row_w < s, pltpu.roll(prev, s, 0), pltpu.roll(x, s, 0))
                acc = acc + cw[DN_CONV - 1 - s:DN_CONV - s, :] * shifted
            prev_ref[b * 3 + sec] = x
            return _silu(acc)

        q_all = conv_silu(q_ref, 0)
        k_all = conv_silu(k_ref, 1)
        v_all = conv_silu(v_ref, 2)

        sm = sm_ref[b]
        beta_all = _sigmoid(sm)
        xa = sm + dtb_ref[...]
        softplus = jnp.maximum(xa, 0.0) + jnp.log(1.0 + jnp.exp(-jnp.abs(xa)))
        g = -jnp.exp(alog_ref[...]) * softplus
        s = 1
        while s < c:
            g = g + jnp.where(row_s >= s, pltpu.roll(g, s, 0), 0.0)
            s *= 2
        g_t = jnp.concatenate([g, pltpu.roll(g, 127, 1)], axis=0).T

        for p in range(n_pairs):
            a = 2 * p
            sl = slice(p * pw, (p + 1) * pw)
            q = q_all[:, sl]
            k = k_all[:, sl]
            v = v_all[:, sl]
            q = q * (inv_norm(q) * scale)
            k = k * inv_norm(k)
            gc_a = g[:, DN_HEADS + a:DN_HEADS + a + 1]
            gc_b = g[:, DN_HEADS + a + 1:DN_HEADS + a + 2]
            gl_a = gc_a[c - 1:c, :]
            gl_b = gc_b[c - 1:c, :]
            grow = g_t[DN_HEADS + a:DN_HEADS + a + 1, :]
            gcol = jnp.where(left, gc_a, gc_b)
            decay = jnp.exp(jnp.where(tri, gcol - grow, -1e30))
            beta = bcast2(beta_all[:, a:a + 1], beta_all[:, a + 1:a + 2])
            egc = bcast2(jnp.exp(gc_a), jnp.exp(gc_b))
            kb = k * beta
            chains.append(dict(
                b=b, p=p, q=q, k=k, kb=kb, vb=v * beta, kbg=kb * egc, qg=q * egc,
                kd=k * bcast2(jnp.exp(gl_a - gc_a), jnp.exp(gl_b - gc_b)),
                dl=jnp.concatenate([jnp.broadcast_to(jnp.exp(gl_a), (1, hd)),
                                    jnp.broadcast_to(jnp.exp(gl_b), (1, hd))], axis=1),
                decay=decay))

    for ch in chains:
        kq = _bdot_nt(jnp.concatenate([ch["kb"], ch["q"]], axis=0), blockdiag_heads(ch["k"]))
        ch["bk"] = -jnp.where(strict, kq[:c] * ch["decay"], 0.0)
        ch["attn"] = jnp.where(tri, kq[c:] * ch["decay"], 0.0)
        ch["qm"] = eye + ch["bk"]
    for ch in chains:
        ch["bk"] = _bdot(ch["bk"], blockdiag_rows(ch["bk"]))
    for lvl in range(1, 6):
        for ch in chains:
            if lvl < 5:
                r = _bdot(jnp.concatenate([ch["qm"], ch["bk"]], axis=0), blockdiag_rows(ch["bk"]))
                ch["qm"] = ch["qm"] + r[:c]
                ch["bk"] = r[c:]
            else:
                ch["qm"] = ch["qm"] + _bdot(ch["qm"], blockdiag_rows(ch["bk"]))
    for ch in chains:
        vb, kbg = ch["vb"], ch["kbg"]
        rhs = jnp.concatenate(
            [jnp.concatenate([vb[:, :hd], kbg[:, :hd], zeros_h, zeros_h], axis=1),
             jnp.concatenate([zeros_h, zeros_h, vb[:, hd:], kbg[:, hd:]], axis=1)], axis=0)
        sol = _bdot(ch["qm"], rhs)
        ch["u"] = jnp.concatenate([sol[:, :hd], sol[:, 2 * hd:3 * hd]], axis=1)
        ch["w"] = jnp.concatenate([sol[:, hd:2 * hd], sol[:, 3 * hd:]], axis=1)
    for ch in chains:
        ch["state"] = s_ref[ch["b"] * n_pairs + ch["p"]]
        r2 = _bdot(jnp.concatenate([ch["w"], ch["qg"]], axis=0), ch["state"])
        ch["vn"] = ch["u"] - r2[:c]
        ch["o"] = r2[c:]
    for ch in chains:
        ch["o"] = ch["o"] + _bdot(ch["attn"], blockdiag_heads(ch["vn"]))
    for ch in chains:
        upd = ch["state"] * ch["dl"] + _bdot(ch["kd"].T, ch["vn"])
        s_ref[ch["b"] * n_pairs + ch["p"]] = jnp.where(bd_mask, upd, 0.0)
    for ch in chains:
        b, p = ch["b"], ch["p"]
        for half in range(2):
            o = ch["o"][:, half * hd:(half + 1) * hd]
            sl = slice(p * pw + half * hd, p * pw + (half + 1) * hd)
            o = o * lax.rsqrt(jnp.mean(o * o, axis=-1, keepdims=True) + NORM_EPS)
            o = o * nw * _silu(z_ref[b, :, sl].astype(F32))
            o_ref[b, :, sl] = o.astype(o_ref.dtype)


def gdn(proj, small, conv_w, alog_row, dtb_row, norm_w, layer, bsz, seq, nb=1):
    m = proj.shape[0]
    c = DN_CHUNK
    n = seq // c
    w = DN_WIDTH
    qkv0 = 0
    proj3 = proj.reshape(bsz, seq, proj.shape[-1])
    small3 = small.reshape(bsz, seq, small.shape[-1])
    pw = 2 * DN_HEAD_DIM
    out = pl.pallas_call(
        functools.partial(_gdn_kernel, nb=nb),
        grid=(bsz // nb, n),
        in_specs=[
            pl.BlockSpec((nb, c, w), lambda b, t: (b, t, qkv0)),
            pl.BlockSpec((nb, c, w), lambda b, t: (b, t, qkv0 + 1)),
            pl.BlockSpec((nb, c, w), lambda b, t: (b, t, qkv0 + 2)),
            pl.BlockSpec((nb, c, w), lambda b, t: (b, t, qkv0 + 3)),
            pl.BlockSpec((nb, c, 128), lambda b, t: (b, t, 0)),
            pl.BlockSpec((None, DN_CONV, 3 * w), lambda b, t: (layer, 0, 0)),
            pl.BlockSpec((None, 1, 128), lambda b, t: (layer, 0, 0)),
            pl.BlockSpec((None, 1, 128), lambda b, t: (layer, 0, 0)),
            pl.BlockSpec((None, 1, DN_HEAD_DIM), lambda b, t: (layer, 0, 0)),
        ],
        out_specs=pl.BlockSpec((nb, c, w), lambda b, t: (b, t, 0)),
        out_shape=jax.ShapeDtypeStruct((bsz, seq, w), BF16),
        scratch_shapes=[pltpu.VMEM((nb * (DN_HEADS // 2), pw, pw), F32),
                        pltpu.VMEM((nb * 3, c, w), F32)],
        compiler_params=_params(2),
        name="gdn",
    )(proj3, proj3, proj3, proj3, small3, conv_w, alog_row, dtb_row, norm_w)
    return out.reshape(m, w)


def _route_kernel(h_ref, g_ref, r_ref, hn_ref, ti_ref, tw_ref):
    x = h_ref[...]
    ms = jnp.mean(x * x, axis=-1, keepdims=True)
    hn = x * lax.rsqrt(ms + NORM_EPS) * g_ref[...]
    hn_ref[...] = hn
    logits = jnp.dot(hn, r_ref[...], preferred_element_type=F32,
                     precision=lax.Precision.HIGHEST)
    lane = lax.broadcasted_iota(jnp.int32, logits.shape, 1)
    neg = jnp.float32(-3.0e38)
    logits = jnp.where(lane < N_EXPERTS, logits, neg)
    m1 = jnp.max(logits, axis=-1, keepdims=True)
    i1 = jnp.min(jnp.where(logits == m1, lane, 128), axis=-1, keepdims=True)
    rest = jnp.where(lane == i1, neg, logits)
    m2 = jnp.max(rest, axis=-1, keepdims=True)
    i2 = jnp.min(jnp.where(rest == m2, lane, 128), axis=-1, keepdims=True)
    e2 = jnp.exp(m2 - m1)
    w1 = 1.0 / (1.0 + e2)
    w2 = e2 / (1.0 + e2)
    ti_ref[...] = jnp.where(lane == 0, i1, jnp.where(lane == 1, i2, 0))
    tw_ref[...] = jnp.where(lane == 0, w1, jnp.where(lane == 1, w2, 0.0))


def route(h, g, layer, router_pad, j, tm=256):
    m, d = h.shape
    g = g.reshape(g.shape[0], 1, d)
    return pl.pallas_call(
        _route_kernel,
        grid=(m // tm,),
        in_specs=[pl.BlockSpec((tm, d), lambda i: (i, 0)),
                  pl.BlockSpec((None, 1, d), lambda i: (layer, 0, 0)),
                  pl.BlockSpec((None, d, 128), lambda i: (j, 0, 0))],
        out_specs=[pl.BlockSpec((tm, d), lambda i: (i, 0)),
                   pl.BlockSpec((tm, 128), lambda i: (i, 0)),
                   pl.BlockSpec((tm, 128), lambda i: (i, 0))],
        out_shape=[jax.ShapeDtypeStruct((m, d), F32),
                   jax.ShapeDtypeStruct((m, 128), jnp.int32),
                   jax.ShapeDtypeStruct((m, 128), F32)],
        compiler_params=_params(1),
        name="route",
    )(h, g, router_pad)


def _row_copy(src_ref, src_row, dst_ref, dst_row, sem):
    return pltpu.make_async_copy(src_ref.at[pl.ds(src_row, 1)], dst_ref.at[pl.ds(dst_row, 1)], sem)


def _moe_gather_kernel(pos_ref, tstart_ref, tend_ref, hn_ref, xs_ref, fill_sem, row_sem):
    step = pl.program_id(0)
    tm = MOE_TM

    n_tiles = xs_ref.shape[0] // tm
    n_active = tend_ref[N_EXPERTS - 1]

    def fill_tile(t):
        dst = pl.multiple_of(t * tm, tm)
        return pltpu.make_async_copy(hn_ref, xs_ref.at[pl.ds(dst, tm)], fill_sem)

    def fills(act):
        for e in range(N_EXPERTS):
            @pl.when(tend_ref[e] > tstart_ref[e])
            def _():
                act(fill_tile(tend_ref[e] - 1))
        for t in range(N_EXPERTS):
            @pl.when(n_active + t < n_tiles)
            def _():
                act(fill_tile(n_active + t))

    @pl.when(step == 0)
    def _():
        fills(lambda cp: cp.start())
        fills(lambda cp: cp.wait())

    base = step * tm * TOP_K

    def issue(t, carry):
        for slot in range(TOP_K):
            _row_copy(hn_ref, t, xs_ref, pos_ref[base + t * TOP_K + slot], row_sem).start()
        return carry

    lax.fori_loop(0, tm, issue, 0, unroll=4)

    def drain(i, carry):
        _row_copy(hn_ref, 0, xs_ref, 0, row_sem).wait()
        return carry

    lax.fori_loop(0, tm * TOP_K, drain, 0)


def moe_gather(hn, pos, tile_start, tile_end, rows):
    m, d = hn.shape
    tm = MOE_TM
    return pl.pallas_call(
        _moe_gather_kernel,
        grid_spec=pltpu.PrefetchScalarGridSpec(
            num_scalar_prefetch=3,
            grid=(m // tm,),
            in_specs=[pl.BlockSpec((tm, d), lambda i, pos, ts, te: (i, 0))],
            out_specs=pl.BlockSpec(memory_space=pl.ANY),
            scratch_shapes=[pltpu.SemaphoreType.DMA(()), pltpu.SemaphoreType.DMA(())],
        ),
        out_shape=jax.ShapeDtypeStruct((rows, d), F32),
        compiler_params=_params(1),
        name="moe_gather",
    )(pos, tile_start, tile_end, hn)


def _expert_changed(te_ref, r):
    prev = te_ref[jnp.maximum(r - 1, 0)]
    return jnp.logical_or(r == 0, te_ref[r] != prev)


def _moe_up_kernel(te_ref, na_ref, x_ref, wg_ref, wu_ref, o_ref, wgb_ref, wub_ref):
    r = pl.program_id(1)

    @pl.when(_expert_changed(te_ref, r))
    def _():
        _cast_rows(wg_ref, wgb_ref)
        _cast_rows(wu_ref, wub_ref)

    @pl.when(r < na_ref[0])
    def _():
        x = x_ref[...].astype(BF16)
        a = jnp.dot(x, wgb_ref[...], preferred_element_type=F32)
        b = jnp.dot(x, wub_ref[...], preferred_element_type=F32)
        o_ref[...] = (_silu(a) * b).astype(o_ref.dtype)

    @pl.when(r >= na_ref[0])
    def _():
        o_ref[...] = jnp.zeros(o_ref.shape, o_ref.dtype)


def _moe_down_kernel(te_ref, na_ref, x_ref, wd_ref, o_ref, wdb_ref):
    r = pl.program_id(1)

    @pl.when(_expert_changed(te_ref, r))
    def _():
        _cast_rows(wd_ref, wdb_ref)

    @pl.when(r < na_ref[0])
    def _():
        o_ref[...] = jnp.dot(x_ref[...], wdb_ref[...], preferred_element_type=F32)

    @pl.when(r >= na_ref[0])
    def _():
        o_ref[...] = jnp.zeros(o_ref.shape, o_ref.dtype)


def moe_experts(xs, tile_expert, n_active, w_gate, w_up, w_down, j, tf=1024, tn=512):
    rows, d = xs.shape
    tm = MOE_TM
    n_tiles = rows // tm
    f = w_gate.shape[-1]

    def active_tile(r, na):
        return jnp.where(r < na[0], r, 0)

    hmid = pl.pallas_call(
        _moe_up_kernel,
        grid_spec=pltpu.PrefetchScalarGridSpec(
            num_scalar_prefetch=2,
            grid=(f // tf, n_tiles),
            in_specs=[
                pl.BlockSpec((tm, d), lambda c, r, te, na: (active_tile(r, na), 0)),
                pl.BlockSpec((None, None, d, tf), lambda c, r, te, na: (j, te[r], 0, c)),
                pl.BlockSpec((None, None, d, tf), lambda c, r, te, na: (j, te[r], 0, c)),
            ],
            out_specs=pl.BlockSpec((tm, tf), lambda c, r, te, na: (r, c)),
            scratch_shapes=[pltpu.VMEM((d, tf), BF16), pltpu.VMEM((d, tf), BF16)],
        ),
        out_shape=jax.ShapeDtypeStruct((rows, f), BF16),
        compiler_params=_params(2),
        name="moe_up",
    )(tile_expert, n_active, xs, w_gate, w_up)
    return pl.pallas_call(
        _moe_down_kernel,
        grid_spec=pltpu.PrefetchScalarGridSpec(
            num_scalar_prefetch=2,
            grid=(d // tn, n_tiles),
            in_specs=[
                pl.BlockSpec((tm, f), lambda c, r, te, na: (r, 0)),
                pl.BlockSpec((None, None, f, tn), lambda c, r, te, na: (j, te[r], 0, c)),
            ],
            out_specs=pl.BlockSpec((tm, tn), lambda c, r, te, na: (r, c)),
            scratch_shapes=[pltpu.VMEM((f, tn), BF16)],
        ),
        out_shape=jax.ShapeDtypeStruct((rows, d), F32),
        compiler_params=_params(2),
        name="moe_down",
    )(tile_expert, n_active, hmid, w_down)


def _moe_combine_kernel(pos_ref, ys_ref, h_ref, tw_ref, g_ref, ho_ref, hn_ref, buf_ref, sem):
    tm = MOE_COMBINE_TM
    base = pl.program_id(0) * tm * TOP_K

    def issue(i, carry):
        for slot in range(TOP_K):
            _row_copy(ys_ref, pos_ref[base + i * TOP_K + slot], buf_ref.at[slot], i, sem).start()
        return carry

    lax.fori_loop(0, tm, issue, 0, unroll=4)

    def drain(i, carry):
        _row_copy(ys_ref, 0, buf_ref.at[0], 0, sem).wait()
        return carry

    lax.fori_loop(0, tm * TOP_K, drain, 0)

    tw = tw_ref[...]
    x = h_ref[...] + tw[:, 0:1] * buf_ref[0] + tw[:, 1:2] * buf_ref[1]
    ho_ref[...] = x
    ms = jnp.mean(x * x, axis=-1, keepdims=True)
    hn_ref[...] = (x * lax.rsqrt(ms + NORM_EPS) * g_ref[...]).astype(hn_ref.dtype)


def moe_combine(ys, pos, h, tw, norm_g, layer):
    m, d = h.shape
    tm = MOE_COMBINE_TM
    norm_g = norm_g.reshape(norm_g.shape[0], 1, d)
    return pl.pallas_call(
        _moe_combine_kernel,
        grid_spec=pltpu.PrefetchScalarGridSpec(
            num_scalar_prefetch=1,
            grid=(m // tm,),
            in_specs=[
                pl.BlockSpec(memory_space=pl.ANY),
                pl.BlockSpec((tm, d), lambda i, pos: (i, 0)),
                pl.BlockSpec((tm, 128), lambda i, pos: (i, 0)),
                pl.BlockSpec((None, 1, d), lambda i, pos: (layer, 0, 0)),
            ],
            out_specs=[pl.BlockSpec((tm, d), lambda i, pos: (i, 0)),
                       pl.BlockSpec((tm, d), lambda i, pos: (i, 0))],
            scratch_shapes=[pltpu.VMEM((TOP_K, tm, d), F32), pltpu.SemaphoreType.DMA(())],
        ),
        out_shape=[jax.ShapeDtypeStruct((m, d), F32), jax.ShapeDtypeStruct((m, d), BF16)],
        compiler_params=_params(1),
        name="moe_combine",
    )(pos, ys, h, tw, norm_g)


def moe_layer(h, norm_g, layer, router_pad, w_gate, w_up, w_down, j, next_norm_g):
    m, d = h.shape
    tm = MOE_TM
    hn, ti, tw = route(h, norm_g, layer, router_pad, j)
    top_i = ti[:, :TOP_K].reshape(-1)
    n_pairs = m * TOP_K
    n_tiles = n_pairs // tm + N_EXPERTS
    onehot = (top_i[:, None] == jnp.arange(N_EXPERTS, dtype=jnp.int32)[None, :]).astype(jnp.int32)
    csum = jnp.cumsum(onehot, axis=0)
    rank = jnp.sum((csum - onehot) * onehot, axis=1)
    counts = csum[-1]
    tiles_per = (counts + tm - 1) // tm
    tile_end = jnp.cumsum(tiles_per).astype(jnp.int32)
    tile_start = (tile_end - tiles_per).astype(jnp.int32)
    pos = (jnp.sum(onehot * tile_start[None, :], axis=1) * tm + rank).astype(jnp.int32)
    n_active = tile_end[-1:]
    tile_ids = jnp.arange(n_tiles, dtype=jnp.int32)
    tile_expert = jnp.sum((tile_ids[:, None] >= tile_end[None, :]).astype(jnp.int32), axis=1)
    last_expert = tile_expert[jnp.maximum(n_active[0] - 1, 0)]
    tile_expert = jnp.where(tile_ids < n_active[0], tile_expert, last_expert).astype(jnp.int32)

    xs = moe_gather(hn, pos, tile_start, tile_end, n_tiles * tm)
    ys = moe_experts(xs, tile_expert, n_active, w_gate, w_up, w_down, j)
    return moe_combine(ys, pos, h, tw, next_norm_g, layer)


def kernel(x, p, norm_mix, w_in, sgu_ln_g, sgu_ln_b, sgu_w, sgu_b, dn_conv_w, dn_a_log,
           dn_dt_bias, dn_norm_w, w_branch, w_out, norm_ffn, ffn_w_gate, ffn_w_up,
           ffn_w_down, moe_router, moe_w_gate, moe_w_up, moe_w_down, norm_ple,
           ple_w_gate, ple_b_gate, ple_w_proj, norm_final):
    bsz, seq, d = x.shape
    depth = p.shape[0]
    m = bsz * seq
    h = x.reshape(m, d)
    p2 = p.reshape(depth, m, p.shape[-1])
    main_w = 2 * SGU_WIDTH + 4 * DN_WIDTH
    n_small = 2 * DN_HEADS
    pad_h = 128 - n_small
    alog_row = jnp.pad(dn_a_log, ((0, 0), (DN_HEADS, pad_h))).reshape(depth, 1, 128)
    dtb_row = jnp.pad(dn_dt_bias, ((0, 0), (DN_HEADS, pad_h))).reshape(depth, 1, 128)
    ln_g = sgu_ln_g.reshape(depth, 1, SGU_WIDTH)
    ln_b = sgu_ln_b.reshape(depth, 1, SGU_WIDTH)
    sgu_b_t = jnp.swapaxes(sgu_b, 1, 2)
    nw = dn_norm_w.reshape(depth, 1, DN_HEAD_DIM)
    router_pad = jnp.pad(moe_router, ((0, 0), (0, 0), (0, 128 - N_EXPERTS)))
    ple_b = ple_b_gate.reshape(depth, 1, d)

    for i in range(depth):
        hn = rmsnorm(h, norm_mix, i, BF16)
        uv = fused_mm([(hn, 0)], [(w_in, (i,), 0, 0)], [], _ep_gelu, 2 * SGU_WIDTH, BF16,
                      1024, 512, "proj_uv")
        qkvz = fused_mm([(hn, 0)], [(w_in, (i,), (2 * SGU_WIDTH) // 512, 0)], [], _ep_plain,
                        4 * DN_WIDTH, BF16, 1024, 512, "proj_qkvz")
        gates = shifted_sigmoid_mm(hn, w_in, i, main_w, n_small, 2 * d, 1024, 512, "proj_gates")
        small = fused_mm([(hn, 0)], [(w_in, (i,), main_w // 128, 0)], [], _ep_plain, 128, F32,
                         1024, 128, "proj_small")
        y_a = sgu(uv, ln_g, ln_b, sgu_w, sgu_b_t, i)
        y_b = gdn(qkvz, small, dn_conv_w, alog_row, dtb_row, nw, i, bsz, seq, nb=2)
        merged = fused_mm([(y_a, 0), (y_b, 0)],
                          [(w_branch, (i, 0), 0, 0), (w_branch, (i, 1), 0, 1)],
                          [(gates, 0, "tile", 0), (gates, 0, "tile", d // 512)],
                          _ep_merge, d, BF16, 1024, 512, "merge")
        h = fused_mm([(merged, 0)], [(w_out, (i,), 0, 0)], [(h, 0, "tile", 0)],
                     _ep_residual, d, F32, 1024, 512, "out_proj")
        j = i // 2
        if i % 2 == 0:
            hn = rmsnorm(h, norm_ffn, i, BF16)
            hmid = fused_mm([(hn, 0)], [(ffn_w_gate, (j,), 0, 0), (ffn_w_up, (j,), 0, 0)], [],
                            _ep_swiglu, ffn_w_gate.shape[-1], BF16, 1024, 512, "ffn_up")
            h = fused_mm([(hmid, 0)], [(ffn_w_down, (j,), 0, 0)], [(h, 0, "tile", 0)],
                         _ep_residual, d, F32, 512, 512, "ffn_down")
            hn = rmsnorm(h, norm_ple, i, BF16)
        else:
            h, hn = moe_layer(h, norm_ffn, i, router_pad, moe_w_gate, moe_w_up, moe_w_down, j,
                              norm_ple)
        h = fused_mm([(hn, 0), (p2, i)],
                     [(ple_w_gate, (i,), 0, 0), (ple_w_proj, (i,), 0, 1)],
                     [(h, 0, "tile", 0), (ple_b, i, "row", 0)],
                     _ep_ple, d, F32, 1024, 512, "ple")
    out = rmsnorm(h, norm_final, 0, F32)
    return out.reshape(bsz, seq, d)
```

```python
import functools
import math

import jax
import jax.numpy as jnp
from jax import lax
from jax.experimental import pallas as pl
from jax.experimental.pallas import tpu as pltpu

F32 = jnp.float32
BF16 = jnp.bfloat16

D_MODEL = 2048
SGU_CHUNK = 128
SGU_GROUPS = 8
SGU_WIDTH = 1024
DN_HEADS = 8
DN_HEAD_DIM = 128
DN_WIDTH = 1024
DN_CONV = 4
DN_CHUNK = 64
N_EXPERTS = 8
TOP_K = 2
NORM_EPS = 1e-6

V7X_VMEM_LIMIT = 56 * 1024 * 1024
CAST_ROWS = 256
MOE_TM = 256
MOE_COMBINE_TM = 256


def _params(n_axes):
    return pltpu.CompilerParams(
        dimension_semantics=("arbitrary",) * n_axes,
        vmem_limit_bytes=V7X_VMEM_LIMIT)


def _sigmoid(x):
    return 1.0 / (1.0 + jnp.exp(-x))


def _silu(x):
    return x * _sigmoid(x)


def _gelu_tanh(x):
    c = math.sqrt(2.0 / math.pi)
    return 0.5 * x * (1.0 + jnp.tanh(c * (x + 0.044715 * (x * x * x))))


def _cast_rows(w_ref, wb_ref):
    k = w_ref.shape[0]
    ch = min(CAST_ROWS, k)

    def body(c, carry):
        r = pl.multiple_of(c * ch, ch)
        wb_ref[pl.ds(r, ch), :] = w_ref[pl.ds(r, ch), :].astype(BF16)
        return carry

    lax.fori_loop(0, k // ch, body, 0)


def _cast_rows_transposed(wt_ref, wb_ref):
    tn = wt_ref.shape[0]
    ch = min(CAST_ROWS, tn)
    for c in range(tn // ch):
        wb_ref[:, c * ch:(c + 1) * ch] = wt_ref[c * ch:(c + 1) * ch, :].T.astype(BF16)


def _rmsnorm_kernel(h_ref, g_ref, o_ref):
    x = h_ref[...]
    ms = jnp.mean(x * x, axis=-1, keepdims=True)
    o_ref[...] = (x * lax.rsqrt(ms + NORM_EPS) * g_ref[...]).astype(o_ref.dtype)


def rmsnorm(h, g, layer, out_dtype, tm=512):
    m, d = h.shape
    if g.ndim == 1:
        g = g.reshape(1, 1, d)
        layer = 0
    else:
        g = g.reshape(g.shape[0], 1, d)
    return pl.pallas_call(
        _rmsnorm_kernel,
        grid=(m // tm,),
        in_specs=[pl.BlockSpec((tm, d), lambda i: (i, 0)),
                  pl.BlockSpec((None, 1, d), lambda i: (layer, 0, 0))],
        out_specs=pl.BlockSpec((tm, d), lambda i: (i, 0)),
        out_shape=jax.ShapeDtypeStruct((m, d), out_dtype),
        compiler_params=_params(1),
        name="rmsnorm",
    )(h, g)


def _fused_mm_kernel(*refs, n_x, n_w, w_x, n_e, epilogue, w_t):
    x_refs = refs[:n_x]
    w_refs = refs[n_x:n_x + n_w]
    e_refs = refs[n_x + n_w:n_x + n_w + n_e]
    o_ref = refs[n_x + n_w + n_e]
    wb_refs = refs[n_x + n_w + n_e + 1:]

    @pl.when(pl.program_id(1) == 0)
    def _():
        for w_ref, wb_ref in zip(w_refs, wb_refs):
            (_cast_rows_transposed if w_t else _cast_rows)(w_ref, wb_ref)

    xs = [x_ref[...].astype(BF16) for x_ref in x_refs]
    accs = [jnp.dot(xs[w_x[l]], wb_refs[l][...], preferred_element_type=F32)
            for l in range(n_w)]
    o_ref[...] = epilogue(accs, [e[...] for e in e_refs], pl.program_id(0)).astype(o_ref.dtype)


def fused_mm(xs, ws, extras, epilogue, n_out, out_dtype, tm, tn, name, w_t=False):
    m = xs[0][0].shape[-2]
    grid = (n_out // tn, m // tm)
    in_specs, args = [], []
    for arr, lead in xs:
        k = arr.shape[-1]
        if arr.ndim == 2:
            in_specs.append(pl.BlockSpec((tm, k), lambda j, i: (i, 0)))
        else:
            in_specs.append(pl.BlockSpec((None, tm, k), lambda j, i, lead=lead: (lead, i, 0)))
        args.append(arr)
    scratch = []
    for arr, lead, off, _ in ws:
        k = arr.shape[-1] if w_t else arr.shape[-2]
        nlead = arr.ndim - 2
        lead = tuple(lead) if nlead else ()
        if w_t:
            in_specs.append(pl.BlockSpec(
                (None,) * nlead + (tn, k),
                lambda j, i, lead=lead, off=off: lead + (j + off, 0)))
        else:
            in_specs.append(pl.BlockSpec(
                (None,) * nlead + (k, tn),
                lambda j, i, lead=lead, off=off: lead + (0, j + off)))
        args.append(arr)
        scratch.append(pltpu.VMEM((k, tn), BF16))
    for arr, lead, kind, off in extras:
        if kind == "tile":
            if arr.ndim == 2:
                in_specs.append(pl.BlockSpec((tm, tn), lambda j, i, off=off: (i, j + off)))
            else:
                in_specs.append(pl.BlockSpec(
                    (None, tm, tn), lambda j, i, lead=lead, off=off: (lead, i, j + off)))
        else:
            in_specs.append(pl.BlockSpec(
                (None, 1, tn), lambda j, i, lead=lead, off=off: (lead, 0, j + off)))
        args.append(arr)
    kern = functools.partial(
        _fused_mm_kernel, n_x=len(xs), n_w=len(ws), w_x=tuple(w[3] for w in ws),
        n_e=len(extras), epilogue=epilogue, w_t=w_t)
    return pl.pallas_call(
        kern,
        grid=grid,
        in_specs=in_specs,
        out_specs=pl.BlockSpec((tm, tn), lambda j, i: (i, j)),
        out_shape=jax.ShapeDtypeStruct((m, n_out), out_dtype),
        scratch_shapes=scratch,
        compiler_params=_params(2),
        name=name,
    )(*args)


def _shifted_mm_kernel(x_ref, wa_ref, wn_ref, o_ref, wb_ref, *, shift):
    tn = wa_ref.shape[0]

    @pl.when(pl.program_id(1) == 0)
    def _():
        for c in range(tn // CAST_ROWS):
            lo = shift + c * CAST_ROWS
            hi = lo + CAST_ROWS
            if hi <= tn:
                rows = wa_ref[lo:hi, :]
            else:
                rows = jnp.concatenate([wa_ref[lo:tn, :], wn_ref[0:hi - tn, :]], axis=0)
            wb_ref[:, c * CAST_ROWS:(c + 1) * CAST_ROWS] = rows.T.astype(BF16)

    acc = jnp.dot(x_ref[...], wb_ref[...], preferred_element_type=F32)
    o_ref[...] = _sigmoid(acc).astype(o_ref.dtype)


def shifted_sigmoid_mm(x, wt, layer, row0, shift, n_out, tm, tn, name):
    m, k = x.shape
    base = row0 // tn
    return pl.pallas_call(
        functools.partial(_shifted_mm_kernel, shift=shift),
        grid=(n_out // tn, m // tm),
        in_specs=[
            pl.BlockSpec((tm, k), lambda j, i: (i, 0)),
            pl.BlockSpec((None, tn, k), lambda j, i: (layer, base + j, 0)),
            pl.BlockSpec((None, shift, k), lambda j, i: (layer, (base + j + 1) * (tn // shift), 0)),
        ],
        out_specs=pl.BlockSpec((tm, tn), lambda j, i: (i, j)),
        out_shape=jax.ShapeDtypeStruct((m, n_out), BF16),
        scratch_shapes=[pltpu.VMEM((k, tn), BF16)],
        compiler_params=_params(2),
        name=name,
    )(x, wt, wt)


def _ep_gelu(accs, extras, j):
    return _gelu_tanh(accs[0])


def _ep_sigmoid(accs, extras, j):
    return _sigmoid(accs[0])


def _ep_plain(accs, extras, j):
    return accs[0]


def _ep_merge(accs, extras, j):
    return extras[0].astype(F32) * accs[0] + extras[1].astype(F32) * accs[1]


def _ep_residual(accs, extras, j):
    return extras[0] + accs[0]


def _ep_swiglu(accs, extras, j):
    return _silu(accs[0]) * accs[1]


def _ep_ple(accs, extras, j):
    return extras[0] + _sigmoid(accs[0] + extras[1]) * accs[1]


def _sgu_kernel(u_ref, v_ref, lng_ref, lnb_ref, w_ref, bt_ref, o_ref, wm_ref):
    c = SGU_CHUNK

    @pl.when(pl.program_id(0) == 0)
    def _():
        ti = lax.broadcasted_iota(jnp.int32, (c, c), 0)
        si = lax.broadcasted_iota(jnp.int32, (c, c), 1)
        for g in range(SGU_GROUPS):
            wm_ref[g] = jnp.where(si <= ti, w_ref[g], 0.0).astype(BF16)

    v = v_ref[...].astype(F32)
    mu = jnp.mean(v, axis=-1, keepdims=True)
    vc = v - mu
    var = jnp.mean(vc * vc, axis=-1, keepdims=True)
    vln = (vc * lax.rsqrt(var + NORM_EPS) * lng_ref[...] + lnb_ref[...]).astype(BF16)
    bt = bt_ref[...]
    for g in range(SGU_GROUPS):
        sl = slice(g * c, (g + 1) * c)
        mixed = jnp.dot(wm_ref[g], vln[:, sl], preferred_element_type=F32) + bt[:, g:g + 1]
        o_ref[:, sl] = (u_ref[:, sl].astype(F32) * mixed).astype(o_ref.dtype)


def sgu(proj, ln_g, ln_b, w_s, b_s_t, layer):
    m = proj.shape[0]
    c = SGU_CHUNK
    return pl.pallas_call(
        _sgu_kernel,
        grid=(m // c,),
        in_specs=[
            pl.BlockSpec((c, SGU_WIDTH), lambda i: (i, 0)),
            pl.BlockSpec((c, SGU_WIDTH), lambda i: (i, 1)),
            pl.BlockSpec((None, 1, SGU_WIDTH), lambda i: (layer, 0, 0)),
            pl.BlockSpec((None, 1, SGU_WIDTH), lambda i: (layer, 0, 0)),
            pl.BlockSpec((None, SGU_GROUPS, c, c), lambda i: (layer, 0, 0, 0)),
            pl.BlockSpec((None, c, SGU_GROUPS), lambda i: (layer, 0, 0)),
        ],
        out_specs=pl.BlockSpec((c, SGU_WIDTH), lambda i: (i, 0)),
        out_shape=jax.ShapeDtypeStruct((m, SGU_WIDTH), BF16),
        scratch_shapes=[pltpu.VMEM((SGU_GROUPS, c, c), BF16)],
        compiler_params=_params(1),
        name="sgu",
    )(proj, proj, ln_g, ln_b, w_s, b_s_t)


def _bdot(a, b):
    return jnp.dot(a.astype(BF16), b.astype(BF16), preferred_element_type=F32)


def _bdot_nt(a, b):
    return lax.dot_general(a.astype(BF16), b.astype(BF16), (((1,), (1,)), ((), ())),
                           preferred_element_type=F32)


def _gdn_kernel(q_ref, k_ref, v_ref, z_ref, sm_ref, cw_ref, alog_ref, dtb_ref, nw_ref,
                o_ref, s_ref, prev_ref, *, nb):
    c = DN_CHUNK
    hd = DN_HEAD_DIM
    w = DN_WIDTH
    pw = 2 * hd
    n_pairs = DN_HEADS // 2

    @pl.when(pl.program_id(1) == 0)
    def _():
        s_ref[...] = jnp.zeros(s_ref.shape, F32)
        prev_ref[...] = jnp.zeros(prev_ref.shape, F32)

    row_w = lax.broadcasted_iota(jnp.int32, (c, w), 0)
    row_s = lax.broadcasted_iota(jnp.int32, (c, 128), 0)
    lane = lax.broadcasted_iota(jnp.int32, (c, 128), 1)
    left = lane < c
    jmod = jnp.where(left, lane, lane - c)
    tri = row_s >= jmod
    strict = row_s > jmod
    eye = (row_s == jmod).astype(F32)
    r256 = lax.broadcasted_iota(jnp.int32, (pw, pw), 0)
    c256 = lax.broadcasted_iota(jnp.int32, (pw, pw), 1)
    bd_mask = (r256 < hd) == (c256 < hd)
    zeros_h = jnp.zeros((c, hd), F32)
    nw = nw_ref[...]
    scale = hd ** -0.5

    def bcast2(col_a, col_b):
        return jnp.concatenate([jnp.broadcast_to(col_a, (c, hd)),
                                jnp.broadcast_to(col_b, (c, hd))], axis=1)

    def blockdiag_rows(x):
        return jnp.concatenate([jnp.where(left, x, 0.0), jnp.where(left, 0.0, x)], axis=0)

    def blockdiag_heads(x):
        return jnp.concatenate(
            [jnp.concatenate([x[:, :hd], zeros_h], axis=1),
             jnp.concatenate([zeros_h, x[:, hd:]], axis=1)], axis=0)

    def inv_norm(x):
        ss_a = jnp.sum(x[:, :hd] * x[:, :hd], axis=-1, keepdims=True)
        ss_b = jnp.sum(x[:, hd:] * x[:, hd:], axis=-1, keepdims=True)
        return bcast2(lax.rsqrt(ss_a + NORM_EPS), lax.rsqrt(ss_b + NORM_EPS))

    chains = []
    for b in range(nb):
        def conv_silu(x_ref, sec, b=b):
            x = x_ref[b].astype(F32)
            prev = prev_ref[b * 3 + sec]
            cw = cw_ref[:, sec * w:(sec + 1) * w]
            acc = cw[DN_CONV - 1:DN_CONV, :] * x
            for s in range(1, DN_CONV):
                shifted = jnp.where(row_w < s, pltpu.roll(prev, s, 0), pltpu.roll(x, s, 0))
                acc = acc + cw[DN_CONV - 1 - s:DN_CONV - s, :] * shifted
            prev_ref[b * 3 + sec] = x
            return _silu(acc)

        q_all = conv_silu(q_ref, 0)
        k_all = conv_silu(k_ref, 1)
        v_all = conv_silu(v_ref, 2)

        sm = sm_ref[b]
        beta_all = _sigmoid(sm)
        xa = sm + dtb_ref[...]
        softplus = jnp.maximum(xa, 0.0) + jnp.log(1.0 + jnp.exp(-jnp.abs(xa)))
        g = -jnp.exp(alog_ref[...]) * softplus
        s = 1
        while s < c:
            g = g + jnp.where(row_s >= s, pltpu.roll(g, s, 0), 0.0)
            s *= 2
        g_t = jnp.concatenate([g, pltpu.roll(g, 127, 1)], axis=0).T

        for p in range(n_pairs):
            a = 2 * p
            sl = slice(p * pw, (p + 1) * pw)
            q = q_all[:, sl]
            k = k_all[:, sl]
            v = v_all[:, sl]
            q = q * (inv_norm(q) * scale)
            k = k * inv_norm(k)
            gc_a = g[:, DN_HEADS + a:DN_HEADS + a + 1]
            gc_b = g[:, DN_HEADS + a + 1:DN_HEADS + a + 2]
            gl_a = gc_a[c - 1:c, :]
            gl_b = gc_b[c - 1:c, :]
            grow = g_t[DN_HEADS + a:DN_HEADS + a + 1, :]
            gcol = jnp.where(left, gc_a, gc_b)
            decay = jnp.exp(jnp.where(tri, gcol - grow, -1e30))
            beta = bcast2(beta_all[:, a:a + 1], beta_all[:, a + 1:a + 2])
            egc = bcast2(jnp.exp(gc_a), jnp.exp(gc_b))
            kb = k * beta
            chains.append(dict(
                b=b, p=p, q=q, k=k, kb=kb, vb=v * beta, kbg=kb * egc, qg=q * egc,
                kd=k * bcast2(jnp.exp(gl_a - gc_a), jnp.exp(gl_b - gc_b)),
                dl=jnp.concatenate([jnp.broadcast_to(jnp.exp(gl_a), (1, hd)),
                                    jnp.broadcast_to(jnp.exp(gl_b), (1, hd))], axis=1),
                decay=decay))

    for ch in chains:
        kq = _bdot_nt(jnp.concatenate([ch["kb"], ch["q"]], axis=0), blockdiag_heads(ch["k"]))
        ch["bk"] = -jnp.where(strict, kq[:c] * ch["decay"], 0.0)
        ch["attn"] = jnp.where(tri, kq[c:] * ch["decay"], 0.0)
        ch["qm"] = eye + ch["bk"]
    for ch in chains:
        ch["bk"] = _bdot(ch["bk"], blockdiag_rows(ch["bk"]))
    for lvl in range(1, 6):
        for ch in chains:
            if lvl < 5:
                r = _bdot(jnp.concatenate([ch["qm"], ch["bk"]], axis=0), blockdiag_rows(ch["bk"]))
                ch["qm"] = ch["qm"] + r[:c]
                ch["bk"] = r[c:]
            else:
                ch["qm"] = ch["qm"] + _bdot(ch["qm"], blockdiag_rows(ch["bk"]))
    for ch in chains:
        vb, kbg = ch["vb"], ch["kbg"]
        rhs = jnp.concatenate(
            [jnp.concatenate([vb[:, :hd], kbg[:, :hd], zeros_h, zeros_h], axis=1),
             jnp.concatenate([zeros_h, zeros_h, vb[:, hd:], kbg[:, hd:]], axis=1)], axis=0)
        sol = _bdot(ch["qm"], rhs)
        ch["u"] = jnp.concatenate([sol[:, :hd], sol[:, 2 * hd:3 * hd]], axis=1)
        ch["w"] = jnp.concatenate([sol[:, hd:2 * hd], sol[:, 3 * hd:]], axis=1)
    for ch in chains:
        ch["state"] = s_ref[ch["b"] * n_pairs + ch["p"]]
        r2 = _bdot(jnp.concatenate([ch["w"], ch["qg"]], axis=0), ch["state"])
        ch["vn"] = ch["u"] - r2[:c]
        ch["o"] = r2[c:]
    for ch in chains:
        ch["o"] = ch["o"] + _bdot(ch["attn"], blockdiag_heads(ch["vn"]))
    for ch in chains:
        upd = ch["state"] * ch["dl"] + _bdot(ch["kd"].T, ch["vn"])
        s_ref[ch["b"] * n_pairs + ch["p"]] = jnp.where(bd_mask, upd, 0.0)
    for ch in chains:
        b, p = ch["b"], ch["p"]
        for half in range(2):
            o = ch["o"][:, half * hd:(half + 1) * hd]
            sl = slice(p * pw + half * hd, p * pw + (half + 1) * hd)
            o = o * lax.rsqrt(jnp.mean(o * o, axis=-1, keepdims=True) + NORM_EPS)
            o = o * nw * _silu(z_ref[b, :, sl].astype(F32))
            o_ref[b, :, sl] = o.astype(o_ref.dtype)


def gdn(proj, small, conv_w, alog_row, dtb_row, norm_w, layer, bsz, seq, nb=1):
    m = proj.shape[0]
    c = DN_CHUNK
    n = seq // c
    w = DN_WIDTH
    qkv0 = 0
    proj3 = proj.reshape(bsz, seq, proj.shape[-1])
    small3 = small.reshape(bsz, seq, small.shape[-1])
    pw = 2 * DN_HEAD_DIM
    out = pl.pallas_call(
        functools.partial(_gdn_kernel, nb=nb),
        grid=(bsz // nb, n),
        in_specs=[
            pl.BlockSpec((nb, c, w), lambda b, t: (b, t, qkv0)),
            pl.BlockSpec((nb, c, w), lambda b, t: (b, t, qkv0 + 1)),
            pl.BlockSpec((nb, c, w), lambda b, t: (b, t, qkv0 + 2)),
            pl.BlockSpec((nb, c, w), lambda b, t: (b, t, qkv0 + 3)),
            pl.BlockSpec((nb, c, 128), lambda b, t: (b, t, 0)),
            pl.BlockSpec((None, DN_CONV, 3 * w), lambda b, t: (layer, 0, 0)),
            pl.BlockSpec((None, 1, 128), lambda b, t: (layer, 0, 0)),
            pl.BlockSpec((None, 1, 128), lambda b, t: (layer, 0, 0)),
            pl.BlockSpec((None, 1, DN_HEAD_DIM), lambda b, t: (layer, 0, 0)),
        ],
        out_specs=pl.BlockSpec((nb, c, w), lambda b, t: (b, t, 0)),
        out_shape=jax.ShapeDtypeStruct((bsz, seq, w), BF16),
        scratch_shapes=[pltpu.VMEM((nb * (DN_HEADS // 2), pw, pw), F32),
                        pltpu.VMEM((nb * 3, c, w), F32)],
        compiler_params=_params(2),
        name="gdn",
    )(proj3, proj3, proj3, proj3, small3, conv_w, alog_row, dtb_row, norm_w)
    return out.reshape(m, w)


def _route_kernel(h_ref, g_ref, r_ref, hn_ref, ti_ref, tw_ref):
    x = h_ref[...]
    ms = jnp.mean(x * x, axis=-1, keepdims=True)
    hn = x * lax.rsqrt(ms + NORM_EPS) * g_ref[...]
    hn_ref[...] = hn
    logits = jnp.dot(hn, r_ref[...], preferred_element_type=F32,
                     precision=lax.Precision.HIGHEST)
    lane = lax.broadcasted_iota(jnp.int32, logits.shape, 1)
    neg = jnp.float32(-3.0e38)
    logits = jnp.where(lane < N_EXPERTS, logits, neg)
    m1 = jnp.max(logits, axis=-1, keepdims=True)
    i1 = jnp.min(jnp.where(logits == m1, lane, 128), axis=-1, keepdims=True)
    rest = jnp.where(lane == i1, neg, logits)
    m2 = jnp.max(rest, axis=-1, keepdims=True)
    i2 = jnp.min(jnp.where(rest == m2, lane, 128), axis=-1, keepdims=True)
    e2 = jnp.exp(m2 - m1)
    w1 = 1.0 / (1.0 + e2)
    w2 = e2 / (1.0 + e2)
    ti_ref[...] = jnp.where(lane == 0, i1, jnp.where(lane == 1, i2, 0))
    tw_ref[...] = jnp.where(lane == 0, w1, jnp.where(lane == 1, w2, 0.0))


def route(h, g, layer, router_pad, j, tm=256):
    m, d = h.shape
    g = g.reshape(g.shape[0], 1, d)
    return pl.pallas_call(
        _route_kernel,
        grid=(m // tm,),
        in_specs=[pl.BlockSpec((tm, d), lambda i: (i, 0)),
                  pl.BlockSpec((None, 1, d), lambda i: (layer, 0, 0)),
                  pl.BlockSpec((None, d, 128), lambda i: (j, 0, 0))],
        out_specs=[pl.BlockSpec((tm, d), lambda i: (i, 0)),
                   pl.BlockSpec((tm, 128), lambda i: (i, 0)),
                   pl.BlockSpec((tm, 128), lambda i: (i, 0))],
        out_shape=[jax.ShapeDtypeStruct((m, d), F32),
                   jax.ShapeDtypeStruct((m, 128), jnp.int32),
                   jax.ShapeDtypeStruct((m, 128), F32)],
        compiler_params=_params(1),
        name="route",
    )(h, g, router_pad)


def _row_copy(src_ref, src_row, dst_ref, dst_row, sem):
    return pltpu.make_async_copy(src_ref.at[pl.ds(src_row, 1)], dst_ref.at[pl.ds(dst_row, 1)], sem)


def _moe_gather_kernel(pos_ref, tstart_ref, tend_ref, hn_ref, xs_ref, fill_sem, row_sem):
    step = pl.program_id(0)
    tm = MOE_TM

    n_tiles = xs_ref.shape[0] // tm
    n_active = tend_ref[N_EXPERTS - 1]

    def fill_tile(t):
        dst = pl.multiple_of(t * tm, tm)
        return pltpu.make_async_copy(hn_ref, xs_ref.at[pl.ds(dst, tm)], fill_sem)

    def fills(act):
        for e in range(N_EXPERTS):
            @pl.when(tend_ref[e] > tstart_ref[e])
            def _():
                act(fill_tile(tend_ref[e] - 1))
        for t in range(N_EXPERTS):
            @pl.when(n_active + t < n_tiles)
            def _():
                act(fill_tile(n_active + t))

    @pl.when(step == 0)
    def _():
        fills(lambda cp: cp.start())
        fills(lambda cp: cp.wait())

    base = step * tm * TOP_K

    def issue(t, carry):
        for slot in range(TOP_K):
            _row_copy(hn_ref, t, xs_ref, pos_ref[base + t * TOP_K + slot], row_sem).start()
        return carry

    lax.fori_loop(0, tm, issue, 0, unroll=4)

    def drain(i, carry):
        _row_copy(hn_ref, 0, xs_ref, 0, row_sem).wait()
        return carry

    lax.fori_loop(0, tm * TOP_K, drain, 0)


def moe_gather(hn, pos, tile_start, tile_end, rows):
    m, d = hn.shape
    tm = MOE_TM
    return pl.pallas_call(
        _moe_gather_kernel,
        grid_spec=pltpu.PrefetchScalarGridSpec(
            num_scalar_prefetch=3,
            grid=(m // tm,),
            in_specs=[pl.BlockSpec((tm, d), lambda i, pos, ts, te: (i, 0))],
            out_specs=pl.BlockSpec(memory_space=pl.ANY),
            scratch_shapes=[pltpu.SemaphoreType.DMA(()), pltpu.SemaphoreType.DMA(())],
        ),
        out_shape=jax.ShapeDtypeStruct((rows, d), F32),
        compiler_params=_params(1),
        name="moe_gather",
    )(pos, tile_start, tile_end, hn)


def _expert_changed(te_ref, r):
    prev = te_ref[jnp.maximum(r - 1, 0)]
    return jnp.logical_or(r == 0, te_ref[r] != prev)


def _moe_up_kernel(te_ref, na_ref, x_ref, wg_ref, wu_ref, o_ref, wgb_ref, wub_ref):
    r = pl.program_id(1)

    @pl.when(_expert_changed(te_ref, r))
    def _():
        _cast_rows(wg_ref, wgb_ref)
        _cast_rows(wu_ref, wub_ref)

    @pl.when(r < na_ref[0])
    def _():
        x = x_ref[...].astype(BF16)
        a = jnp.dot(x, wgb_ref[...], preferred_element_type=F32)
        b = jnp.dot(x, wub_ref[...], preferred_element_type=F32)
        o_ref[...] = (_silu(a) * b).astype(o_ref.dtype)

    @pl.when(r >= na_ref[0])
    def _():
        o_ref[...] = jnp.zeros(o_ref.shape, o_ref.dtype)


def _moe_down_kernel(te_ref, na_ref, x_ref, wd_ref, o_ref, wdb_ref):
    r = pl.program_id(1)

    @pl.when(_expert_changed(te_ref, r))
    def _():
        _cast_rows(wd_ref, wdb_ref)

    @pl.when(r < na_ref[0])
    def _():
        o_ref[...] = jnp.dot(x_ref[...], wdb_ref[...], preferred_element_type=F32)

    @pl.when(r >= na_ref[0])
    def _():
        o_ref[...] = jnp.zeros(o_ref.shape, o_ref.dtype)


def moe_experts(xs, tile_expert, n_active, w_gate, w_up, w_down, j, tf=1024, tn=512):
    rows, d = xs.shape
    tm = MOE_TM
    n_tiles = rows // tm
    f = w_gate.shape[-1]

    def active_tile(r, na):
        return jnp.where(r < na[0], r, 0)

    hmid = pl.pallas_call(
        _moe_up_kernel,
        grid_spec=pltpu.PrefetchScalarGridSpec(
            num_scalar_prefetch=2,
            grid=(f // tf, n_tiles),
            in_specs=[
                pl.BlockSpec((tm, d), lambda c, r, te, na: (active_tile(r, na), 0)),
                pl.BlockSpec((None, None, d, tf), lambda c, r, te, na: (j, te[r], 0, c)),
                pl.BlockSpec((None, None, d, tf), lambda c, r, te, na: (j, te[r], 0, c)),
            ],
            out_specs=pl.BlockSpec((tm, tf), lambda c, r, te, na: (r, c)),
            scratch_shapes=[pltpu.VMEM((d, tf), BF16), pltpu.VMEM((d, tf), BF16)],
        ),
        out_shape=jax.ShapeDtypeStruct((rows, f), BF16),
        compiler_params=_params(2),
        name="moe_up",
    )(tile_expert, n_active, xs, w_gate, w_up)
    return pl.pallas_call(
        _moe_down_kernel,
        grid_spec=pltpu.PrefetchScalarGridSpec(
            num_scalar_prefetch=2,
            grid=(d // tn, n_tiles),
            in_specs=[
                pl.BlockSpec((tm, f), lambda c, r, te, na: (r, 0)),
                pl.BlockSpec((None, None, f, tn), lambda c, r, te, na: (j, te[r], 0, c)),
            ],
            out_specs=pl.BlockSpec((tm, tn), lambda c, r, te, na: (r, c)),
            scratch_shapes=[pltpu.VMEM((f, tn), BF16)],
        ),
        out_shape=jax.ShapeDtypeStruct((rows, d), F32),
        compiler_params=_params(2),
        name="moe_down",
    )(tile_expert, n_active, hmid, w_down)


def _moe_combine_kernel(pos_ref, ys_ref, h_ref, tw_ref, g_ref, ho_ref, hn_ref, buf_ref, sem):
    tm = MOE_COMBINE_TM
    base = pl.program_id(0) * tm * TOP_K

    def issue(i, carry):
        for slot in range(TOP_K):
            _row_copy(ys_ref, pos_ref[base + i * TOP_K + slot], buf_ref.at[slot], i, sem).start()
        return carry

    lax.fori_loop(0, tm, issue, 0, unroll=4)

    def drain(i, carry):
        _row_copy(ys_ref, 0, buf_ref.at[0], 0, sem).wait()
        return carry

    lax.fori_loop(0, tm * TOP_K, drain, 0)

    tw = tw_ref[...]
    x = h_ref[...] + tw[:, 0:1] * buf_ref[0] + tw[:, 1:2] * buf_ref[1]
    ho_ref[...] = x
    ms = jnp.mean(x * x, axis=-1, keepdims=True)
    hn_ref[...] = (x * lax.rsqrt(ms + NORM_EPS) * g_ref[...]).astype(hn_ref.dtype)


def moe_combine(ys, pos, h, tw, norm_g, layer):
    m, d = h.shape
    tm = MOE_COMBINE_TM
    norm_g = norm_g.reshape(norm_g.shape[0], 1, d)
    return pl.pallas_call(
        _moe_combine_kernel,
        grid_spec=pltpu.PrefetchScalarGridSpec(
            num_scalar_prefetch=1,
            grid=(m // tm,),
            in_specs=[
                pl.BlockSpec(memory_space=pl.ANY),
                pl.BlockSpec((tm, d), lambda i, pos: (i, 0)),
                pl.BlockSpec((tm, 128), lambda i, pos: (i, 0)),
                pl.BlockSpec((None, 1, d), lambda i, pos: (layer, 0, 0)),
            ],
            out_specs=[pl.BlockSpec((tm, d), lambda i, pos: (i, 0)),
                       pl.BlockSpec((tm, d), lambda i, pos: (i, 0))],
            scratch_shapes=[pltpu.VMEM((TOP_K, tm, d), F32), pltpu.SemaphoreType.DMA(())],
        ),
        out_shape=[jax.ShapeDtypeStruct((m, d), F32), jax.ShapeDtypeStruct((m, d), BF16)],
        compiler_params=_params(1),
        name="moe_combine",
    )(pos, ys, h, tw, norm_g)


def moe_layer(h, norm_g, layer, router_pad, w_gate, w_up, w_down, j, next_norm_g):
    m, d = h.shape
    tm = MOE_TM
    hn, ti, tw = route(h, norm_g, layer, router_pad, j)
    top_i = ti[:, :TOP_K].reshape(-1)
    n_pairs = m * TOP_K
    n_tiles = n_pairs // tm + N_EXPERTS
    onehot = (top_i[:, None] == jnp.arange(N_EXPERTS, dtype=jnp.int32)[None, :]).astype(jnp.int32)
    csum = jnp.cumsum(onehot, axis=0)
    rank = jnp.sum((csum - onehot) * onehot, axis=1)
    counts = csum[-1]
    tiles_per = (counts + tm - 1) // tm
    tile_end = jnp.cumsum(tiles_per).astype(jnp.int32)
    tile_start = (tile_end - tiles_per).astype(jnp.int32)
    pos = (jnp.sum(onehot * tile_start[None, :], axis=1) * tm + rank).astype(jnp.int32)
    n_active = tile_end[-1:]
    tile_ids = jnp.arange(n_tiles, dtype=jnp.int32)
    tile_expert = jnp.sum((tile_ids[:, None] >= tile_end[None, :]).astype(jnp.int32), axis=1)
    last_expert = tile_expert[jnp.maximum(n_active[0] - 1, 0)]
    tile_expert = jnp.where(tile_ids < n_active[0], tile_expert, last_expert).astype(jnp.int32)

    xs = moe_gather(hn, pos, tile_start, tile_end, n_tiles * tm)
    ys = moe_experts(xs, tile_expert, n_active, w_gate, w_up, w_down, j)
    return moe_combine(ys, pos, h, tw, next_norm_g, layer)


def kernel(x, p, norm_mix, w_in, sgu_ln_g, sgu_ln_b, sgu_w, sgu_b, dn_conv_w, dn_a_log,
           dn_dt_bias, dn_norm_w, w_branch, w_out, norm_ffn, ffn_w_gate, ffn_w_up,
           ffn_w_down, moe_router, moe_w_gate, moe_w_up, moe_w_down, norm_ple,
           ple_w_gate, ple_b_gate, ple_w_proj, norm_final):
    bsz, seq, d = x.shape
    depth = p.shape[0]
    m = bsz * seq
    h = x.reshape(m, d)
    p2 = p.reshape(depth, m, p.shape[-1])
    main_w = 2 * SGU_WIDTH + 4 * DN_WIDTH
    n_small = 2 * DN_HEADS
    w_in_t = jnp.swapaxes(w_in, 1, 2)
    pad_h = 128 - n_small
    alog_row = jnp.pad(dn_a_log, ((0, 0), (DN_HEADS, pad_h))).reshape(depth, 1, 128)
    dtb_row = jnp.pad(dn_dt_bias, ((0, 0), (DN_HEADS, pad_h))).reshape(depth, 1, 128)
    ln_g = sgu_ln_g.reshape(depth, 1, SGU_WIDTH)
    ln_b = sgu_ln_b.reshape(depth, 1, SGU_WIDTH)
    sgu_b_t = jnp.swapaxes(sgu_b, 1, 2)
    nw = dn_norm_w.reshape(depth, 1, DN_HEAD_DIM)
    router_pad = jnp.pad(moe_router, ((0, 0), (0, 0), (0, 128 - N_EXPERTS)))
    ple_b = ple_b_gate.reshape(depth, 1, d)

    for i in range(depth):
        hn = rmsnorm(h, norm_mix, i, BF16)
        uv = fused_mm([(hn, 0)], [(w_in_t, (i,), 0, 0)], [], _ep_gelu, 2 * SGU_WIDTH, BF16,
                      1024, 512, "proj_uv", w_t=True)
        qkvz = fused_mm([(hn, 0)], [(w_in_t, (i,), (2 * SGU_WIDTH) // 512, 0)], [], _ep_plain,
                        4 * DN_WIDTH, BF16, 1024, 512, "proj_qkvz", w_t=True)
        gates = shifted_sigmoid_mm(hn, w_in_t, i, main_w, n_small, 2 * d, 1024, 512, "proj_gates")
        small = fused_mm([(hn, 0)], [(w_in_t, (i,), main_w // 128, 0)], [], _ep_plain, 128, F32,
                         1024, 128, "proj_small", w_t=True)
        y_a = sgu(uv, ln_g, ln_b, sgu_w, sgu_b_t, i)
        y_b = gdn(qkvz, small, dn_conv_w, alog_row, dtb_row, nw, i, bsz, seq, nb=2)
        merged = fused_mm([(y_a, 0), (y_b, 0)],
                          [(w_branch, (i, 0), 0, 0), (w_branch, (i, 1), 0, 1)],
                          [(gates, 0, "tile", 0), (gates, 0, "tile", d // 512)],
                          _ep_merge, d, BF16, 1024, 512, "merge")
        h = fused_mm([(merged, 0)], [(w_out, (i,), 0, 0)], [(h, 0, "tile", 0)],
                     _ep_residual, d, F32, 1024, 512, "out_proj")
        j = i // 2
        if i % 2 == 0:
            hn = rmsnorm(h, norm_ffn, i, BF16)
            hmid = fused_mm([(hn, 0)], [(ffn_w_gate, (j,), 0, 0), (ffn_w_up, (j,), 0, 0)], [],
                            _ep_swiglu, ffn_w_gate.shape[-1], BF16, 1024, 512, "ffn_up")
            h = fused_mm([(hmid, 0)], [(ffn_w_down, (j,), 0, 0)], [(h, 0, "tile", 0)],
                         _ep_residual, d, F32, 512, 512, "ffn_down")
            hn = rmsnorm(h, norm_ple, i, BF16)
        else:
            h, hn = moe_layer(h, norm_ffn, i, router_pad, moe_w_gate, moe_w_up, moe_w_down, j,
                              norm_ple)
        h = fused_mm([(hn, 0), (p2, i)],
                     [(ple_w_gate, (i,), 0, 0), (ple_w_proj, (i,), 0, 1)],
                     [(h, 0, "tile", 0), (ple_b, i, "row", 0)],
                     _ep_ple, d, F32, 1024, 512, "ple")
    out = rmsnorm(h, norm_final, 0, F32)
    return out.reshape(bsz, seq, d)
```

```python
import functools
import math

import jax
import jax.numpy as jnp
from jax import lax
from jax.experimental import pallas as pl
from jax.experimental.pallas import tpu as pltpu

F32 = jnp.float32
BF16 = jnp.bfloat16

D_MODEL = 2048
SGU_CHUNK = 128
SGU_GROUPS = 8
SGU_WIDTH = 1024
DN_HEADS = 8
DN_HEAD_DIM = 128
DN_WIDTH = 1024
DN_CONV = 4
DN_CHUNK = 64
N_EXPERTS = 8
TOP_K = 2
NORM_EPS = 1e-6

V7X_VMEM_LIMIT = 56 * 1024 * 1024
CAST_ROWS = 256
MOE_TM = 256
MOE_COMBINE_TM = 256


def _params(n_axes):
    return pltpu.CompilerParams(
        dimension_semantics=("arbitrary",) * n_axes,
        vmem_limit_bytes=V7X_VMEM_LIMIT)


def _sigmoid(x):
    return 1.0 / (1.0 + jnp.exp(-x))


def _silu(x):
    return x * _sigmoid(x)


def _gelu_tanh(x):
    c = math.sqrt(2.0 / math.pi)
    return 0.5 * x * (1.0 + jnp.tanh(c * (x + 0.044715 * (x * x * x))))


def _cast_rows(w_ref, wb_ref):
    k = w_ref.shape[0]
    ch = min(CAST_ROWS, k)

    def body(c, carry):
        r = pl.multiple_of(c * ch, ch)
        wb_ref[pl.ds(r, ch), :] = w_ref[pl.ds(r, ch), :].astype(BF16)
        return carry

    lax.fori_loop(0, k // ch, body, 0)


def _cast_rows_transposed(wt_ref, wb_ref):
    tn = wt_ref.shape[0]
    ch = min(CAST_ROWS, tn)
    for c in range(tn // ch):
        wb_ref[:, c * ch:(c + 1) * ch] = wt_ref[c * ch:(c + 1) * ch, :].T.astype(BF16)


def _rmsnorm_kernel(h_ref, g_ref, o_ref):
    x = h_ref[...]
    ms = jnp.mean(x * x, axis=-1, keepdims=True)
    o_ref[...] = (x * lax.rsqrt(ms + NORM_EPS) * g_ref[...]).astype(o_ref.dtype)


def rmsnorm(h, g, layer, out_dtype, tm=512):
    m, d = h.shape
    if g.ndim == 1:
        g = g.reshape(1, 1, d)
        layer = 0
    else:
        g = g.reshape(g.shape[0], 1, d)
    return pl.pallas_call(
        _rmsnorm_kernel,
        grid=(m // tm,),
        in_specs=[pl.BlockSpec((tm, d), lambda i: (i, 0)),
                  pl.BlockSpec((None, 1, d), lambda i: (layer, 0, 0))],
        out_specs=pl.BlockSpec((tm, d), lambda i: (i, 0)),
        out_shape=jax.ShapeDtypeStruct((m, d), out_dtype),
        compiler_params=_params(1),
        name="rmsnorm",
    )(h, g)


def _fused_mm_kernel(*refs, n_x, n_w, w_x, n_e, epilogue, w_t):
    x_refs = refs[:n_x]
    w_refs = refs[n_x:n_x + n_w]
    e_refs = refs[n_x + n_w:n_x + n_w + n_e]
    o_ref = refs[n_x + n_w + n_e]
    wb_refs = refs[n_x + n_w + n_e + 1:]

    @pl.when(pl.program_id(1) == 0)
    def _():
        for w_ref, wb_ref in zip(w_refs, wb_refs):
            (_cast_rows_transposed if w_t else _cast_rows)(w_ref, wb_ref)

    xs = [x_ref[...].astype(BF16) for x_ref in x_refs]
    accs = [jnp.dot(xs[w_x[l]], wb_refs[l][...], preferred_element_type=F32)
            for l in range(n_w)]
    o_ref[...] = epilogue(accs, [e[...] for e in e_refs], pl.program_id(0)).astype(o_ref.dtype)


def fused_mm(xs, ws, extras, epilogue, n_out, out_dtype, tm, tn, name, w_t=False):
    m = xs[0][0].shape[-2]
    grid = (n_out // tn, m // tm)
    in_specs, args = [], []
    for arr, lead in xs:
        k = arr.shape[-1]
        if arr.ndim == 2:
            in_specs.append(pl.BlockSpec((tm, k), lambda j, i: (i, 0)))
        else:
            in_specs.append(pl.BlockSpec((None, tm, k), lambda j, i, lead=lead: (lead, i, 0)))
        args.append(arr)
    scratch = []
    for arr, lead, off, _ in ws:
        k = arr.shape[-1] if w_t else arr.shape[-2]
        nlead = arr.ndim - 2
        lead = tuple(lead) if nlead else ()
        if w_t:
            in_specs.append(pl.BlockSpec(
                (None,) * nlead + (tn, k),
                lambda j, i, lead=lead, off=off: lead + (j + off, 0)))
        else:
            in_specs.append(pl.BlockSpec(
                (None,) * nlead + (k, tn),
                lambda j, i, lead=lead, off=off: lead + (0, j + off)))
        args.append(arr)
        scratch.append(pltpu.VMEM((k, tn), BF16))
    for arr, lead, kind, off in extras:
        if kind == "tile":
            if arr.ndim == 2:
                in_specs.append(pl.BlockSpec((tm, tn), lambda j, i, off=off: (i, j + off)))
            else:
                in_specs.append(pl.BlockSpec(
                    (None, tm, tn), lambda j, i, lead=lead, off=off: (lead, i, j + off)))
        else:
            in_specs.append(pl.BlockSpec(
                (None, 1, tn), lambda j, i, lead=lead, off=off: (lead, 0, j + off)))
        args.append(arr)
    kern = functools.partial(
        _fused_mm_kernel, n_x=len(xs), n_w=len(ws), w_x=tuple(w[3] for w in ws),
        n_e=len(extras), epilogue=epilogue, w_t=w_t)
    return pl.pallas_call(
        kern,
        grid=grid,
        in_specs=in_specs,
        out_specs=pl.BlockSpec((tm, tn), lambda j, i: (i, j)),
        out_shape=jax.ShapeDtypeStruct((m, n_out), out_dtype),
        scratch_shapes=scratch,
        compiler_params=_params(2),
        name=name,
    )(*args)


def _shifted_mm_kernel(x_ref, wa_ref, wn_ref, o_ref, wb_ref, *, shift):
    tn = wa_ref.shape[0]

    @pl.when(pl.program_id(1) == 0)
    def _():
        for c in range(tn // CAST_ROWS):
            lo = shift + c * CAST_ROWS
            hi = lo + CAST_ROWS
            if hi <= tn:
                rows = wa_ref[lo:hi, :]
            else:
                rows = jnp.concatenate([wa_ref[lo:tn, :], wn_ref[0:hi - tn, :]], axis=0)
            wb_ref[:, c * CAST_ROWS:(c + 1) * CAST_ROWS] = rows.T.astype(BF16)

    acc = jnp.dot(x_ref[...], wb_ref[...], preferred_element_type=F32)
    o_ref[...] = _sigmoid(acc).astype(o_ref.dtype)


def shifted_sigmoid_mm(x, wt, layer, row0, shift, n_out, tm, tn, name):
    m, k = x.shape
    base = row0 // tn
    return pl.pallas_call(
        functools.partial(_shifted_mm_kernel, shift=shift),
        grid=(n_out // tn, m // tm),
        in_specs=[
            pl.BlockSpec((tm, k), lambda j, i: (i, 0)),
            pl.BlockSpec((None, tn, k), lambda j, i: (layer, base + j, 0)),
            pl.BlockSpec((None, shift, k), lambda j, i: (layer, (base + j + 1) * (tn // shift), 0)),
        ],
        out_specs=pl.BlockSpec((tm, tn), lambda j, i: (i, j)),
        out_shape=jax.ShapeDtypeStruct((m, n_out), BF16),
        scratch_shapes=[pltpu.VMEM((k, tn), BF16)],
        compiler_params=_params(2),
        name=name,
    )(x, wt, wt)


def _ep_gelu(accs, extras, j):
    return _gelu_tanh(accs[0])


def _ep_sigmoid(accs, extras, j):
    return _sigmoid(accs[0])


def _ep_plain(accs, extras, j):
    return accs[0]


def _ep_merge(accs, extras, j):
    return extras[0].astype(F32) * accs[0] + extras[1].astype(F32) * accs[1]


def _ep_residual(accs, extras, j):
    return extras[0] + accs[0]


def _ep_swiglu(accs, extras, j):
    return _silu(accs[0]) * accs[1]


def _ep_ple(accs, extras, j):
    return extras[0] + _sigmoid(accs[0] + extras[1]) * accs[1]


def _sgu_kernel(u_ref, v_ref, lng_ref, lnb_ref, w_ref, bt_ref, o_ref, wm_ref):
    c = SGU_CHUNK

    @pl.when(pl.program_id(0) == 0)
    def _():
        ti = lax.broadcasted_iota(jnp.int32, (c, c), 0)
        si = lax.broadcasted_iota(jnp.int32, (c, c), 1)
        for g in range(SGU_GROUPS):
            wm_ref[g] = jnp.where(si <= ti, w_ref[g], 0.0).astype(BF16)

    v = v_ref[...].astype(F32)
    mu = jnp.mean(v, axis=-1, keepdims=True)
    vc = v - mu
    var = jnp.mean(vc * vc, axis=-1, keepdims=True)
    vln = (vc * lax.rsqrt(var + NORM_EPS) * lng_ref[...] + lnb_ref[...]).astype(BF16)
    bt = bt_ref[...]
    for g in range(SGU_GROUPS):
        sl = slice(g * c, (g + 1) * c)
        mixed = jnp.dot(wm_ref[g], vln[:, sl], preferred_element_type=F32) + bt[:, g:g + 1]
        o_ref[:, sl] = (u_ref[:, sl].astype(F32) * mixed).astype(o_ref.dtype)


def sgu(proj, ln_g, ln_b, w_s, b_s_t, layer):
    m = proj.shape[0]
    c = SGU_CHUNK
    return pl.pallas_call(
        _sgu_kernel,
        grid=(m // c,),
        in_specs=[
            pl.BlockSpec((c, SGU_WIDTH), lambda i: (i, 0)),
            pl.BlockSpec((c, SGU_WIDTH), lambda i: (i, 1)),
            pl.BlockSpec((None, 1, SGU_WIDTH), lambda i: (layer, 0, 0)),
            pl.BlockSpec((None, 1, SGU_WIDTH), lambda i: (layer, 0, 0)),
            pl.BlockSpec((None, SGU_GROUPS, c, c), lambda i: (layer, 0, 0, 0)),
            pl.BlockSpec((None, c, SGU_GROUPS), lambda i: (layer, 0, 0)),
        ],
        out_specs=pl.BlockSpec((c, SGU_WIDTH), lambda i: (i, 0)),
        out_shape=jax.ShapeDtypeStruct((m, SGU_WIDTH), BF16),
        scratch_shapes=[pltpu.VMEM((SGU_GROUPS, c, c), BF16)],
        compiler_params=_params(1),
        name="sgu",
    )(proj, proj, ln_g, ln_b, w_s, b_s_t)


def _bdot(a, b):
    return jnp.dot(a.astype(BF16), b.astype(BF16), preferred_element_type=F32)


def _bdot_nt(a, b):
    return lax.dot_general(a.astype(BF16), b.astype(BF16), (((1,), (1,)), ((), ())),
                           preferred_element_type=F32)


def _gdn_kernel(q_ref, k_ref, v_ref, z_ref, sm_ref, cw_ref, alog_ref, dtb_ref, nw_ref,
                o_ref, s_ref, prev_ref, *, nb):
    c = DN_CHUNK
    hd = DN_HEAD_DIM
    w = DN_WIDTH
    pw = 2 * hd
    n_pairs = DN_HEADS // 2

    @pl.when(pl.program_id(1) == 0)
    def _():
        s_ref[...] = jnp.zeros(s_ref.shape, F32)
        prev_ref[...] = jnp.zeros(prev_ref.shape, F32)

    row_w = lax.broadcasted_iota(jnp.int32, (c, w), 0)
    row_s = lax.broadcasted_iota(jnp.int32, (c, 128), 0)
    lane = lax.broadcasted_iota(jnp.int32, (c, 128), 1)
    left = lane < c
    jmod = jnp.where(left, lane, lane - c)
    tri = row_s >= jmod
    strict = row_s > jmod
    eye = (row_s == jmod).astype(F32)
    r256 = lax.broadcasted_iota(jnp.int32, (pw, pw), 0)
    c256 = lax.broadcasted_iota(jnp.int32, (pw, pw), 1)
    bd_mask = (r256 < hd) == (c256 < hd)
    zeros_h = jnp.zeros((c, hd), F32)
    nw = nw_ref[...]
    scale = hd ** -0.5

    def bcast2(col_a, col_b):
        return jnp.concatenate([jnp.broadcast_to(col_a, (c, hd)),
                                jnp.broadcast_to(col_b, (c, hd))], axis=1)

    def blockdiag_rows(x):
        return jnp.concatenate([jnp.where(left, x, 0.0), jnp.where(left, 0.0, x)], axis=0)

    def blockdiag_heads(x):
        return jnp.concatenate(
            [jnp.concatenate([x[:, :hd], zeros_h], axis=1),
             jnp.concatenate([zeros_h, x[:, hd:]], axis=1)], axis=0)

    def inv_norm(x):
        ss_a = jnp.sum(x[:, :hd] * x[:, :hd], axis=-1, keepdims=True)
        ss_b = jnp.sum(x[:, hd:] * x[:, hd:], axis=-1, keepdims=True)
        return bcast2(lax.rsqrt(ss_a + NORM_EPS), lax.rsqrt(ss_b + NORM_EPS))

    chains = []
    for b in range(nb):
        def conv_silu(x_ref, sec, b=b):
            x = x_ref[b].astype(F32)
            prev = prev_ref[b * 3 + sec]
            cw = cw_ref[:, sec * w:(sec + 1) * w]
            acc = cw[DN_CONV - 1:DN_CONV, :] * x
            for s in range(1, DN_CONV):
                shifted = jnp.where(row_w < s, pltpu.roll(prev, s, 0), pltpu.roll(x, s, 0))
                acc = acc + cw[DN_CONV - 1 - s:DN_CONV - s, :] * shifted
            prev_ref[b * 3 + sec] = x
            return _silu(acc)

        q_all = conv_silu(q_ref, 0)
        k_all = conv_silu(k_ref, 1)
        v_all = conv_silu(v_ref, 2)

        sm = sm_ref[b]
        beta_all = _sigmoid(sm)
        xa = sm + dtb_ref[...]
        softplus = jnp.maximum(xa, 0.0) + jnp.log(1.0 + jnp.exp(-jnp.abs(xa)))
        g = -jnp.exp(alog_ref[...]) * softplus
        s = 1
        while s < c:
            g = g + jnp.where(row_s >= s, pltpu.roll(g, s, 0), 0.0)
            s *= 2
        g_t = jnp.concatenate([g, pltpu.roll(g, 127, 1)], axis=0).T

        for p in range(n_pairs):
            a = 2 * p
            sl = slice(p * pw, (p + 1) * pw)
            q = q_all[:, sl]
            k = k_all[:, sl]
            v = v_all[:, sl]
            q = q * (inv_norm(q) * scale)
            k = k * inv_norm(k)
            gc_a = g[:, DN_HEADS + a:DN_HEADS + a + 1]
            gc_b = g[:, DN_HEADS + a + 1:DN_HEADS + a + 2]
            gl_a = gc_a[c - 1:c, :]
            gl_b = gc_b[c - 1:c, :]
            grow = g_t[DN_HEADS + a:DN_HEADS + a + 1, :]
            gcol = jnp.where(left, gc_a, gc_b)
            decay = jnp.exp(jnp.where(tri, gcol - grow, -1e30))
            beta = bcast2(beta_all[:, a:a + 1], beta_all[:, a + 1:a + 2])
            egc = bcast2(jnp.exp(gc_a), jnp.exp(gc_b))
            kb = k * beta
            chains.append(dict(
                b=b, p=p, q=q, k=k, kb=kb, vb=v * beta, kbg=kb * egc, qg=q * egc,
                kd=k * bcast2(jnp.exp(gl_a - gc_a), jnp.exp(gl_b - gc_b)),
                dl=jnp.concatenate([jnp.broadcast_to(jnp.exp(gl_a), (1, hd)),
                                    jnp.broadcast_to(jnp.exp(gl_b), (1, hd))], axis=1),
                decay=decay))

    for ch in chains:
        kq = _bdot_nt(jnp.concatenate([ch["kb"], ch["q"]], axis=0), blockdiag_heads(ch["k"]))
        ch["bk"] = -jnp.where(strict, kq[:c] * ch["decay"], 0.0)
        ch["attn"] = jnp.where(tri, kq[c:] * ch["decay"], 0.0)
        ch["qm"] = eye + ch["bk"]
    for ch in chains:
        ch["bk"] = _bdot(ch["bk"], blockdiag_rows(ch["bk"]))
    for lvl in range(1, 6):
        for ch in chains:
            if lvl < 5:
                r = _bdot(jnp.concatenate([ch["qm"], ch["bk"]], axis=0), blockdiag_rows(ch["bk"]))
                ch["qm"] = ch["qm"] + r[:c]
                ch["bk"] = r[c:]
            else:
                ch["qm"] = ch["qm"] + _bdot(ch["qm"], blockdiag_rows(ch["bk"]))
    for ch in chains:
        vb, kbg = ch["vb"], ch["kbg"]
        rhs = jnp.concatenate(
            [jnp.concatenate([vb[:, :hd], kbg[:, :hd], zeros_h, zeros_h], axis=1),
             jnp.concatenate([zeros_h, zeros_h, vb[:, hd:], kbg[:, hd:]], axis=1)], axis=0)
        sol = _bdot(ch["qm"], rhs)
        ch["u"] = jnp.concatenate([sol[:, :hd], sol[:, 2 * hd:3 * hd]], axis=1)
        ch["w"] = jnp.concatenate([sol[:, hd:2 * hd], sol[:, 3 * hd:]], axis=1)
    for ch in chains:
        ch["state"] = s_ref[ch["b"] * n_pairs + ch["p"]]
        r2 = _bdot(jnp.concatenate([ch["w"], ch["qg"]], axis=0), ch["state"])
        ch["vn"] = ch["u"] - r2[:c]
        ch["o"] = r2[c:]
    for ch in chains:
        ch["o"] = ch["o"] + _bdot(ch["attn"], blockdiag_heads(ch["vn"]))
    for ch in chains:
        upd = ch["state"] * ch["dl"] + _bdot(ch["kd"].T, ch["vn"])
        s_ref[ch["b"] * n_pairs + ch["p"]] = jnp.where(bd_mask, upd, 0.0)
    for ch in chains:
        b, p = ch["b"], ch["p"]
        for half in range(2):
            o = ch["o"][:, half * hd:(half + 1) * hd]
            sl = slice(p * pw + half * hd, p * pw + (half + 1) * hd)
            o = o * lax.rsqrt(jnp.mean(o * o, axis=-1, keepdims=True) + NORM_EPS)
            o = o * nw * _silu(z_ref[b, :, sl].astype(F32))
            o_ref[b, :, sl] = o.astype(o_ref.dtype)


def gdn(proj, small, conv_w, alog_row, dtb_row, norm_w, layer, bsz, seq, nb=1):
    m = proj.shape[0]
    c = DN_CHUNK
    n = seq // c
    w = DN_WIDTH
    qkv0 = 0
    proj3 = proj.reshape(bsz, seq, proj.shape[-1])
    small3 = small.reshape(bsz, seq, small.shape[-1])
    pw = 2 * DN_HEAD_DIM
    out = pl.pallas_call(
        functools.partial(_gdn_kernel, nb=nb),
        grid=(bsz // nb, n),
        in_specs=[
            pl.BlockSpec((nb, c, w), lambda b, t: (b, t, qkv0)),
            pl.BlockSpec((nb, c, w), lambda b, t: (b, t, qkv0 + 1)),
            pl.BlockSpec((nb, c, w), lambda b, t: (b, t, qkv0 + 2)),
            pl.BlockSpec((nb, c, w), lambda b, t: (b, t, qkv0 + 3)),
            pl.BlockSpec((nb, c, 128), lambda b, t: (b, t, 0)),
            pl.BlockSpec((None, DN_CONV, 3 * w), lambda b, t: (layer, 0, 0)),
            pl.BlockSpec((None, 1, 128), lambda b, t: (layer, 0, 0)),
            pl.BlockSpec((None, 1, 128), lambda b, t: (layer, 0, 0)),
            pl.BlockSpec((None, 1, DN_HEAD_DIM), lambda b, t: (layer, 0, 0)),
        ],
        out_specs=pl.BlockSpec((nb, c, w), lambda b, t: (b, t, 0)),
        out_shape=jax.ShapeDtypeStruct((bsz, seq, w), BF16),
        scratch_shapes=[pltpu.VMEM((nb * (DN_HEADS // 2), pw, pw), F32),
                        pltpu.VMEM((nb * 3, c, w), F32)],
        compiler_params=_params(2),
        name="gdn",
    )(proj3, proj3, proj3, proj3, small3, conv_w, alog_row, dtb_row, norm_w)
    return out.reshape(m, w)


def _route_kernel(h_ref, g_ref, r_ref, hn_ref, ti_ref, tw_ref):
    x = h_ref[...]
    ms = jnp.mean(x * x, axis=-1, keepdims=True)
    hn = x * lax.rsqrt(ms + NORM_EPS) * g_ref[...]
    hn_ref[...] = hn
    logits = jnp.dot(hn, r_ref[...], preferred_element_type=F32,
                     precision=lax.Precision.HIGHEST)
    lane = lax.broadcasted_iota(jnp.int32, logits.shape, 1)
    neg = jnp.float32(-3.0e38)
    logits = jnp.where(lane < N_EXPERTS, logits, neg)
    m1 = jnp.max(logits, axis=-1, keepdims=True)
    i1 = jnp.min(jnp.where(logits == m1, lane, 128), axis=-1, keepdims=True)
    rest = jnp.where(lane == i1, neg, logits)
    m2 = jnp.max(rest, axis=-1, keepdims=True)
    i2 = jnp.min(jnp.where(rest == m2, lane, 128), axis=-1, keepdims=True)
    e2 = jnp.exp(m2 - m1)
    w1 = 1.0 / (1.0 + e2)
    w2 = e2 / (1.0 + e2)
    ti_ref[...] = jnp.where(lane == 0, i1, jnp.where(lane == 1, i2, 0))
    tw_ref[...] = jnp.where(lane == 0, w1, jnp.where(lane == 1, w2, 0.0))


def route(h, g, layer, router_pad, j, tm=256):
    m, d = h.shape
    g = g.reshape(g.shape[0], 1, d)
    return pl.pallas_call(
        _route_kernel,
        grid=(m // tm,),
        in_specs=[pl.BlockSpec((tm, d), lambda i: (i, 0)),
                  pl.BlockSpec((None, 1, d), lambda i: (layer, 0, 0)),
                  pl.BlockSpec((None, d, 128), lambda i: (j, 0, 0))],
        out_specs=[pl.BlockSpec((tm, d), lambda i: (i, 0)),
                   pl.BlockSpec((tm, 128), lambda i: (i, 0)),
                   pl.BlockSpec((tm, 128), lambda i: (i, 0))],
        out_shape=[jax.ShapeDtypeStruct((m, d), F32),
                   jax.ShapeDtypeStruct((m, 128), jnp.int32),
                   jax.ShapeDtypeStruct((m, 128), F32)],
        compiler_params=_params(1),
        name="route",
    )(h, g, router_pad)


def _row_copy(src_ref, src_row, dst_ref, dst_row, sem):
    return pltpu.make_async_copy(src_ref.at[pl.ds(src_row, 1)], dst_ref.at[pl.ds(dst_row, 1)], sem)


def _moe_gather_kernel(pos_ref, tstart_ref, tend_ref, hn_ref, xs_ref, fill_sem, row_sem):
    step = pl.program_id(0)
    tm = MOE_TM

    n_tiles = xs_ref.shape[0] // tm
    n_active = tend_ref[N_EXPERTS - 1]

    def fill_tile(t):
        dst = pl.multiple_of(t * tm, tm)
        return pltpu.make_async_copy(hn_ref, xs_ref.at[pl.ds(dst, tm)], fill_sem)

    def fills(act):
        for e in range(N_EXPERTS):
            @pl.when(tend_ref[e] > tstart_ref[e])
            def _():
                act(fill_tile(tend_ref[e] - 1))
        for t in range(N_EXPERTS):
            @pl.when(n_active + t < n_tiles)
            def _():
                act(fill_tile(n_active + t))

    @pl.when(step == 0)
    def _():
        fills(lambda cp: cp.start())
        fills(lambda cp: cp.wait())

    base = step * tm * TOP_K

    def issue(t, carry):
        for slot in range(TOP_K):
            _row_copy(hn_ref, t, xs_ref, pos_ref[base + t * TOP_K + slot], row_sem).start()
        return carry

    lax.fori_loop(0, tm, issue, 0, unroll=4)

    def drain(i, carry):
        _row_copy(hn_ref, 0, xs_ref, 0, row_sem).wait()
        return carry

    lax.fori_loop(0, tm * TOP_K, drain, 0)


def moe_gather(hn, pos, tile_start, tile_end, rows):
    m, d = hn.shape
    tm = MOE_TM
    return pl.pallas_call(
        _moe_gather_kernel,
        grid_spec=pltpu.PrefetchScalarGridSpec(
            num_scalar_prefetch=3,
            grid=(m // tm,),
            in_specs=[pl.BlockSpec((tm, d), lambda i, pos, ts, te: (i, 0))],
            out_specs=pl.BlockSpec(memory_space=pl.ANY),
            scratch_shapes=[pltpu.SemaphoreType.DMA(()), pltpu.SemaphoreType.DMA(())],
        ),
        out_shape=jax.ShapeDtypeStruct((rows, d), F32),
        compiler_params=_params(1),
        name="moe_gather",
    )(pos, tile_start, tile_end, hn)


def _moe_group_kernel(ts_ref, te_ref, x_ref, *refs, n_w, compute):
    w_refs = refs[:n_w]
    o_ref = refs[n_w]
    wb_refs = refs[n_w + 1:2 * n_w + 1]
    xbuf, obuf, xsem, osem = refs[2 * n_w + 1:]
    tm, tn = obuf.shape[1], obuf.shape[2]
    n_tiles = o_ref.shape[0] // tm
    e = pl.program_id(1)
    col = pl.multiple_of(pl.program_id(0) * tn, tn)
    t0 = ts_ref[e]
    n = te_ref[e] - t0

    def x_copy(t, slot):
        row = pl.multiple_of((t0 + t) * tm, tm)
        return pltpu.make_async_copy(x_ref.at[pl.ds(row, tm)], xbuf.at[slot], xsem.at[slot])

    def o_copy(tile, slot):
        row = pl.multiple_of(tile * tm, tm)
        return pltpu.make_async_copy(
            obuf.at[slot], o_ref.at[pl.ds(row, tm), pl.ds(col, tn)], osem.at[slot])

    @pl.when(n > 0)
    def _():
        x_copy(0, 0).start()
        for w_ref, wb_ref in zip(w_refs, wb_refs):
            _cast_rows(w_ref, wb_ref)

        def body(t, carry):
            slot = lax.rem(t, 2)
            x_copy(t, slot).wait()

            @pl.when(t + 1 < n)
            def _():
                x_copy(t + 1, 1 - slot).start()

            @pl.when(t >= 2)
            def _():
                o_copy(t0 + t - 2, slot).wait()

            obuf[slot] = compute(xbuf[slot], [wb[...] for wb in wb_refs]).astype(obuf.dtype)
            o_copy(t0 + t, slot).start()
            return carry

        lax.fori_loop(0, n, body, 0)

        @pl.when(n >= 2)
        def _():
            o_copy(t0 + n - 2, lax.rem(n, 2)).wait()

        o_copy(t0 + n - 1, lax.rem(n - 1, 2)).wait()

    @pl.when(e == N_EXPERTS - 1)
    def _():
        n_active = te_ref[N_EXPERTS - 1]
        obuf[0] = jnp.zeros(obuf.shape[1:], obuf.dtype)
        for k in range(N_EXPERTS):
            @pl.when(n_active + k < n_tiles)
            def _():
                o_copy(n_active + k, 0).start()
        for k in range(N_EXPERTS):
            @pl.when(n_active + k < n_tiles)
            def _():
                o_copy(n_active + k, 0).wait()


def _moe_up_compute(x, wbs):
    xb = x.astype(BF16)
    a = jnp.dot(xb, wbs[0], preferred_element_type=F32)
    b = jnp.dot(xb, wbs[1], preferred_element_type=F32)
    return _silu(a) * b


def _moe_down_compute(x, wbs):
    return jnp.dot(x, wbs[0], preferred_element_type=F32)


def moe_group_mm(x, ws, j, tile_start, tile_end, compute, n_out, out_dtype, tn, name):
    rows, k = x.shape
    tm = MOE_TM
    w_specs = [pl.BlockSpec((None, None, k, tn), lambda c, e, ts, te: (j, e, 0, c)) for _ in ws]
    return pl.pallas_call(
        functools.partial(_moe_group_kernel, n_w=len(ws), compute=compute),
        grid_spec=pltpu.PrefetchScalarGridSpec(
            num_scalar_prefetch=2,
            grid=(n_out // tn, N_EXPERTS),
            in_specs=[pl.BlockSpec(memory_space=pl.ANY)] + w_specs,
            out_specs=pl.BlockSpec(memory_space=pl.ANY),
            scratch_shapes=[pltpu.VMEM((k, tn), BF16) for _ in ws] + [
                pltpu.VMEM((2, tm, k), x.dtype),
                pltpu.VMEM((2, tm, tn), out_dtype),
                pltpu.SemaphoreType.DMA((2,)),
                pltpu.SemaphoreType.DMA((2,)),
            ],
        ),
        out_shape=jax.ShapeDtypeStruct((rows, n_out), out_dtype),
        compiler_params=_params(2),
        name=name,
    )(tile_start, tile_end, x, *ws)


def moe_experts(xs, tile_start, tile_end, w_gate, w_up, w_down, j, tf=1024, tn=512):
    d = xs.shape[1]
    f = w_gate.shape[-1]
    hmid = moe_group_mm(xs, [w_gate, w_up], j, tile_start, tile_end, _moe_up_compute,
                        f, BF16, tf, "moe_up")
    return moe_group_mm(hmid, [w_down], j, tile_start, tile_end, _moe_down_compute,
                        d, F32, tn, "moe_down")


def _moe_combine_kernel(pos_ref, ys_ref, h_ref, tw_ref, g_ref, ho_ref, hn_ref, buf_ref, sem):
    tm = MOE_COMBINE_TM
    base = pl.program_id(0) * tm * TOP_K

    def issue(i, carry):
        for slot in range(TOP_K):
            _row_copy(ys_ref, pos_ref[base + i * TOP_K + slot], buf_ref.at[slot], i, sem).start()
        return carry

    lax.fori_loop(0, tm, issue, 0, unroll=4)

    def drain(i, carry):
        _row_copy(ys_ref, 0, buf_ref.at[0], 0, sem).wait()
        return carry

    lax.fori_loop(0, tm * TOP_K, drain, 0)

    tw = tw_ref[...]
    x = h_ref[...] + tw[:, 0:1] * buf_ref[0] + tw[:, 1:2] * buf_ref[1]
    ho_ref[...] = x
    ms = jnp.mean(x * x, axis=-1, keepdims=True)
    hn_ref[...] = (x * lax.rsqrt(ms + NORM_EPS) * g_ref[...]).astype(hn_ref.dtype)


def moe_combine(ys, pos, h, tw, norm_g, layer):
    m, d = h.shape
    tm = MOE_COMBINE_TM
    norm_g = norm_g.reshape(norm_g.shape[0], 1, d)
    return pl.pallas_call(
        _moe_combine_kernel,
        grid_spec=pltpu.PrefetchScalarGridSpec(
            num_scalar_prefetch=1,
            grid=(m // tm,),
            in_specs=[
                pl.BlockSpec(memory_space=pl.ANY),
                pl.BlockSpec((tm, d), lambda i, pos: (i, 0)),
                pl.BlockSpec((tm, 128), lambda i, pos: (i, 0)),
                pl.BlockSpec((None, 1, d), lambda i, pos: (layer, 0, 0)),
            ],
            out_specs=[pl.BlockSpec((tm, d), lambda i, pos: (i, 0)),
                       pl.BlockSpec((tm, d), lambda i, pos: (i, 0))],
            scratch_shapes=[pltpu.VMEM((TOP_K, tm, d), F32), pltpu.SemaphoreType.DMA(())],
        ),
        out_shape=[jax.ShapeDtypeStruct((m, d), F32), jax.ShapeDtypeStruct((m, d), BF16)],
        compiler_params=_params(1),
        name="moe_combine",
    )(pos, ys, h, tw, norm_g)


def moe_layer(h, norm_g, layer, router_pad, w_gate, w_up, w_down, j, next_norm_g):
    m, d = h.shape
    tm = MOE_TM
    hn, ti, tw = route(h, norm_g, layer, router_pad, j)
    top_i = ti[:, :TOP_K].reshape(-1)
    n_pairs = m * TOP_K
    n_tiles = n_pairs // tm + N_EXPERTS
    onehot = (top_i[:, None] == jnp.arange(N_EXPERTS, dtype=jnp.int32)[None, :]).astype(jnp.int32)
    csum = jnp.cumsum(onehot, axis=0)
    rank = jnp.sum((csum - onehot) * onehot, axis=1)
    counts = csum[-1]
    tiles_per = (counts + tm - 1) // tm
    tile_end = jnp.cumsum(tiles_per).astype(jnp.int32)
    tile_start = (tile_end - tiles_per).astype(jnp.int32)
    pos = (jnp.sum(onehot * tile_start[None, :], axis=1) * tm + rank).astype(jnp.int32)

    xs = moe_gather(hn, pos, tile_start, tile_end, n_tiles * tm)
    ys = moe_experts(xs, tile_start, tile_end, w_gate, w_up, w_down, j)
    return moe_combine(ys, pos, h, tw, next_norm_g, layer)


def kernel(x, p, norm_mix, w_in, sgu_ln_g, sgu_ln_b, sgu_w, sgu_b, dn_conv_w, dn_a_log,
           dn_dt_bias, dn_norm_w, w_branch, w_out, norm_ffn, ffn_w_gate, ffn_w_up,
           ffn_w_down, moe_router, moe_w_gate, moe_w_up, moe_w_down, norm_ple,
           ple_w_gate, ple_b_gate, ple_w_proj, norm_final):
    bsz, seq, d = x.shape
    depth = p.shape[0]
    m = bsz * seq
    h = x.reshape(m, d)
    p2 = p.reshape(depth, m, p.shape[-1])
    main_w = 2 * SGU_WIDTH + 4 * DN_WIDTH
    n_small = 2 * DN_HEADS
    w_in_t = jnp.swapaxes(w_in, 1, 2)
    pad_h = 128 - n_small
    alog_row = jnp.pad(dn_a_log, ((0, 0), (DN_HEADS, pad_h))).reshape(depth, 1, 128)
    dtb_row = jnp.pad(dn_dt_bias, ((0, 0), (DN_HEADS, pad_h))).reshape(depth, 1, 128)
    ln_g = sgu_ln_g.reshape(depth, 1, SGU_WIDTH)
    ln_b = sgu_ln_b.reshape(depth, 1, SGU_WIDTH)
    sgu_b_t = jnp.swapaxes(sgu_b, 1, 2)
    nw = dn_norm_w.reshape(depth, 1, DN_HEAD_DIM)
    router_pad = jnp.pad(moe_router, ((0, 0), (0, 0), (0, 128 - N_EXPERTS)))
    ple_b = ple_b_gate.reshape(depth, 1, d)

    for i in range(depth):
        hn = rmsnorm(h, norm_mix, i, BF16)
        uv = fused_mm([(hn, 0)], [(w_in_t, (i,), 0, 0)], [], _ep_gelu, 2 * SGU_WIDTH, BF16,
                      1024, 512, "proj_uv", w_t=True)
        qkvz = fused_mm([(hn, 0)], [(w_in_t, (i,), (2 * SGU_WIDTH) // 512, 0)], [], _ep_plain,
                        4 * DN_WIDTH, BF16, 1024, 512, "proj_qkvz", w_t=True)
        gates = shifted_sigmoid_mm(hn, w_in_t, i, main_w, n_small, 2 * d, 1024, 512, "proj_gates")
        small = fused_mm([(hn, 0)], [(w_in_t, (i,), main_w // 128, 0)], [], _ep_plain, 128, F32,
                         1024, 128, "proj_small", w_t=True)
        y_a = sgu(uv, ln_g, ln_b, sgu_w, sgu_b_t, i)
        y_b = gdn(qkvz, small, dn_conv_w, alog_row, dtb_row, nw, i, bsz, seq, nb=2)
        merged = fused_mm([(y_a, 0), (y_b, 0)],
                          [(w_branch, (i, 0), 0, 0), (w_branch, (i, 1), 0, 1)],
                          [(gates, 0, "tile", 0), (gates, 0, "tile", d // 512)],
                          _ep_merge, d, BF16, 1024, 512, "merge")
        h = fused_mm([(merged, 0)], [(w_out, (i,), 0, 0)], [(h, 0, "tile", 0)],
                     _ep_residual, d, F32, 1024, 512, "out_proj")
        j = i // 2
        if i % 2 == 0:
            hn = rmsnorm(h, norm_ffn, i, BF16)
            hmid = fused_mm([(hn, 0)], [(ffn_w_gate, (j,), 0, 0), (ffn_w_up, (j,), 0, 0)], [],
                            _ep_swiglu, ffn_w_gate.shape[-1], BF16, 1024, 512, "ffn_up")
            h = fused_mm([(hmid, 0)], [(ffn_w_down, (j,), 0, 0)], [(h, 0, "tile", 0)],
                         _ep_residual, d, F32, 512, 512, "ffn_down")
            hn = rmsnorm(h, norm_ple, i, BF16)
        else:
            h, hn = moe_layer(h, norm_ffn, i, router_pad, moe_w_gate, moe_w_up, moe_w_down, j,
                              norm_ple)
        h = fused_mm([(hn, 0), (p2, i)],
                     [(ple_w_gate, (i,), 0, 0), (ple_w_proj, (i,), 0, 1)],
                     [(h, 0, "tile", 0), (ple_b, i, "row", 0)],
                     _ep_ple, d, F32, 1024, 512, "ple")
    out = rmsnorm(h, norm_final, 0, F32)
    return out.reshape(bsz, seq, d)
```

```python
import functools
import math

import jax
import jax.numpy as jnp
from jax import lax
from jax.experimental import pallas as pl
from jax.experimental.pallas import tpu as pltpu

F32 = jnp.float32
BF16 = jnp.bfloat16

D_MODEL = 2048
SGU_CHUNK = 128
SGU_GROUPS = 8
SGU_WIDTH = 1024
DN_HEADS = 8
DN_HEAD_DIM = 128
DN_WIDTH = 1024
DN_CONV = 4
DN_CHUNK = 64
N_EXPERTS = 8
TOP_K = 2
NORM_EPS = 1e-6

V7X_VMEM_LIMIT = 56 * 1024 * 1024
CAST_ROWS = 256
MOE_TM = 256
MOE_COMBINE_TM = 256


def _params(n_axes):
    return pltpu.CompilerParams(
        dimension_semantics=("arbitrary",) * n_axes,
        vmem_limit_bytes=V7X_VMEM_LIMIT)


def _sigmoid(x):
    return 1.0 / (1.0 + jnp.exp(-x))


def _silu(x):
    return x * _sigmoid(x)


def _gelu_tanh(x):
    c = math.sqrt(2.0 / math.pi)
    return 0.5 * x * (1.0 + jnp.tanh(c * (x + 0.044715 * (x * x * x))))


def _cast_rows(w_ref, wb_ref):
    k = w_ref.shape[0]
    ch = min(CAST_ROWS, k)

    def body(c, carry):
        r = pl.multiple_of(c * ch, ch)
        wb_ref[pl.ds(r, ch), :] = w_ref[pl.ds(r, ch), :].astype(BF16)
        return carry

    lax.fori_loop(0, k // ch, body, 0)


def _cast_rows_transposed(wt_ref, wb_ref):
    tn = wt_ref.shape[0]
    ch = min(CAST_ROWS, tn)
    for c in range(tn // ch):
        wb_ref[:, c * ch:(c + 1) * ch] = wt_ref[c * ch:(c + 1) * ch, :].T.astype(BF16)


def _rmsnorm_kernel(h_ref, g_ref, o_ref):
    x = h_ref[...]
    ms = jnp.mean(x * x, axis=-1, keepdims=True)
    o_ref[...] = (x * lax.rsqrt(ms + NORM_EPS) * g_ref[...]).astype(o_ref.dtype)


def rmsnorm(h, g, layer, out_dtype, tm=512):
    m, d = h.shape
    if g.ndim == 1:
        g = g.reshape(1, 1, d)
        layer = 0
    else:
        g = g.reshape(g.shape[0], 1, d)
    return pl.pallas_call(
        _rmsnorm_kernel,
        grid=(m // tm,),
        in_specs=[pl.BlockSpec((tm, d), lambda i: (i, 0)),
                  pl.BlockSpec((None, 1, d), lambda i: (layer, 0, 0))],
        out_specs=pl.BlockSpec((tm, d), lambda i: (i, 0)),
        out_shape=jax.ShapeDtypeStruct((m, d), out_dtype),
        compiler_params=_params(1),
        name="rmsnorm",
    )(h, g)


def _fused_mm_kernel(*refs, n_x, n_w, w_x, n_e, epilogue, w_t):
    x_refs = refs[:n_x]
    w_refs = refs[n_x:n_x + n_w]
    e_refs = refs[n_x + n_w:n_x + n_w + n_e]
    o_ref = refs[n_x + n_w + n_e]
    wb_refs = refs[n_x + n_w + n_e + 1:]

    @pl.when(pl.program_id(1) == 0)
    def _():
        for w_ref, wb_ref in zip(w_refs, wb_refs):
            (_cast_rows_transposed if w_t else _cast_rows)(w_ref, wb_ref)

    xs = [x_ref[...].astype(BF16) for x_ref in x_refs]
    accs = [jnp.dot(xs[w_x[l]], wb_refs[l][...], preferred_element_type=F32)
            for l in range(n_w)]
    o_ref[...] = epilogue(accs, [e[...] for e in e_refs], pl.program_id(0)).astype(o_ref.dtype)


def fused_mm(xs, ws, extras, epilogue, n_out, out_dtype, tm, tn, name, w_t=False):
    m = xs[0][0].shape[-2]
    grid = (n_out // tn, m // tm)
    in_specs, args = [], []
    for arr, lead in xs:
        k = arr.shape[-1]
        if arr.ndim == 2:
            in_specs.append(pl.BlockSpec((tm, k), lambda j, i: (i, 0)))
        else:
            in_specs.append(pl.BlockSpec((None, tm, k), lambda j, i, lead=lead: (lead, i, 0)))
        args.append(arr)
    scratch = []
    for arr, lead, off, _ in ws:
        k = arr.shape[-1] if w_t else arr.shape[-2]
        nlead = arr.ndim - 2
        lead = tuple(lead) if nlead else ()
        if w_t:
            in_specs.append(pl.BlockSpec(
                (None,) * nlead + (tn, k),
                lambda j, i, lead=lead, off=off: lead + (j + off, 0)))
        else:
            in_specs.append(pl.BlockSpec(
                (None,) * nlead + (k, tn),
                lambda j, i, lead=lead, off=off: lead + (0, j + off)))
        args.append(arr)
        scratch.append(pltpu.VMEM((k, tn), BF16))
    for arr, lead, kind, off in extras:
        if kind == "tile":
            if arr.ndim == 2:
                in_specs.append(pl.BlockSpec((tm, tn), lambda j, i, off=off: (i, j + off)))
            else:
                in_specs.append(pl.BlockSpec(
                    (None, tm, tn), lambda j, i, lead=lead, off=off: (lead, i, j + off)))
        else:
            in_specs.append(pl.BlockSpec(
                (None, 1, tn), lambda j, i, lead=lead, off=off: (lead, 0, j + off)))
        args.append(arr)
    kern = functools.partial(
        _fused_mm_kernel, n_x=len(xs), n_w=len(ws), w_x=tuple(w[3] for w in ws),
        n_e=len(extras), epilogue=epilogue, w_t=w_t)
    return pl.pallas_call(
        kern,
        grid=grid,
        in_specs=in_specs,
        out_specs=pl.BlockSpec((tm, tn), lambda j, i: (i, j)),
        out_shape=jax.ShapeDtypeStruct((m, n_out), out_dtype),
        scratch_shapes=scratch,
        compiler_params=_params(2),
        name=name,
    )(*args)


def _shifted_mm_kernel(x_ref, wa_ref, wn_ref, o_ref, wb_ref, *, shift):
    tn = wa_ref.shape[0]

    @pl.when(pl.program_id(1) == 0)
    def _():
        for c in range(tn // CAST_ROWS):
            lo = shift + c * CAST_ROWS
            hi = lo + CAST_ROWS
            if hi <= tn:
                rows = wa_ref[lo:hi, :]
            else:
                rows = jnp.concatenate([wa_ref[lo:tn, :], wn_ref[0:hi - tn, :]], axis=0)
            wb_ref[:, c * CAST_ROWS:(c + 1) * CAST_ROWS] = rows.T.astype(BF16)

    acc = jnp.dot(x_ref[...], wb_ref[...], preferred_element_type=F32)
    o_ref[...] = _sigmoid(acc).astype(o_ref.dtype)


def shifted_sigmoid_mm(x, wt, layer, row0, shift, n_out, tm, tn, name):
    m, k = x.shape
    base = row0 // tn
    return pl.pallas_call(
        functools.partial(_shifted_mm_kernel, shift=shift),
        grid=(n_out // tn, m // tm),
        in_specs=[
            pl.BlockSpec((tm, k), lambda j, i: (i, 0)),
            pl.BlockSpec((None, tn, k), lambda j, i: (layer, base + j, 0)),
            pl.BlockSpec((None, shift, k), lambda j, i: (layer, (base + j + 1) * (tn // shift), 0)),
        ],
        out_specs=pl.BlockSpec((tm, tn), lambda j, i: (i, j)),
        out_shape=jax.ShapeDtypeStruct((m, n_out), BF16),
        scratch_shapes=[pltpu.VMEM((k, tn), BF16)],
        compiler_params=_params(2),
        name=name,
    )(x, wt, wt)


def _ep_gelu(accs, extras, j):
    return _gelu_tanh(accs[0])


def _ep_sigmoid(accs, extras, j):
    return _sigmoid(accs[0])


def _ep_plain(accs, extras, j):
    return accs[0]


def _ep_merge(accs, extras, j):
    return extras[0].astype(F32) * accs[0] + extras[1].astype(F32) * accs[1]


def _ep_residual(accs, extras, j):
    return extras[0] + accs[0]


def _ep_swiglu(accs, extras, j):
    return _silu(accs[0]) * accs[1]


def _ep_ple(accs, extras, j):
    return extras[0] + _sigmoid(accs[0] + extras[1]) * accs[1]


def _sgu_kernel(u_ref, v_ref, lng_ref, lnb_ref, w_ref, bt_ref, o_ref, wm_ref):
    c = SGU_CHUNK

    @pl.when(pl.program_id(0) == 0)
    def _():
        ti = lax.broadcasted_iota(jnp.int32, (c, c), 0)
        si = lax.broadcasted_iota(jnp.int32, (c, c), 1)
        for g in range(SGU_GROUPS):
            wm_ref[g] = jnp.where(si <= ti, w_ref[g], 0.0).astype(BF16)

    v = v_ref[...].astype(F32)
    mu = jnp.mean(v, axis=-1, keepdims=True)
    vc = v - mu
    var = jnp.mean(vc * vc, axis=-1, keepdims=True)
    vln = (vc * lax.rsqrt(var + NORM_EPS) * lng_ref[...] + lnb_ref[...]).astype(BF16)
    bt = bt_ref[...]
    for g in range(SGU_GROUPS):
        sl = slice(g * c, (g + 1) * c)
        mixed = jnp.dot(wm_ref[g], vln[:, sl], preferred_element_type=F32) + bt[:, g:g + 1]
        o_ref[:, sl] = (u_ref[:, sl].astype(F32) * mixed).astype(o_ref.dtype)


def sgu(proj, ln_g, ln_b, w_s, b_s_t, layer):
    m = proj.shape[0]
    c = SGU_CHUNK
    return pl.pallas_call(
        _sgu_kernel,
        grid=(m // c,),
        in_specs=[
            pl.BlockSpec((c, SGU_WIDTH), lambda i: (i, 0)),
            pl.BlockSpec((c, SGU_WIDTH), lambda i: (i, 1)),
            pl.BlockSpec((None, 1, SGU_WIDTH), lambda i: (layer, 0, 0)),
            pl.BlockSpec((None, 1, SGU_WIDTH), lambda i: (layer, 0, 0)),
            pl.BlockSpec((None, SGU_GROUPS, c, c), lambda i: (layer, 0, 0, 0)),
            pl.BlockSpec((None, c, SGU_GROUPS), lambda i: (layer, 0, 0)),
        ],
        out_specs=pl.BlockSpec((c, SGU_WIDTH), lambda i: (i, 0)),
        out_shape=jax.ShapeDtypeStruct((m, SGU_WIDTH), BF16),
        scratch_shapes=[pltpu.VMEM((SGU_GROUPS, c, c), BF16)],
        compiler_params=_params(1),
        name="sgu",
    )(proj, proj, ln_g, ln_b, w_s, b_s_t)


def _bdot(a, b):
    return jnp.dot(a.astype(BF16), b.astype(BF16), preferred_element_type=F32)


def _bdot_nt(a, b):
    return lax.dot_general(a.astype(BF16), b.astype(BF16), (((1,), (1,)), ((), ())),
                           preferred_element_type=F32)


def _gdn_kernel(q_ref, k_ref, v_ref, z_ref, sm_ref, cw_ref, alog_ref, dtb_ref, nw_ref,
                o_ref, s_ref, prev_ref, *, nb):
    c = DN_CHUNK
    hd = DN_HEAD_DIM
    w = DN_WIDTH
    pw = 2 * hd
    n_pairs = DN_HEADS // 2

    @pl.when(pl.program_id(1) == 0)
    def _():
        s_ref[...] = jnp.zeros(s_ref.shape, F32)
        prev_ref[...] = jnp.zeros(prev_ref.shape, F32)

    row_w = lax.broadcasted_iota(jnp.int32, (c, w), 0)
    row_s = lax.broadcasted_iota(jnp.int32, (c, 128), 0)
    lane = lax.broadcasted_iota(jnp.int32, (c, 128), 1)
    left = lane < c
    jmod = jnp.where(left, lane, lane - c)
    tri = row_s >= jmod
    strict = row_s > jmod
    eye = (row_s == jmod).astype(F32)
    r256 = lax.broadcasted_iota(jnp.int32, (pw, pw), 0)
    c256 = lax.broadcasted_iota(jnp.int32, (pw, pw), 1)
    bd_mask = (r256 < hd) == (c256 < hd)
    zeros_h = jnp.zeros((c, hd), F32)
    nw = nw_ref[...]
    scale = hd ** -0.5

    def bcast2(col_a, col_b):
        return jnp.concatenate([jnp.broadcast_to(col_a, (c, hd)),
                                jnp.broadcast_to(col_b, (c, hd))], axis=1)

    def blockdiag_rows(x):
        return jnp.concatenate([jnp.where(left, x, 0.0), jnp.where(left, 0.0, x)], axis=0)

    def blockdiag_heads(x):
        return jnp.concatenate(
            [jnp.concatenate([x[:, :hd], zeros_h], axis=1),
             jnp.concatenate([zeros_h, x[:, hd:]], axis=1)], axis=0)

    def inv_norm(x):
        ss_a = jnp.sum(x[:, :hd] * x[:, :hd], axis=-1, keepdims=True)
        ss_b = jnp.sum(x[:, hd:] * x[:, hd:], axis=-1, keepdims=True)
        return bcast2(lax.rsqrt(ss_a + NORM_EPS), lax.rsqrt(ss_b + NORM_EPS))

    chains = []
    for b in range(nb):
        def conv_silu(x_ref, sec, b=b):
            x = x_ref[b].astype(F32)
            prev = prev_ref[b * 3 + sec]
            cw = cw_ref[:, sec * w:(sec + 1) * w]
            acc = cw[DN_CONV - 1:DN_CONV, :] * x
            for s in range(1, DN_CONV):
                shifted = jnp.where(row_w < s, pltpu.roll(prev, s, 0), pltpu.roll(x, s, 0))
                acc = acc + cw[DN_CONV - 1 - s:DN_CONV - s, :] * shifted
            prev_ref[b * 3 + sec] = x
            return _silu(acc)

        q_all = conv_silu(q_ref, 0)
        k_all = conv_silu(k_ref, 1)
        v_all = conv_silu(v_ref, 2)

        sm = sm_ref[b]
        beta_all = _sigmoid(sm)
        xa = sm + dtb_ref[...]
        softplus = jnp.maximum(xa, 0.0) + jnp.log(1.0 + jnp.exp(-jnp.abs(xa)))
        g = -jnp.exp(alog_ref[...]) * softplus
        s = 1
        while s < c:
            g = g + jnp.where(row_s >= s, pltpu.roll(g, s, 0), 0.0)
            s *= 2
        g_t = jnp.concatenate([g, pltpu.roll(g, 127, 1)], axis=0).T

        for p in range(n_pairs):
            a = 2 * p
            sl = slice(p * pw, (p + 1) * pw)
            q = q_all[:, sl]
            k = k_all[:, sl]
            v = v_all[:, sl]
            q = q * (inv_norm(q) * scale)
            k = k * inv_norm(k)
            gc_a = g[:, DN_HEADS + a:DN_HEADS + a + 1]
            gc_b = g[:, DN_HEADS + a + 1:DN_HEADS + a + 2]
            gl_a = gc_a[c - 1:c, :]
            gl_b = gc_b[c - 1:c, :]
            grow = g_t[DN_HEADS + a:DN_HEADS + a + 1, :]
            gcol = jnp.where(left, gc_a, gc_b)
            decay = jnp.exp(jnp.where(tri, gcol - grow, -1e30))
            beta = bcast2(beta_all[:, a:a + 1], beta_all[:, a + 1:a + 2])
            egc = bcast2(jnp.exp(gc_a), jnp.exp(gc_b))
            kb = k * beta
            chains.append(dict(
                b=b, p=p, q=q, k=k, kb=kb, vb=v * beta, kbg=kb * egc, qg=q * egc,
                kd=k * bcast2(jnp.exp(gl_a - gc_a), jnp.exp(gl_b - gc_b)),
                dl=jnp.concatenate([jnp.broadcast_to(jnp.exp(gl_a), (1, hd)),
                                    jnp.broadcast_to(jnp.exp(gl_b), (1, hd))], axis=1),
                decay=decay))

    for ch in chains:
        kq = _bdot_nt(jnp.concatenate([ch["kb"], ch["q"]], axis=0), blockdiag_heads(ch["k"]))
        ch["bk"] = -jnp.where(strict, kq[:c] * ch["decay"], 0.0)
        ch["attn"] = jnp.where(tri, kq[c:] * ch["decay"], 0.0)
        ch["qm"] = eye + ch["bk"]
    for ch in chains:
        ch["bk"] = _bdot(ch["bk"], blockdiag_rows(ch["bk"]))
    for lvl in range(1, 6):
        for ch in chains:
            if lvl < 5:
                r = _bdot(jnp.concatenate([ch["qm"], ch["bk"]], axis=0), blockdiag_rows(ch["bk"]))
                ch["qm"] = ch["qm"] + r[:c]
                ch["bk"] = r[c:]
            else:
                ch["qm"] = ch["qm"] + _bdot(ch["qm"], blockdiag_rows(ch["bk"]))
    for ch in chains:
        vb, kbg = ch["vb"], ch["kbg"]
        rhs = jnp.concatenate(
            [jnp.concatenate([vb[:, :hd], kbg[:, :hd], zeros_h, zeros_h], axis=1),
             jnp.concatenate([zeros_h, zeros_h, vb[:, hd:], kbg[:, hd:]], axis=1)], axis=0)
        sol = _bdot(ch["qm"], rhs)
        ch["u"] = jnp.concatenate([sol[:, :hd], sol[:, 2 * hd:3 * hd]], axis=1)
        ch["w"] = jnp.concatenate([sol[:, hd:2 * hd], sol[:, 3 * hd:]], axis=1)
    for ch in chains:
        ch["state"] = s_ref[ch["b"] * n_pairs + ch["p"]]
        r2 = _bdot(jnp.concatenate([ch["w"], ch["qg"]], axis=0), ch["state"])
        ch["vn"] = ch["u"] - r2[:c]
        ch["o"] = r2[c:]
    for ch in chains:
        ch["o"] = ch["o"] + _bdot(ch["attn"], blockdiag_heads(ch["vn"]))
    for ch in chains:
        upd = ch["state"] * ch["dl"] + _bdot(ch["kd"].T, ch["vn"])
        s_ref[ch["b"] * n_pairs + ch["p"]] = jnp.where(bd_mask, upd, 0.0)
    for ch in chains:
        b, p = ch["b"], ch["p"]
        for half in range(2):
            o = ch["o"][:, half * hd:(half + 1) * hd]
            sl = slice(p * pw + half * hd, p * pw + (half + 1) * hd)
            o = o * lax.rsqrt(jnp.mean(o * o, axis=-1, keepdims=True) + NORM_EPS)
            o = o * nw * _silu(z_ref[b, :, sl].astype(F32))
            o_ref[b, :, sl] = o.astype(o_ref.dtype)


def gdn(proj, small, conv_w, alog_row, dtb_row, norm_w, layer, bsz, seq, nb=1):
    m = proj.shape[0]
    c = DN_CHUNK
    n = seq // c
    w = DN_WIDTH
    qkv0 = 0
    proj3 = proj.reshape(bsz, seq, proj.shape[-1])
    small3 = small.reshape(bsz, seq, small.shape[-1])
    pw = 2 * DN_HEAD_DIM
    out = pl.pallas_call(
        functools.partial(_gdn_kernel, nb=nb),
        grid=(bsz // nb, n),
        in_specs=[
            pl.BlockSpec((nb, c, w), lambda b, t: (b, t, qkv0)),
            pl.BlockSpec((nb, c, w), lambda b, t: (b, t, qkv0 + 1)),
            pl.BlockSpec((nb, c, w), lambda b, t: (b, t, qkv0 + 2)),
            pl.BlockSpec((nb, c, w), lambda b, t: (b, t, qkv0 + 3)),
            pl.BlockSpec((nb, c, 128), lambda b, t: (b, t, 0)),
            pl.BlockSpec((None, DN_CONV, 3 * w), lambda b, t: (layer, 0, 0)),
            pl.BlockSpec((None, 1, 128), lambda b, t: (layer, 0, 0)),
            pl.BlockSpec((None, 1, 128), lambda b, t: (layer, 0, 0)),
            pl.BlockSpec((None, 1, DN_HEAD_DIM), lambda b, t: (layer, 0, 0)),
        ],
        out_specs=pl.BlockSpec((nb, c, w), lambda b, t: (b, t, 0)),
        out_shape=jax.ShapeDtypeStruct((bsz, seq, w), BF16),
        scratch_shapes=[pltpu.VMEM((nb * (DN_HEADS // 2), pw, pw), F32),
                        pltpu.VMEM((nb * 3, c, w), F32)],
        compiler_params=_params(2),
        name="gdn",
    )(proj3, proj3, proj3, proj3, small3, conv_w, alog_row, dtb_row, norm_w)
    return out.reshape(m, w)


def _route_kernel(h_ref, g_ref, r_ref, hn_ref, ti_ref, tw_ref):
    x = h_ref[...]
    ms = jnp.mean(x * x, axis=-1, keepdims=True)
    hn = x * lax.rsqrt(ms + NORM_EPS) * g_ref[...]
    hn_ref[...] = hn
    logits = jnp.dot(hn, r_ref[...], preferred_element_type=F32,
                     precision=lax.Precision.HIGHEST)
    lane = lax.broadcasted_iota(jnp.int32, logits.shape, 1)
    neg = jnp.float32(-3.0e38)
    logits = jnp.where(lane < N_EXPERTS, logits, neg)
    m1 = jnp.max(logits, axis=-1, keepdims=True)
    i1 = jnp.min(jnp.where(logits == m1, lane, 128), axis=-1, keepdims=True)
    rest = jnp.where(lane == i1, neg, logits)
    m2 = jnp.max(rest, axis=-1, keepdims=True)
    i2 = jnp.min(jnp.where(rest == m2, lane, 128), axis=-1, keepdims=True)
    e2 = jnp.exp(m2 - m1)
    w1 = 1.0 / (1.0 + e2)
    w2 = e2 / (1.0 + e2)
    ti_ref[...] = jnp.where(lane == 0, i1, jnp.where(lane == 1, i2, 0))
    tw_ref[...] = jnp.where(lane == 0, w1, jnp.where(lane == 1, w2, 0.0))


def route(h, g, layer, router_pad, j, tm=256):
    m, d = h.shape
    g = g.reshape(g.shape[0], 1, d)
    return pl.pallas_call(
        _route_kernel,
        grid=(m // tm,),
        in_specs=[pl.BlockSpec((tm, d), lambda i: (i, 0)),
                  pl.BlockSpec((None, 1, d), lambda i: (layer, 0, 0)),
                  pl.BlockSpec((None, d, 128), lambda i: (j, 0, 0))],
        out_specs=[pl.BlockSpec((tm, d), lambda i: (i, 0)),
                   pl.BlockSpec((tm, 128), lambda i: (i, 0)),
                   pl.BlockSpec((tm, 128), lambda i: (i, 0))],
        out_shape=[jax.ShapeDtypeStruct((m, d), F32),
                   jax.ShapeDtypeStruct((m, 128), jnp.int32),
                   jax.ShapeDtypeStruct((m, 128), F32)],
        compiler_params=_params(1),
        name="route",
    )(h, g, router_pad)


def _row_copy(src_ref, src_row, dst_ref, dst_row, sem):
    return pltpu.make_async_copy(src_ref.at[pl.ds(src_row, 1)], dst_ref.at[pl.ds(dst_row, 1)], sem)


def _moe_gather_kernel(pos_ref, tstart_ref, tend_ref, hn_ref, xs_ref, fill_sem, row_sem):
    step = pl.program_id(0)
    tm = MOE_TM

    n_tiles = xs_ref.shape[0] // tm
    n_active = tend_ref[N_EXPERTS - 1]

    def fill_tile(t):
        dst = pl.multiple_of(t * tm, tm)
        return pltpu.make_async_copy(hn_ref, xs_ref.at[pl.ds(dst, tm)], fill_sem)

    def fills(act):
        for e in range(N_EXPERTS):
            @pl.when(tend_ref[e] > tstart_ref[e])
            def _():
                act(fill_tile(tend_ref[e] - 1))
        for t in range(N_EXPERTS):
            @pl.when(n_active + t < n_tiles)
            def _():
                act(fill_tile(n_active + t))

    @pl.when(step == 0)
    def _():
        fills(lambda cp: cp.start())
        fills(lambda cp: cp.wait())

    base = step * tm * TOP_K

    def issue(t, carry):
        for slot in range(TOP_K):
            _row_copy(hn_ref, t, xs_ref, pos_ref[base + t * TOP_K + slot], row_sem).start()
        return carry

    lax.fori_loop(0, tm, issue, 0, unroll=4)

    def drain(i, carry):
        _row_copy(hn_ref, 0, xs_ref, 0, row_sem).wait()
        return carry

    lax.fori_loop(0, tm * TOP_K, drain, 0)


def moe_gather(hn, pos, tile_start, tile_end, rows):
    m, d = hn.shape
    tm = MOE_TM
    return pl.pallas_call(
        _moe_gather_kernel,
        grid_spec=pltpu.PrefetchScalarGridSpec(
            num_scalar_prefetch=3,
            grid=(m // tm,),
            in_specs=[pl.BlockSpec((tm, d), lambda i, pos, ts, te: (i, 0))],
            out_specs=pl.BlockSpec(memory_space=pl.ANY),
            scratch_shapes=[pltpu.SemaphoreType.DMA(()), pltpu.SemaphoreType.DMA(())],
        ),
        out_shape=jax.ShapeDtypeStruct((rows, d), F32),
        compiler_params=_params(1),
        name="moe_gather",
    )(pos, tile_start, tile_end, hn)


def _moe_group_kernel(ts_ref, te_ref, x_ref, *refs, n_w, compute, mults, k_split):
    w_refs = refs[:n_w]
    o_ref = refs[n_w]
    wb_refs = refs[n_w + 1:2 * n_w + 1]
    xbuf, obuf, xsem, osem = refs[2 * n_w + 1:]
    tm = MOE_TM
    tn = obuf.shape[2]
    n_tiles = o_ref.shape[0] // tm
    e = pl.program_id(1)
    kh = pl.program_id(2)
    col = pl.multiple_of(pl.program_id(0) * tn, tn)
    t0 = ts_ref[e]
    n = te_ref[e] - t0
    kblk = w_refs[0].shape[0]

    @pl.when(n > 0)
    def _():
        for w_ref, wb_ref in zip(w_refs, wb_refs):
            if k_split == 1:
                _cast_rows(w_ref, wb_ref)
            else:
                _cast_rows(w_ref, wb_ref.at[pl.ds(pl.multiple_of(kh * kblk, kblk), kblk)])

    big = mults[0]
    n_big = n // big
    rem = n - n_big * big

    def step_info(s):
        mult = jnp.where(s < n_big, big, 0)
        off = s * big
        idx = n_big
        first = n_big * big
        for mlt in mults[1:]:
            present = ((rem // mlt) % 2).astype(jnp.int32)
            here = jnp.logical_and(present == 1, s == idx)
            mult = jnp.where(here, mlt, mult)
            off = jnp.where(here, first, off)
            idx = idx + present
            first = first + present * mlt
        return mult, off

    n_steps = n_big
    for mlt in mults[1:]:
        n_steps = n_steps + (rem // mlt) % 2

    def for_mult(mult, fn):
        for mlt in mults:
            @pl.when(mult == mlt)
            def _():
                fn(mlt)

    def x_copy(off, mlt, slot):
        row = pl.multiple_of((t0 + off) * tm, tm)
        return pltpu.make_async_copy(x_ref.at[pl.ds(row, mlt * tm)],
                                     xbuf.at[slot, pl.ds(0, mlt * tm)], xsem.at[slot])

    def o_copy(tile, mlt, slot):
        row = pl.multiple_of(tile * tm, tm)
        return pltpu.make_async_copy(
            obuf.at[slot, pl.ds(0, mlt * tm)],
            o_ref.at[pl.ds(row, mlt * tm), pl.ds(col, tn)], osem.at[slot])

    @pl.when(jnp.logical_and(n > 0, kh == k_split - 1))
    def _():
        m0, off0 = step_info(0)
        for_mult(m0, lambda mlt: x_copy(off0, mlt, 0).start())

        def body(s, carry):
            slot = lax.rem(s, 2)
            ms, offs = step_info(s)
            for_mult(ms, lambda mlt: x_copy(offs, mlt, slot).wait())

            @pl.when(s + 1 < n_steps)
            def _():
                mn, offn = step_info(s + 1)
                for_mult(mn, lambda mlt: x_copy(offn, mlt, 1 - slot).start())

            @pl.when(s >= 2)
            def _():
                mp, offp = step_info(s - 2)
                for_mult(mp, lambda mlt: o_copy(t0 + offp, mlt, slot).wait())

            def run(mlt):
                rows = mlt * tm
                obuf[slot, :rows] = compute(
                    xbuf[slot, :rows], [wb[...] for wb in wb_refs]).astype(obuf.dtype)
                o_copy(t0 + offs, mlt, slot).start()

            for_mult(ms, run)
            return carry

        lax.fori_loop(0, n_steps, body, 0)

        @pl.when(n_steps >= 2)
        def _():
            mp, offp = step_info(n_steps - 2)
            for_mult(mp, lambda mlt: o_copy(t0 + offp, mlt, lax.rem(n_steps, 2)).wait())

        ml, offl = step_info(n_steps - 1)
        for_mult(ml, lambda mlt: o_copy(t0 + offl, mlt, lax.rem(n_steps - 1, 2)).wait())

    @pl.when(jnp.logical_and(e == N_EXPERTS - 1, kh == k_split - 1))
    def _():
        n_active = te_ref[N_EXPERTS - 1]
        obuf[0, :tm] = jnp.zeros((tm, tn), obuf.dtype)
        for k in range(N_EXPERTS):
            @pl.when(n_active + k < n_tiles)
            def _():
                o_copy(n_active + k, 1, 0).start()
        for k in range(N_EXPERTS):
            @pl.when(n_active + k < n_tiles)
            def _():
                o_copy(n_active + k, 1, 0).wait()


def _moe_up_compute(x, wbs):
    xb = x.astype(BF16)
    a = jnp.dot(xb, wbs[0], preferred_element_type=F32)
    b = jnp.dot(xb, wbs[1], preferred_element_type=F32)
    return _silu(a) * b


def _moe_down_compute(x, wbs):
    return jnp.dot(x, wbs[0], preferred_element_type=F32)


def moe_group_mm(x, ws, j, tile_start, tile_end, compute, n_out, out_dtype, tn, mults, k_split,
                 name):
    rows, k = x.shape
    tm = MOE_TM
    w_specs = [pl.BlockSpec((None, None, k // k_split, tn),
                            lambda c, e, kh, ts, te: (j, e, kh, c)) for _ in ws]
    return pl.pallas_call(
        functools.partial(_moe_group_kernel, n_w=len(ws), compute=compute, mults=mults,
                          k_split=k_split),
        grid_spec=pltpu.PrefetchScalarGridSpec(
            num_scalar_prefetch=2,
            grid=(n_out // tn, N_EXPERTS, k_split),
            in_specs=[pl.BlockSpec(memory_space=pl.ANY)] + w_specs,
            out_specs=pl.BlockSpec(memory_space=pl.ANY),
            scratch_shapes=[pltpu.VMEM((k, tn), BF16) for _ in ws] + [
                pltpu.VMEM((2, mults[0] * tm, k), x.dtype),
                pltpu.VMEM((2, mults[0] * tm, tn), out_dtype),
                pltpu.SemaphoreType.DMA((2,)),
                pltpu.SemaphoreType.DMA((2,)),
            ],
        ),
        out_shape=jax.ShapeDtypeStruct((rows, n_out), out_dtype),
        compiler_params=_params(3),
        name=name,
    )(tile_start, tile_end, x, *ws)


def moe_experts(xs, tile_start, tile_end, w_gate, w_up, w_down, j):
    d = xs.shape[1]
    f = w_gate.shape[-1]
    hmid = moe_group_mm(xs, [w_gate, w_up], j, tile_start, tile_end, _moe_up_compute,
                        f, BF16, 512, (4, 2, 1), 1, "moe_up")
    return moe_group_mm(hmid, [w_down], j, tile_start, tile_end, _moe_down_compute,
                        d, F32, 512, (2, 1), 2, "moe_down")


def _moe_combine_kernel(pos_ref, ys_ref, h_ref, tw_ref, g_ref, ho_ref, hn_ref, buf_ref, sem):
    tm = MOE_COMBINE_TM
    base = pl.program_id(0) * tm * TOP_K

    def issue(i, carry):
        for slot in range(TOP_K):
            _row_copy(ys_ref, pos_ref[base + i * TOP_K + slot], buf_ref.at[slot], i, sem).start()
        return carry

    lax.fori_loop(0, tm, issue, 0, unroll=4)

    def drain(i, carry):
        _row_copy(ys_ref, 0, buf_ref.at[0], 0, sem).wait()
        return carry

    lax.fori_loop(0, tm * TOP_K, drain, 0)

    tw = tw_ref[...]
    x = h_ref[...] + tw[:, 0:1] * buf_ref[0] + tw[:, 1:2] * buf_ref[1]
    ho_ref[...] = x
    ms = jnp.mean(x * x, axis=-1, keepdims=True)
    hn_ref[...] = (x * lax.rsqrt(ms + NORM_EPS) * g_ref[...]).astype(hn_ref.dtype)


def moe_combine(ys, pos, h, tw, norm_g, layer):
    m, d = h.shape
    tm = MOE_COMBINE_TM
    norm_g = norm_g.reshape(norm_g.shape[0], 1, d)
    return pl.pallas_call(
        _moe_combine_kernel,
        grid_spec=pltpu.PrefetchScalarGridSpec(
            num_scalar_prefetch=1,
            grid=(m // tm,),
            in_specs=[
                pl.BlockSpec(memory_space=pl.ANY),
                pl.BlockSpec((tm, d), lambda i, pos: (i, 0)),
                pl.BlockSpec((tm, 128), lambda i, pos: (i, 0)),
                pl.BlockSpec((None, 1, d), lambda i, pos: (layer, 0, 0)),
            ],
            out_specs=[pl.BlockSpec((tm, d), lambda i, pos: (i, 0)),
                       pl.BlockSpec((tm, d), lambda i, pos: (i, 0))],
            scratch_shapes=[pltpu.VMEM((TOP_K, tm, d), F32), pltpu.SemaphoreType.DMA(())],
        ),
        out_shape=[jax.ShapeDtypeStruct((m, d), F32), jax.ShapeDtypeStruct((m, d), BF16)],
        compiler_params=_params(1),
        name="moe_combine",
    )(pos, ys, h, tw, norm_g)


def moe_layer(h, norm_g, layer, router_pad, w_gate, w_up, w_down, j, next_norm_g):
    m, d = h.shape
    tm = MOE_TM
    hn, ti, tw = route(h, norm_g, layer, router_pad, j)
    top_i = ti[:, :TOP_K].reshape(-1)
    n_pairs = m * TOP_K
    n_tiles = n_pairs // tm + N_EXPERTS
    onehot = (top_i[:, None] == jnp.arange(N_EXPERTS, dtype=jnp.int32)[None, :]).astype(jnp.int32)
    csum = jnp.cumsum(onehot, axis=0)
    rank = jnp.sum((csum - onehot) * onehot, axis=1)
    counts = csum[-1]
    tiles_per = (counts + tm - 1) // tm
    tile_end = jnp.cumsum(tiles_per).astype(jnp.int32)
    tile_start = (tile_end - tiles_per).astype(jnp.int32)
    pos = (jnp.sum(onehot * tile_start[None, :], axis=1) * tm + rank).astype(jnp.int32)

    xs = moe_gather(hn, pos, tile_start, tile_end, n_tiles * tm)
    ys = moe_experts(xs, tile_start, tile_end, w_gate, w_up, w_down, j)
    return moe_combine(ys, pos, h, tw, next_norm_g, layer)


def kernel(x, p, norm_mix, w_in, sgu_ln_g, sgu_ln_b, sgu_w, sgu_b, dn_conv_w, dn_a_log,
           dn_dt_bias, dn_norm_w, w_branch, w_out, norm_ffn, ffn_w_gate, ffn_w_up,
           ffn_w_down, moe_router, moe_w_gate, moe_w_up, moe_w_down, norm_ple,
           ple_w_gate, ple_b_gate, ple_w_proj, norm_final):
    bsz, seq, d = x.shape
    depth = p.shape[0]
    m = bsz * seq
    h = x.reshape(m, d)
    p2 = p.reshape(depth, m, p.shape[-1])
    main_w = 2 * SGU_WIDTH + 4 * DN_WIDTH
    n_small = 2 * DN_HEADS
    w_in_t = jnp.swapaxes(w_in, 1, 2)
    pad_h = 128 - n_small
    alog_row = jnp.pad(dn_a_log, ((0, 0), (DN_HEADS, pad_h))).reshape(depth, 1, 128)
    dtb_row = jnp.pad(dn_dt_bias, ((0, 0), (DN_HEADS, pad_h))).reshape(depth, 1, 128)
    ln_g = sgu_ln_g.reshape(depth, 1, SGU_WIDTH)
    ln_b = sgu_ln_b.reshape(depth, 1, SGU_WIDTH)
    sgu_b_t = jnp.swapaxes(sgu_b, 1, 2)
    nw = dn_norm_w.reshape(depth, 1, DN_HEAD_DIM)
    router_pad = jnp.pad(moe_router, ((0, 0), (0, 0), (0, 128 - N_EXPERTS)))
    ple_b = ple_b_gate.reshape(depth, 1, d)

    for i in range(depth):
        hn = rmsnorm(h, norm_mix, i, BF16)
        uv = fused_mm([(hn, 0)], [(w_in_t, (i,), 0, 0)], [], _ep_gelu, 2 * SGU_WIDTH, BF16,
                      1024, 512, "proj_uv", w_t=True)
        qkvz = fused_mm([(hn, 0)], [(w_in_t, (i,), (2 * SGU_WIDTH) // 512, 0)], [], _ep_plain,
                        4 * DN_WIDTH, BF16, 1024, 512, "proj_qkvz", w_t=True)
        gates = shifted_sigmoid_mm(hn, w_in_t, i, main_w, n_small, 2 * d, 1024, 512, "proj_gates")
        small = fused_mm([(hn, 0)], [(w_in_t, (i,), main_w // 128, 0)], [], _ep_plain, 128, F32,
                         1024, 128, "proj_small", w_t=True)
        y_a = sgu(uv, ln_g, ln_b, sgu_w, sgu_b_t, i)
        y_b = gdn(qkvz, small, dn_conv_w, alog_row, dtb_row, nw, i, bsz, seq, nb=2)
        merged = fused_mm([(y_a, 0), (y_b, 0)],
                          [(w_branch, (i, 0), 0, 0), (w_branch, (i, 1), 0, 1)],
                          [(gates, 0, "tile", 0), (gates, 0, "tile", d // 512)],
                          _ep_merge, d, BF16, 1024, 512, "merge")
        h = fused_mm([(merged, 0)], [(w_out, (i,), 0, 0)], [(h, 0, "tile", 0)],
                     _ep_residual, d, F32, 1024, 512, "out_proj")
        j = i // 2
        if i % 2 == 0:
            hn = rmsnorm(h, norm_ffn, i, BF16)
            hmid = fused_mm([(hn, 0)], [(ffn_w_gate, (j,), 0, 0), (ffn_w_up, (j,), 0, 0)], [],
                            _ep_swiglu, ffn_w_gate.shape[-1], BF16, 1024, 512, "ffn_up")
            h = fused_mm([(hmid, 0)], [(ffn_w_down, (j,), 0, 0)], [(h, 0, "tile", 0)],
                         _ep_residual, d, F32, 512, 512, "ffn_down")
            hn = rmsnorm(h, norm_ple, i, BF16)
        else:
            h, hn = moe_layer(h, norm_ffn, i, router_pad, moe_w_gate, moe_w_up, moe_w_down, j,
                              norm_ple)
        h = fused_mm([(hn, 0), (p2, i)],
                     [(ple_w_gate, (i,), 0, 0), (ple_w_proj, (i,), 0, 1)],
                     [(h, 0, "tile", 0), (ple_b, i, "row", 0)],
                     _ep_ple, d, F32, 1024, 512, "ple")
    out = rmsnorm(h, norm_final, 0, F32)
    return out.reshape(bsz, seq, d)
```

```python
import functools
import math

import jax
import jax.numpy as jnp
from jax import lax
from jax.experimental import pallas as pl
from jax.experimental.pallas import tpu as pltpu

F32 = jnp.float32
BF16 = jnp.bfloat16

D_MODEL = 2048
SGU_CHUNK = 128
SGU_GROUPS = 8
SGU_WIDTH = 1024
DN_HEADS = 8
DN_HEAD_DIM = 128
DN_WIDTH = 1024
DN_CONV = 4
DN_CHUNK = 64
N_EXPERTS = 8
TOP_K = 2
NORM_EPS = 1e-6

V7X_VMEM_LIMIT = 56 * 1024 * 1024
CAST_ROWS = 256
MOE_TM = 256
MOE_COMBINE_TM = 256


def _params(n_axes):
    return pltpu.CompilerParams(
        dimension_semantics=("arbitrary",) * n_axes,
        vmem_limit_bytes=V7X_VMEM_LIMIT)


def _sigmoid(x):
    return 1.0 / (1.0 + jnp.exp(-x))


def _silu(x):
    return x * _sigmoid(x)


def _gelu_tanh(x):
    c = math.sqrt(2.0 / math.pi)
    return 0.5 * x * (1.0 + jnp.tanh(c * (x + 0.044715 * (x * x * x))))


def _cast_rows(w_ref, wb_ref):
    k = w_ref.shape[0]
    ch = min(CAST_ROWS, k)

    def body(c, carry):
        r = pl.multiple_of(c * ch, ch)
        wb_ref[pl.ds(r, ch), :] = w_ref[pl.ds(r, ch), :].astype(BF16)
        return carry

    lax.fori_loop(0, k // ch, body, 0)


def _cast_rows_transposed(wt_ref, wb_ref):
    tn = wt_ref.shape[0]
    ch = min(CAST_ROWS, tn)
    for c in range(tn // ch):
        wb_ref[:, c * ch:(c + 1) * ch] = wt_ref[c * ch:(c + 1) * ch, :].T.astype(BF16)


def _rmsnorm_kernel(h_ref, g_ref, o_ref):
    x = h_ref[...]
    ms = jnp.mean(x * x, axis=-1, keepdims=True)
    o_ref[...] = (x * lax.rsqrt(ms + NORM_EPS) * g_ref[...]).astype(o_ref.dtype)


def rmsnorm(h, g, layer, out_dtype, tm=512):
    m, d = h.shape
    if g.ndim == 1:
        g = g.reshape(1, 1, d)
        layer = 0
    else:
        g = g.reshape(g.shape[0], 1, d)
    return pl.pallas_call(
        _rmsnorm_kernel,
        grid=(m // tm,),
        in_specs=[pl.BlockSpec((tm, d), lambda i: (i, 0)),
                  pl.BlockSpec((None, 1, d), lambda i: (layer, 0, 0))],
        out_specs=pl.BlockSpec((tm, d), lambda i: (i, 0)),
        out_shape=jax.ShapeDtypeStruct((m, d), out_dtype),
        compiler_params=_params(1),
        name="rmsnorm",
    )(h, g)


def _fused_mm_kernel(*refs, n_x, n_w, w_x, n_e, epilogue, w_t):
    x_refs = refs[:n_x]
    w_refs = refs[n_x:n_x + n_w]
    e_refs = refs[n_x + n_w:n_x + n_w + n_e]
    o_ref = refs[n_x + n_w + n_e]
    wb_refs = refs[n_x + n_w + n_e + 1:]

    @pl.when(pl.program_id(1) == 0)
    def _():
        for w_ref, wb_ref in zip(w_refs, wb_refs):
            (_cast_rows_transposed if w_t else _cast_rows)(w_ref, wb_ref)

    xs = [x_ref[...].astype(BF16) for x_ref in x_refs]
    accs = [jnp.dot(xs[w_x[l]], wb_refs[l][...], preferred_element_type=F32)
            for l in range(n_w)]
    o_ref[...] = epilogue(accs, [e[...] for e in e_refs], pl.program_id(0)).astype(o_ref.dtype)


def fused_mm(xs, ws, extras, epilogue, n_out, out_dtype, tm, tn, name, w_t=False):
    m = xs[0][0].shape[-2]
    grid = (n_out // tn, m // tm)
    in_specs, args = [], []
    for arr, lead in xs:
        k = arr.shape[-1]
        if arr.ndim == 2:
            in_specs.append(pl.BlockSpec((tm, k), lambda j, i: (i, 0)))
        else:
            in_specs.append(pl.BlockSpec((None, tm, k), lambda j, i, lead=lead: (lead, i, 0)))
        args.append(arr)
    scratch = []
    for arr, lead, off, _ in ws:
        k = arr.shape[-1] if w_t else arr.shape[-2]
        nlead = arr.ndim - 2
        lead = tuple(lead) if nlead else ()
        if w_t:
            in_specs.append(pl.BlockSpec(
                (None,) * nlead + (tn, k),
                lambda j, i, lead=lead, off=off: lead + (j + off, 0)))
        else:
            in_specs.append(pl.BlockSpec(
                (None,) * nlead + (k, tn),
                lambda j, i, lead=lead, off=off: lead + (0, j + off)))
        args.append(arr)
        scratch.append(pltpu.VMEM((k, tn), BF16))
    for arr, lead, kind, off in extras:
        if kind == "tile":
            if arr.ndim == 2:
                in_specs.append(pl.BlockSpec((tm, tn), lambda j, i, off=off: (i, j + off)))
            else:
                in_specs.append(pl.BlockSpec(
                    (None, tm, tn), lambda j, i, lead=lead, off=off: (lead, i, j + off)))
        else:
            in_specs.append(pl.BlockSpec(
                (None, 1, tn), lambda j, i, lead=lead, off=off: (lead, 0, j + off)))
        args.append(arr)
    kern = functools.partial(
        _fused_mm_kernel, n_x=len(xs), n_w=len(ws), w_x=tuple(w[3] for w in ws),
        n_e=len(extras), epilogue=epilogue, w_t=w_t)
    return pl.pallas_call(
        kern,
        grid=grid,
        in_specs=in_specs,
        out_specs=pl.BlockSpec((tm, tn), lambda j, i: (i, j)),
        out_shape=jax.ShapeDtypeStruct((m, n_out), out_dtype),
        scratch_shapes=scratch,
        compiler_params=_params(2),
        name=name,
    )(*args)


def _shifted_mm_kernel(x_ref, wa_ref, wn_ref, o_ref, wb_ref, *, shift):
    tn = wa_ref.shape[0]

    @pl.when(pl.program_id(1) == 0)
    def _():
        for c in range(tn // CAST_ROWS):
            lo = shift + c * CAST_ROWS
            hi = lo + CAST_ROWS
            if hi <= tn:
                rows = wa_ref[lo:hi, :]
            else:
                rows = jnp.concatenate([wa_ref[lo:tn, :], wn_ref[0:hi - tn, :]], axis=0)
            wb_ref[:, c * CAST_ROWS:(c + 1) * CAST_ROWS] = rows.T.astype(BF16)

    acc = jnp.dot(x_ref[...], wb_ref[...], preferred_element_type=F32)
    o_ref[...] = _sigmoid(acc).astype(o_ref.dtype)


def shifted_sigmoid_mm(x, wt, layer, row0, shift, n_out, tm, tn, name):
    m, k = x.shape
    base = row0 // tn
    return pl.pallas_call(
        functools.partial(_shifted_mm_kernel, shift=shift),
        grid=(n_out // tn, m // tm),
        in_specs=[
            pl.BlockSpec((tm, k), lambda j, i: (i, 0)),
            pl.BlockSpec((None, tn, k), lambda j, i: (layer, base + j, 0)),
            pl.BlockSpec((None, shift, k), lambda j, i: (layer, (base + j + 1) * (tn // shift), 0)),
        ],
        out_specs=pl.BlockSpec((tm, tn), lambda j, i: (i, j)),
        out_shape=jax.ShapeDtypeStruct((m, n_out), BF16),
        scratch_shapes=[pltpu.VMEM((k, tn), BF16)],
        compiler_params=_params(2),
        name=name,
    )(x, wt, wt)


def _ep_gelu(accs, extras, j):
    return _gelu_tanh(accs[0])


def _ep_sigmoid(accs, extras, j):
    return _sigmoid(accs[0])


def _ep_plain(accs, extras, j):
    return accs[0]


def _ep_merge(accs, extras, j):
    return extras[0].astype(F32) * accs[0] + extras[1].astype(F32) * accs[1]


def _ep_residual(accs, extras, j):
    return extras[0] + accs[0]


def _ep_swiglu(accs, extras, j):
    return _silu(accs[0]) * accs[1]


def _ep_ple(accs, extras, j):
    return extras[0] + _sigmoid(accs[0] + extras[1]) * accs[1]


def _sgu_kernel(u_ref, v_ref, lng_ref, lnb_ref, w_ref, bt_ref, o_ref, wm_ref):
    c = SGU_CHUNK

    @pl.when(pl.program_id(0) == 0)
    def _():
        ti = lax.broadcasted_iota(jnp.int32, (c, c), 0)
        si = lax.broadcasted_iota(jnp.int32, (c, c), 1)
        for g in range(SGU_GROUPS):
            wm_ref[g] = jnp.where(si <= ti, w_ref[g], 0.0).astype(BF16)

    v = v_ref[...].astype(F32)
    mu = jnp.mean(v, axis=-1, keepdims=True)
    vc = v - mu
    var = jnp.mean(vc * vc, axis=-1, keepdims=True)
    vln = (vc * lax.rsqrt(var + NORM_EPS) * lng_ref[...] + lnb_ref[...]).astype(BF16)
    bt = bt_ref[...]
    for g in range(SGU_GROUPS):
        sl = slice(g * c, (g + 1) * c)
        mixed = jnp.dot(wm_ref[g], vln[:, sl], preferred_element_type=F32) + bt[:, g:g + 1]
        o_ref[:, sl] = (u_ref[:, sl].astype(F32) * mixed).astype(o_ref.dtype)


def sgu(proj, ln_g, ln_b, w_s, b_s_t, layer):
    m = proj.shape[0]
    c = SGU_CHUNK
    return pl.pallas_call(
        _sgu_kernel,
        grid=(m // c,),
        in_specs=[
            pl.BlockSpec((c, SGU_WIDTH), lambda i: (i, 0)),
            pl.BlockSpec((c, SGU_WIDTH), lambda i: (i, 1)),
            pl.BlockSpec((None, 1, SGU_WIDTH), lambda i: (layer, 0, 0)),
            pl.BlockSpec((None, 1, SGU_WIDTH), lambda i: (layer, 0, 0)),
            pl.BlockSpec((None, SGU_GROUPS, c, c), lambda i: (layer, 0, 0, 0)),
            pl.BlockSpec((None, c, SGU_GROUPS), lambda i: (layer, 0, 0)),
        ],
        out_specs=pl.BlockSpec((c, SGU_WIDTH), lambda i: (i, 0)),
        out_shape=jax.ShapeDtypeStruct((m, SGU_WIDTH), BF16),
        scratch_shapes=[pltpu.VMEM((SGU_GROUPS, c, c), BF16)],
        compiler_params=_params(1),
        name="sgu",
    )(proj, proj, ln_g, ln_b, w_s, b_s_t)


def _bdot(a, b):
    return jnp.dot(a.astype(BF16), b.astype(BF16), preferred_element_type=F32)


def _bdot_nt(a, b):
    return lax.dot_general(a.astype(BF16), b.astype(BF16), (((1,), (1,)), ((), ())),
                           preferred_element_type=F32)


def _gdn_kernel(q_ref, k_ref, v_ref, z_ref, sm_ref, cw_ref, alog_ref, dtb_ref, nw_ref,
                o_ref, s_ref, prev_ref, *, nb):
    c = DN_CHUNK
    hd = DN_HEAD_DIM
    w = DN_WIDTH
    pw = 2 * hd
    n_pairs = DN_HEADS // 2

    @pl.when(pl.program_id(1) == 0)
    def _():
        s_ref[...] = jnp.zeros(s_ref.shape, F32)
        prev_ref[...] = jnp.zeros(prev_ref.shape, F32)

    row_w = lax.broadcasted_iota(jnp.int32, (c, w), 0)
    row_s = lax.broadcasted_iota(jnp.int32, (c, 128), 0)
    lane = lax.broadcasted_iota(jnp.int32, (c, 128), 1)
    left = lane < c
    jmod = jnp.where(left, lane, lane - c)
    tri = row_s >= jmod
    strict = row_s > jmod
    eye = (row_s == jmod).astype(F32)
    r256 = lax.broadcasted_iota(jnp.int32, (pw, pw), 0)
    c256 = lax.broadcasted_iota(jnp.int32, (pw, pw), 1)
    bd_mask = (r256 < hd) == (c256 < hd)
    zeros_h = jnp.zeros((c, hd), F32)
    nw = nw_ref[...]
    scale = hd ** -0.5

    def bcast2(col_a, col_b):
        return jnp.concatenate([jnp.broadcast_to(col_a, (c, hd)),
                                jnp.broadcast_to(col_b, (c, hd))], axis=1)

    def blockdiag_rows(x):
        return jnp.concatenate([jnp.where(left, x, 0.0), jnp.where(left, 0.0, x)], axis=0)

    def blockdiag_heads(x):
        return jnp.concatenate(
            [jnp.concatenate([x[:, :hd], zeros_h], axis=1),
             jnp.concatenate([zeros_h, x[:, hd:]], axis=1)], axis=0)

    def inv_norm(x):
        ss_a = jnp.sum(x[:, :hd] * x[:, :hd], axis=-1, keepdims=True)
        ss_b = jnp.sum(x[:, hd:] * x[:, hd:], axis=-1, keepdims=True)
        return bcast2(lax.rsqrt(ss_a + NORM_EPS), lax.rsqrt(ss_b + NORM_EPS))

    chains = []
    for b in range(nb):
        def conv_silu(x_ref, sec, b=b):
            x = x_ref[b].astype(F32)
            prev = prev_ref[b * 3 + sec]
            cw = cw_ref[:, sec * w:(sec + 1) * w]
            acc = cw[DN_CONV - 1:DN_CONV, :] * x
            for s in range(1, DN_CONV):
                shifted = jnp.where(row_w < s, pltpu.roll(prev, s, 0), pltpu.roll(x, s, 0))
                acc = acc + cw[DN_CONV - 1 - s:DN_CONV - s, :] * shifted
            prev_ref[b * 3 + sec] = x
            return _silu(acc)

        q_all = conv_silu(q_ref, 0)
        k_all = conv_silu(k_ref, 1)
        v_all = conv_silu(v_ref, 2)

        sm = sm_ref[b]
        beta_all = _sigmoid(sm)
        xa = sm + dtb_ref[...]
        softplus = jnp.maximum(xa, 0.0) + jnp.log(1.0 + jnp.exp(-jnp.abs(xa)))
        g = -jnp.exp(alog_ref[...]) * softplus
        s = 1
        while s < c:
            g = g + jnp.where(row_s >= s, pltpu.roll(g, s, 0), 0.0)
            s *= 2
        g_t = jnp.concatenate([g, pltpu.roll(g, 127, 1)], axis=0).T

        for p in range(n_pairs):
            a = 2 * p
            sl = slice(p * pw, (p + 1) * pw)
            q = q_all[:, sl]
            k = k_all[:, sl]
            v = v_all[:, sl]
            q = q * (inv_norm(q) * scale)
            k = k * inv_norm(k)
            gc_a = g[:, DN_HEADS + a:DN_HEADS + a + 1]
            gc_b = g[:, DN_HEADS + a + 1:DN_HEADS + a + 2]
            gl_a = gc_a[c - 1:c, :]
            gl_b = gc_b[c - 1:c, :]
            grow = g_t[DN_HEADS + a:DN_HEADS + a + 1, :]
            gcol = jnp.where(left, gc_a, gc_b)
            decay = jnp.exp(jnp.where(tri, gcol - grow, -1e30))
            beta = bcast2(beta_all[:, a:a + 1], beta_all[:, a + 1:a + 2])
            egc = bcast2(jnp.exp(gc_a), jnp.exp(gc_b))
            kb = k * beta
            chains.append(dict(
                b=b, p=p, q=q, k=k, kb=kb, vb=v * beta, kbg=kb * egc, qg=q * egc,
                kd=k * bcast2(jnp.exp(gl_a - gc_a), jnp.exp(gl_b - gc_b)),
                dl=jnp.concatenate([jnp.broadcast_to(jnp.exp(gl_a), (1, hd)),
                                    jnp.broadcast_to(jnp.exp(gl_b), (1, hd))], axis=1),
                decay=decay))

    for ch in chains:
        kq = _bdot_nt(jnp.concatenate([ch["kb"], ch["q"]], axis=0), blockdiag_heads(ch["k"]))
        ch["bk"] = -jnp.where(strict, kq[:c] * ch["decay"], 0.0)
        ch["attn"] = jnp.where(tri, kq[c:] * ch["decay"], 0.0)
        ch["qm"] = eye + ch["bk"]
    for ch in chains:
        ch["bk"] = _bdot(ch["bk"], blockdiag_rows(ch["bk"]))
    for lvl in range(1, 6):
        for ch in chains:
            if lvl < 5:
                r = _bdot(jnp.concatenate([ch["qm"], ch["bk"]], axis=0), blockdiag_rows(ch["bk"]))
                ch["qm"] = ch["qm"] + r[:c]
                ch["bk"] = r[c:]
            else:
                ch["qm"] = ch["qm"] + _bdot(ch["qm"], blockdiag_rows(ch["bk"]))
    for ch in chains:
        vb, kbg = ch["vb"], ch["kbg"]
        rhs = jnp.concatenate(
            [jnp.concatenate([vb[:, :hd], kbg[:, :hd], zeros_h, zeros_h], axis=1),
             jnp.concatenate([zeros_h, zeros_h, vb[:, hd:], kbg[:, hd:]], axis=1)], axis=0)
        sol = _bdot(ch["qm"], rhs)
        ch["u"] = jnp.concatenate([sol[:, :hd], sol[:, 2 * hd:3 * hd]], axis=1)
        ch["w"] = jnp.concatenate([sol[:, hd:2 * hd], sol[:, 3 * hd:]], axis=1)
    for ch in chains:
        ch["state"] = s_ref[ch["b"] * n_pairs + ch["p"]]
        r2 = _bdot(jnp.concatenate([ch["w"], ch["qg"]], axis=0), ch["state"])
        ch["vn"] = ch["u"] - r2[:c]
        ch["o"] = r2[c:]
    for ch in chains:
        ch["o"] = ch["o"] + _bdot(ch["attn"], blockdiag_heads(ch["vn"]))
    for ch in chains:
        upd = ch["state"] * ch["dl"] + _bdot(ch["kd"].T, ch["vn"])
        s_ref[ch["b"] * n_pairs + ch["p"]] = jnp.where(bd_mask, upd, 0.0)
    for ch in chains:
        b, p = ch["b"], ch["p"]
        for half in range(2):
            o = ch["o"][:, half * hd:(half + 1) * hd]
            sl = slice(p * pw + half * hd, p * pw + (half + 1) * hd)
            o = o * lax.rsqrt(jnp.mean(o * o, axis=-1, keepdims=True) + NORM_EPS)
            o = o * nw * _silu(z_ref[b, :, sl].astype(F32))
            o_ref[b, :, sl] = o.astype(o_ref.dtype)


def gdn(proj, small, conv_w, alog_row, dtb_row, norm_w, layer, bsz, seq, nb=1):
    m = proj.shape[0]
    c = DN_CHUNK
    n = seq // c
    w = DN_WIDTH
    qkv0 = 0
    proj3 = proj.reshape(bsz, seq, proj.shape[-1])
    small3 = small.reshape(bsz, seq, small.shape[-1])
    pw = 2 * DN_HEAD_DIM
    out = pl.pallas_call(
        functools.partial(_gdn_kernel, nb=nb),
        grid=(bsz // nb, n),
        in_specs=[
            pl.BlockSpec((nb, c, w), lambda b, t: (b, t, qkv0)),
            pl.BlockSpec((nb, c, w), lambda b, t: (b, t, qkv0 + 1)),
            pl.BlockSpec((nb, c, w), lambda b, t: (b, t, qkv0 + 2)),
            pl.BlockSpec((nb, c, w), lambda b, t: (b, t, qkv0 + 3)),
            pl.BlockSpec((nb, c, 128), lambda b, t: (b, t, 0)),
            pl.BlockSpec((None, DN_CONV, 3 * w), lambda b, t: (layer, 0, 0)),
            pl.BlockSpec((None, 1, 128), lambda b, t: (layer, 0, 0)),
            pl.BlockSpec((None, 1, 128), lambda b, t: (layer, 0, 0)),
            pl.BlockSpec((None, 1, DN_HEAD_DIM), lambda b, t: (layer, 0, 0)),
        ],
        out_specs=pl.BlockSpec((nb, c, w), lambda b, t: (b, t, 0)),
        out_shape=jax.ShapeDtypeStruct((bsz, seq, w), BF16),
        scratch_shapes=[pltpu.VMEM((nb * (DN_HEADS // 2), pw, pw), F32),
                        pltpu.VMEM((nb * 3, c, w), F32)],
        compiler_params=_params(2),
        name="gdn",
    )(proj3, proj3, proj3, proj3, small3, conv_w, alog_row, dtb_row, norm_w)
    return out.reshape(m, w)


def _route_kernel(h_ref, g_ref, r_ref, hn_ref, ti_ref, tw_ref):
    x = h_ref[...]
    ms = jnp.mean(x * x, axis=-1, keepdims=True)
    hn = x * lax.rsqrt(ms + NORM_EPS) * g_ref[...]
    hn_ref[...] = hn
    logits = jnp.dot(hn, r_ref[...], preferred_element_type=F32,
                     precision=lax.Precision.HIGHEST)
    lane = lax.broadcasted_iota(jnp.int32, logits.shape, 1)
    neg = jnp.float32(-3.0e38)
    logits = jnp.where(lane < N_EXPERTS, logits, neg)
    m1 = jnp.max(logits, axis=-1, keepdims=True)
    i1 = jnp.min(jnp.where(logits == m1, lane, 128), axis=-1, keepdims=True)
    rest = jnp.where(lane == i1, neg, logits)
    m2 = jnp.max(rest, axis=-1, keepdims=True)
    i2 = jnp.min(jnp.where(rest == m2, lane, 128), axis=-1, keepdims=True)
    e2 = jnp.exp(m2 - m1)
    w1 = 1.0 / (1.0 + e2)
    w2 = e2 / (1.0 + e2)
    ti_ref[...] = jnp.where(lane == 0, i1, jnp.where(lane == 1, i2, 0))
    tw_ref[...] = jnp.where(lane == 0, w1, jnp.where(lane == 1, w2, 0.0))


def route(h, g, layer, router_pad, j, tm=256):
    m, d = h.shape
    g = g.reshape(g.shape[0], 1, d)
    return pl.pallas_call(
        _route_kernel,
        grid=(m // tm,),
        in_specs=[pl.BlockSpec((tm, d), lambda i: (i, 0)),
                  pl.BlockSpec((None, 1, d), lambda i: (layer, 0, 0)),
                  pl.BlockSpec((None, d, 128), lambda i: (j, 0, 0))],
        out_specs=[pl.BlockSpec((tm, d), lambda i: (i, 0)),
                   pl.BlockSpec((tm, 128), lambda i: (i, 0)),
                   pl.BlockSpec((tm, 128), lambda i: (i, 0))],
        out_shape=[jax.ShapeDtypeStruct((m, d), F32),
                   jax.ShapeDtypeStruct((m, 128), jnp.int32),
                   jax.ShapeDtypeStruct((m, 128), F32)],
        compiler_params=_params(1),
        name="route",
    )(h, g, router_pad)


def _row_copy(src_ref, src_row, dst_ref, dst_row, sem):
    return pltpu.make_async_copy(src_ref.at[pl.ds(src_row, 1)], dst_ref.at[pl.ds(dst_row, 1)], sem)


def _moe_gather_kernel(pos_ref, tstart_ref, tend_ref, hn_ref, xs_ref, fill_sem, row_sem):
    step = pl.program_id(0)
    tm = MOE_TM

    n_tiles = xs_ref.shape[0] // tm
    n_active = tend_ref[N_EXPERTS - 1]

    def fill_tile(t):
        dst = pl.multiple_of(t * tm, tm)
        return pltpu.make_async_copy(hn_ref, xs_ref.at[pl.ds(dst, tm)], fill_sem)

    def fills(act):
        for e in range(N_EXPERTS):
            @pl.when(tend_ref[e] > tstart_ref[e])
            def _():
                act(fill_tile(tend_ref[e] - 1))
        for t in range(N_EXPERTS):
            @pl.when(n_active + t < n_tiles)
            def _():
                act(fill_tile(n_active + t))

    @pl.when(step == 0)
    def _():
        fills(lambda cp: cp.start())
        fills(lambda cp: cp.wait())

    base = step * tm * TOP_K

    def issue(t, carry):
        for slot in range(TOP_K):
            _row_copy(hn_ref, t, xs_ref, pos_ref[base + t * TOP_K + slot], row_sem).start(
                priority=slot)
        return carry

    lax.fori_loop(0, tm, issue, 0, unroll=4)

    def drain(i, carry):
        _row_copy(hn_ref, 0, xs_ref, 0, row_sem).wait()
        return carry

    lax.fori_loop(0, tm * TOP_K, drain, 0)


def moe_gather(hn, pos, tile_start, tile_end, rows):
    m, d = hn.shape
    tm = MOE_TM
    return pl.pallas_call(
        _moe_gather_kernel,
        grid_spec=pltpu.PrefetchScalarGridSpec(
            num_scalar_prefetch=3,
            grid=(m // tm,),
            in_specs=[pl.BlockSpec((tm, d), lambda i, pos, ts, te: (i, 0))],
            out_specs=pl.BlockSpec(memory_space=pl.ANY),
            scratch_shapes=[pltpu.SemaphoreType.DMA(()), pltpu.SemaphoreType.DMA(())],
        ),
        out_shape=jax.ShapeDtypeStruct((rows, d), F32),
        compiler_params=_params(1),
        name="moe_gather",
    )(pos, tile_start, tile_end, hn)


def _moe_group_kernel(ts_ref, te_ref, x_ref, *refs, n_w, compute, mults, k_split):
    w_refs = refs[:n_w]
    o_ref = refs[n_w]
    wb_refs = refs[n_w + 1:2 * n_w + 1]
    xbuf, obuf, xsem, osem = refs[2 * n_w + 1:]
    tm = MOE_TM
    tn = obuf.shape[2]
    n_tiles = o_ref.shape[0] // tm
    e = pl.program_id(1)
    kh = pl.program_id(2)
    col = pl.multiple_of(pl.program_id(0) * tn, tn)
    t0 = ts_ref[e]
    n = te_ref[e] - t0
    kblk = w_refs[0].shape[0]

    @pl.when(n > 0)
    def _():
        for w_ref, wb_ref in zip(w_refs, wb_refs):
            if k_split == 1:
                _cast_rows(w_ref, wb_ref)
            else:
                _cast_rows(w_ref, wb_ref.at[pl.ds(pl.multiple_of(kh * kblk, kblk), kblk)])

    big = mults[0]
    n_big = n // big
    rem = n - n_big * big

    def step_info(s):
        mult = jnp.where(s < n_big, big, 0)
        off = s * big
        idx = n_big
        first = n_big * big
        for mlt in mults[1:]:
            present = ((rem // mlt) % 2).astype(jnp.int32)
            here = jnp.logical_and(present == 1, s == idx)
            mult = jnp.where(here, mlt, mult)
            off = jnp.where(here, first, off)
            idx = idx + present
            first = first + present * mlt
        return mult, off

    n_steps = n_big
    for mlt in mults[1:]:
        n_steps = n_steps + (rem // mlt) % 2

    def for_mult(mult, fn):
        for mlt in mults:
            @pl.when(mult == mlt)
            def _():
                fn(mlt)

    def x_copy(off, mlt, slot):
        row = pl.multiple_of((t0 + off) * tm, tm)
        return pltpu.make_async_copy(x_ref.at[pl.ds(row, mlt * tm)],
                                     xbuf.at[slot, pl.ds(0, mlt * tm)], xsem.at[slot])

    def o_copy(tile, mlt, slot):
        row = pl.multiple_of(tile * tm, tm)
        return pltpu.make_async_copy(
            obuf.at[slot, pl.ds(0, mlt * tm)],
            o_ref.at[pl.ds(row, mlt * tm), pl.ds(col, tn)], osem.at[slot])

    @pl.when(jnp.logical_and(n > 0, kh == k_split - 1))
    def _():
        m0, off0 = step_info(0)
        for_mult(m0, lambda mlt: x_copy(off0, mlt, 0).start(priority=1))

        def body(s, carry):
            slot = lax.rem(s, 2)
            ms, offs = step_info(s)
            for_mult(ms, lambda mlt: x_copy(offs, mlt, slot).wait())

            @pl.when(s + 1 < n_steps)
            def _():
                mn, offn = step_info(s + 1)
                for_mult(mn, lambda mlt: x_copy(offn, mlt, 1 - slot).start(priority=1))

            @pl.when(s >= 2)
            def _():
                mp, offp = step_info(s - 2)
                for_mult(mp, lambda mlt: o_copy(t0 + offp, mlt, slot).wait())

            def run(mlt):
                rows = mlt * tm
                obuf[slot, :rows] = compute(
                    xbuf[slot, :rows], [wb[...] for wb in wb_refs]).astype(obuf.dtype)
                o_copy(t0 + offs, mlt, slot).start()

            for_mult(ms, run)
            return carry

        lax.fori_loop(0, n_steps, body, 0)

        @pl.when(n_steps >= 2)
        def _():
            mp, offp = step_info(n_steps - 2)
            for_mult(mp, lambda mlt: o_copy(t0 + offp, mlt, lax.rem(n_steps, 2)).wait())

        ml, offl = step_info(n_steps - 1)
        for_mult(ml, lambda mlt: o_copy(t0 + offl, mlt, lax.rem(n_steps - 1, 2)).wait())

    @pl.when(jnp.logical_and(e == N_EXPERTS - 1, kh == k_split - 1))
    def _():
        n_active = te_ref[N_EXPERTS - 1]
        obuf[0, :tm] = jnp.zeros((tm, tn), obuf.dtype)
        for k in range(N_EXPERTS):
            @pl.when(n_active + k < n_tiles)
            def _():
                o_copy(n_active + k, 1, 0).start()
        for k in range(N_EXPERTS):
            @pl.when(n_active + k < n_tiles)
            def _():
                o_copy(n_active + k, 1, 0).wait()


def _moe_up_compute(x, wbs):
    xb = x.astype(BF16)
    a = jnp.dot(xb, wbs[0], preferred_element_type=F32)
    b = jnp.dot(xb, wbs[1], preferred_element_type=F32)
    return _silu(a) * b


def _moe_down_compute(x, wbs):
    return jnp.dot(x, wbs[0], preferred_element_type=F32)


def moe_group_mm(x, ws, j, tile_start, tile_end, compute, n_out, out_dtype, tn, mults, k_split,
                 name):
    rows, k = x.shape
    tm = MOE_TM
    w_specs = [pl.BlockSpec((None, None, k // k_split, tn),
                            lambda c, e, kh, ts, te: (j, e, kh, c)) for _ in ws]
    return pl.pallas_call(
        functools.partial(_moe_group_kernel, n_w=len(ws), compute=compute, mults=mults,
                          k_split=k_split),
        grid_spec=pltpu.PrefetchScalarGridSpec(
            num_scalar_prefetch=2,
            grid=(n_out // tn, N_EXPERTS, k_split),
            in_specs=[pl.BlockSpec(memory_space=pl.ANY)] + w_specs,
            out_specs=pl.BlockSpec(memory_space=pl.ANY),
            scratch_shapes=[pltpu.VMEM((k, tn), BF16) for _ in ws] + [
                pltpu.VMEM((2, mults[0] * tm, k), x.dtype),
                pltpu.VMEM((2, mults[0] * tm, tn), out_dtype),
                pltpu.SemaphoreType.DMA((2,)),
                pltpu.SemaphoreType.DMA((2,)),
            ],
        ),
        out_shape=jax.ShapeDtypeStruct((rows, n_out), out_dtype),
        compiler_params=_params(3),
        name=name,
    )(tile_start, tile_end, x, *ws)


def moe_experts(xs, tile_start, tile_end, w_gate, w_up, w_down, j):
    d = xs.shape[1]
    f = w_gate.shape[-1]
    hmid = moe_group_mm(xs, [w_gate, w_up], j, tile_start, tile_end, _moe_up_compute,
                        f, BF16, 1024, (1,), 1, "moe_up")
    return moe_group_mm(hmid, [w_down], j, tile_start, tile_end, _moe_down_compute,
                        d, F32, 512, (2, 1), 2, "moe_down")


def _moe_combine_kernel(pos_ref, ys_ref, h_ref, tw_ref, g_ref, ho_ref, hn_ref, buf_ref, sem):
    tm = MOE_COMBINE_TM
    base = pl.program_id(0) * tm * TOP_K

    def issue(i, carry):
        for slot in range(TOP_K):
            _row_copy(ys_ref, pos_ref[base + i * TOP_K + slot], buf_ref.at[slot], i, sem).start(
                priority=slot)
        return carry

    lax.fori_loop(0, tm, issue, 0, unroll=4)

    def drain(i, carry):
        _row_copy(ys_ref, 0, buf_ref.at[0], 0, sem).wait()
        return carry

    lax.fori_loop(0, tm * TOP_K, drain, 0)

    tw = tw_ref[...]
    x = h_ref[...] + tw[:, 0:1] * buf_ref[0] + tw[:, 1:2] * buf_ref[1]
    ho_ref[...] = x
    ms = jnp.mean(x * x, axis=-1, keepdims=True)
    hn_ref[...] = (x * lax.rsqrt(ms + NORM_EPS) * g_ref[...]).astype(hn_ref.dtype)


def moe_combine(ys, pos, h, tw, norm_g, layer):
    m, d = h.shape
    tm = MOE_COMBINE_TM
    norm_g = norm_g.reshape(norm_g.shape[0], 1, d)
    return pl.pallas_call(
        _moe_combine_kernel,
        grid_spec=pltpu.PrefetchScalarGridSpec(
            num_scalar_prefetch=1,
            grid=(m // tm,),
            in_specs=[
                pl.BlockSpec(memory_space=pl.ANY),
                pl.BlockSpec((tm, d), lambda i, pos: (i, 0)),
                pl.BlockSpec((tm, 128), lambda i, pos: (i, 0)),
                pl.BlockSpec((None, 1, d), lambda i, pos: (layer, 0, 0)),
            ],
            out_specs=[pl.BlockSpec((tm, d), lambda i, pos: (i, 0)),
                       pl.BlockSpec((tm, d), lambda i, pos: (i, 0))],
            scratch_shapes=[pltpu.VMEM((TOP_K, tm, d), F32), pltpu.SemaphoreType.DMA(())],
        ),
        out_shape=[jax.ShapeDtypeStruct((m, d), F32), jax.ShapeDtypeStruct((m, d), BF16)],
        compiler_params=_params(1),
        name="moe_combine",
    )(pos, ys, h, tw, norm_g)


def moe_layer(h, norm_g, layer, router_pad, w_gate, w_up, w_down, j, next_norm_g):
    m, d = h.shape
    tm = MOE_TM
    hn, ti, tw = route(h, norm_g, layer, router_pad, j)
    top_i = ti[:, :TOP_K].reshape(-1)
    n_pairs = m * TOP_K
    n_tiles = n_pairs // tm + N_EXPERTS
    onehot = (top_i[:, None] == jnp.arange(N_EXPERTS, dtype=jnp.int32)[None, :]).astype(jnp.int32)
    csum = jnp.cumsum(onehot, axis=0)
    rank = jnp.sum((csum - onehot) * onehot, axis=1)
    counts = csum[-1]
    tiles_per = (counts + tm - 1) // tm
    tile_end = jnp.cumsum(tiles_per).astype(jnp.int32)
    tile_start = (tile_end - tiles_per).astype(jnp.int32)
    pos = (jnp.sum(onehot * tile_start[None, :], axis=1) * tm + rank).astype(jnp.int32)

    xs = moe_gather(hn, pos, tile_start, tile_end, n_tiles * tm)
    ys = moe_experts(xs, tile_start, tile_end, w_gate, w_up, w_down, j)
    return moe_combine(ys, pos, h, tw, next_norm_g, layer)


def kernel(x, p, norm_mix, w_in, sgu_ln_g, sgu_ln_b, sgu_w, sgu_b, dn_conv_w, dn_a_log,
           dn_dt_bias, dn_norm_w, w_branch, w_out, norm_ffn, ffn_w_gate, ffn_w_up,
           ffn_w_down, moe_router, moe_w_gate, moe_w_up, moe_w_down, norm_ple,
           ple_w_gate, ple_b_gate, ple_w_proj, norm_final):
    bsz, seq, d = x.shape
    depth = p.shape[0]
    m = bsz * seq
    h = x.reshape(m, d)
    p2 = p.reshape(depth, m, p.shape[-1])
    main_w = 2 * SGU_WIDTH + 4 * DN_WIDTH
    n_small = 2 * DN_HEADS
    w_in_t = jnp.swapaxes(w_in, 1, 2)
    pad_h = 128 - n_small
    alog_row = jnp.pad(dn_a_log, ((0, 0), (DN_HEADS, pad_h))).reshape(depth, 1, 128)
    dtb_row = jnp.pad(dn_dt_bias, ((0, 0), (DN_HEADS, pad_h))).reshape(depth, 1, 128)
    ln_g = sgu_ln_g.reshape(depth, 1, SGU_WIDTH)
    ln_b = sgu_ln_b.reshape(depth, 1, SGU_WIDTH)
    sgu_b_t = jnp.swapaxes(sgu_b, 1, 2)
    nw = dn_norm_w.reshape(depth, 1, DN_HEAD_DIM)
    router_pad = jnp.pad(moe_router, ((0, 0), (0, 0), (0, 128 - N_EXPERTS)))
    ple_b = ple_b_gate.reshape(depth, 1, d)

    for i in range(depth):
        hn = rmsnorm(h, norm_mix, i, BF16)
        uv = fused_mm([(hn, 0)], [(w_in_t, (i,), 0, 0)], [], _ep_gelu, 2 * SGU_WIDTH, BF16,
                      1024, 512, "proj_uv", w_t=True)
        qkvz = fused_mm([(hn, 0)], [(w_in_t, (i,), (2 * SGU_WIDTH) // 512, 0)], [], _ep_plain,
                        4 * DN_WIDTH, BF16, 1024, 512, "proj_qkvz", w_t=True)
        gates = shifted_sigmoid_mm(hn, w_in_t, i, main_w, n_small, 2 * d, 1024, 512, "proj_gates")
        small = fused_mm([(hn, 0)], [(w_in_t, (i,), main_w // 128, 0)], [], _ep_plain, 128, F32,
                         1024, 128, "proj_small", w_t=True)
        y_a = sgu(uv, ln_g, ln_b, sgu_w, sgu_b_t, i)
        y_b = gdn(qkvz, small, dn_conv_w, alog_row, dtb_row, nw, i, bsz, seq, nb=2)
        merged = fused_mm([(y_a, 0), (y_b, 0)],
                          [(w_branch, (i, 0), 0, 0), (w_branch, (i, 1), 0, 1)],
                          [(gates, 0, "tile", 0), (gates, 0, "tile", d // 512)],
                          _ep_merge, d, BF16, 1024, 512, "merge")
        h = fused_mm([(merged, 0)], [(w_out, (i,), 0, 0)], [(h, 0, "tile", 0)],
                     _ep_residual, d, F32, 1024, 512, "out_proj")
        j = i // 2
        if i % 2 == 0:
            hn = rmsnorm(h, norm_ffn, i, BF16)
            hmid = fused_mm([(hn, 0)], [(ffn_w_gate, (j,), 0, 0), (ffn_w_up, (j,), 0, 0)], [],
                            _ep_swiglu, ffn_w_gate.shape[-1], BF16, 1024, 512, "ffn_up")
            h = fused_mm([(hmid, 0)], [(ffn_w_down, (j,), 0, 0)], [(h, 0, "tile", 0)],
                         _ep_residual, d, F32, 512, 512, "ffn_down")
            hn = rmsnorm(h, norm_ple, i, BF16)
        else:
            h, hn = moe_layer(h, norm_ffn, i, router_pad, moe_w_gate, moe_w_up, moe_w_down, j,
                              norm_ple)
        h = fused_mm([(hn, 0), (p2, i)],
                     [(ple_w_gate, (i,), 0, 0), (ple_w_proj, (i,), 0, 1)],
                     [(h, 0, "tile", 0), (ple_b, i, "row", 0)],
                     _ep_ple, d, F32, 1024, 512, "ple")
    out = rmsnorm(h, norm_final, 0, F32)
    return out.reshape(bsz, seq, d)
```

```python
import functools
import math

import jax
import jax.numpy as jnp
from jax import lax
from jax.experimental import pallas as pl
from jax.experimental.pallas import tpu as pltpu

F32 = jnp.float32
BF16 = jnp.bfloat16

D_MODEL = 2048
SGU_CHUNK = 128
SGU_GROUPS = 8
SGU_WIDTH = 1024
DN_HEADS = 8
DN_HEAD_DIM = 128
DN_WIDTH = 1024
DN_CONV = 4
DN_CHUNK = 64
N_EXPERTS = 8
TOP_K = 2
NORM_EPS = 1e-6

V7X_VMEM_LIMIT = 56 * 1024 * 1024
CAST_ROWS = 256
MOE_TM = 256
MOE_COMBINE_TM = 256


def _params(n_axes):
    return pltpu.CompilerParams(
        dimension_semantics=("arbitrary",) * n_axes,
        vmem_limit_bytes=V7X_VMEM_LIMIT)


def _sigmoid(x):
    return 1.0 / (1.0 + jnp.exp(-x))


def _silu(x):
    return x * _sigmoid(x)


def _gelu_tanh(x):
    c = math.sqrt(2.0 / math.pi)
    return 0.5 * x * (1.0 + jnp.tanh(c * (x + 0.044715 * (x * x * x))))


def _cast_rows(w_ref, wb_ref):
    k = w_ref.shape[0]
    ch = min(CAST_ROWS, k)

    def body(c, carry):
        r = pl.multiple_of(c * ch, ch)
        wb_ref[pl.ds(r, ch), :] = w_ref[pl.ds(r, ch), :].astype(BF16)
        return carry

    lax.fori_loop(0, k // ch, body, 0)


def _cast_rows_transposed(wt_ref, wb_ref):
    tn = wt_ref.shape[0]
    ch = min(CAST_ROWS, tn)
    for c in range(tn // ch):
        wb_ref[:, c * ch:(c + 1) * ch] = wt_ref[c * ch:(c + 1) * ch, :].T.astype(BF16)


def _rmsnorm_kernel(h_ref, g_ref, o_ref):
    x = h_ref[...]
    ms = jnp.mean(x * x, axis=-1, keepdims=True)
    o_ref[...] = (x * lax.rsqrt(ms + NORM_EPS) * g_ref[...]).astype(o_ref.dtype)


def rmsnorm(h, g, layer, out_dtype, tm=512):
    m, d = h.shape
    if g.ndim == 1:
        g = g.reshape(1, 1, d)
        layer = 0
    else:
        g = g.reshape(g.shape[0], 1, d)
    return pl.pallas_call(
        _rmsnorm_kernel,
        grid=(m // tm,),
        in_specs=[pl.BlockSpec((tm, d), lambda i: (i, 0)),
                  pl.BlockSpec((None, 1, d), lambda i: (layer, 0, 0))],
        out_specs=pl.BlockSpec((tm, d), lambda i: (i, 0)),
        out_shape=jax.ShapeDtypeStruct((m, d), out_dtype),
        compiler_params=_params(1),
        name="rmsnorm",
    )(h, g)


def _fused_mm_kernel(*refs, n_x, n_w, w_x, n_e, epilogue, w_t):
    x_refs = refs[:n_x]
    w_refs = refs[n_x:n_x + n_w]
    e_refs = refs[n_x + n_w:n_x + n_w + n_e]
    o_ref = refs[n_x + n_w + n_e]
    wb_refs = refs[n_x + n_w + n_e + 1:]

    @pl.when(pl.program_id(1) == 0)
    def _():
        for w_ref, wb_ref in zip(w_refs, wb_refs):
            (_cast_rows_transposed if w_t else _cast_rows)(w_ref, wb_ref)

    xs = [x_ref[...].astype(BF16) for x_ref in x_refs]
    accs = [jnp.dot(xs[w_x[l]], wb_refs[l][...], preferred_element_type=F32)
            for l in range(n_w)]
    o_ref[...] = epilogue(accs, [e[...] for e in e_refs], pl.program_id(0)).astype(o_ref.dtype)


def fused_mm(xs, ws, extras, epilogue, n_out, out_dtype, tm, tn, name, w_t=False):
    m = xs[0][0].shape[-2]
    grid = (n_out // tn, m // tm)
    in_specs, args = [], []
    for arr, lead in xs:
        k = arr.shape[-1]
        if arr.ndim == 2:
            in_specs.append(pl.BlockSpec((tm, k), lambda j, i: (i, 0)))
        else:
            in_specs.append(pl.BlockSpec((None, tm, k), lambda j, i, lead=lead: (lead, i, 0)))
        args.append(arr)
    scratch = []
    for arr, lead, off, _ in ws:
        k = arr.shape[-1] if w_t else arr.shape[-2]
        nlead = arr.ndim - 2
        lead = tuple(lead) if nlead else ()
        if w_t:
            in_specs.append(pl.BlockSpec(
                (None,) * nlead + (tn, k),
                lambda j, i, lead=lead, off=off: lead + (j + off, 0)))
        else:
            in_specs.append(pl.BlockSpec(
                (None,) * nlead + (k, tn),
                lambda j, i, lead=lead, off=off: lead + (0, j + off)))
        args.append(arr)
        scratch.append(pltpu.VMEM((k, tn), BF16))
    for arr, lead, kind, off in extras:
        if kind == "tile":
            if arr.ndim == 2:
                in_specs.append(pl.BlockSpec((tm, tn), lambda j, i, off=off: (i, j + off)))
            else:
                in_specs.append(pl.BlockSpec(
                    (None, tm, tn), lambda j, i, lead=lead, off=off: (lead, i, j + off)))
        else:
            in_specs.append(pl.BlockSpec(
                (None, 1, tn), lambda j, i, lead=lead, off=off: (lead, 0, j + off)))
        args.append(arr)
    kern = functools.partial(
        _fused_mm_kernel, n_x=len(xs), n_w=len(ws), w_x=tuple(w[3] for w in ws),
        n_e=len(extras), epilogue=epilogue, w_t=w_t)
    return pl.pallas_call(
        kern,
        grid=grid,
        in_specs=in_specs,
        out_specs=pl.BlockSpec((tm, tn), lambda j, i: (i, j)),
        out_shape=jax.ShapeDtypeStruct((m, n_out), out_dtype),
        scratch_shapes=scratch,
        compiler_params=_params(2),
        name=name,
    )(*args)


def _shifted_mm_kernel(x_ref, wa_ref, wn_ref, o_ref, wb_ref, *, shift):
    tn = wa_ref.shape[0]

    @pl.when(pl.program_id(1) == 0)
    def _():
        for c in range(tn // CAST_ROWS):
            lo = shift + c * CAST_ROWS
            hi = lo + CAST_ROWS
            if hi <= tn:
                rows = wa_ref[lo:hi, :]
            else:
                rows = jnp.concatenate([wa_ref[lo:tn, :], wn_ref[0:hi - tn, :]], axis=0)
            wb_ref[:, c * CAST_ROWS:(c + 1) * CAST_ROWS] = rows.T.astype(BF16)

    acc = jnp.dot(x_ref[...], wb_ref[...], preferred_element_type=F32)
    o_ref[...] = _sigmoid(acc).astype(o_ref.dtype)


def shifted_sigmoid_mm(x, wt, layer, row0, shift, n_out, tm, tn, name):
    m, k = x.shape
    base = row0 // tn
    return pl.pallas_call(
        functools.partial(_shifted_mm_kernel, shift=shift),
        grid=(n_out // tn, m // tm),
        in_specs=[
            pl.BlockSpec((tm, k), lambda j, i: (i, 0)),
            pl.BlockSpec((None, tn, k), lambda j, i: (layer, base + j, 0)),
            pl.BlockSpec((None, shift, k), lambda j, i: (layer, (base + j + 1) * (tn // shift), 0)),
        ],
        out_specs=pl.BlockSpec((tm, tn), lambda j, i: (i, j)),
        out_shape=jax.ShapeDtypeStruct((m, n_out), BF16),
        scratch_shapes=[pltpu.VMEM((k, tn), BF16)],
        compiler_params=_params(2),
        name=name,
    )(x, wt, wt)


def _ep_gelu(accs, extras, j):
    return _gelu_tanh(accs[0])


def _ep_sigmoid(accs, extras, j):
    return _sigmoid(accs[0])


def _ep_plain(accs, extras, j):
    return accs[0]


def _ep_merge(accs, extras, j):
    return extras[0].astype(F32) * accs[0] + extras[1].astype(F32) * accs[1]


def _ep_residual(accs, extras, j):
    return extras[0] + accs[0]


def _ep_swiglu(accs, extras, j):
    return _silu(accs[0]) * accs[1]


def _ep_ple(accs, extras, j):
    return extras[0] + _sigmoid(accs[0] + extras[1]) * accs[1]


def _sgu_kernel(u_ref, v_ref, lng_ref, lnb_ref, w_ref, bt_ref, o_ref, wm_ref):
    c = SGU_CHUNK

    @pl.when(pl.program_id(0) == 0)
    def _():
        ti = lax.broadcasted_iota(jnp.int32, (c, c), 0)
        si = lax.broadcasted_iota(jnp.int32, (c, c), 1)
        for g in range(SGU_GROUPS):
            wm_ref[g] = jnp.where(si <= ti, w_ref[g], 0.0).astype(BF16)

    v = v_ref[...].astype(F32)
    mu = jnp.mean(v, axis=-1, keepdims=True)
    vc = v - mu
    var = jnp.mean(vc * vc, axis=-1, keepdims=True)
    vln = (vc * lax.rsqrt(var + NORM_EPS) * lng_ref[...] + lnb_ref[...]).astype(BF16)
    bt = bt_ref[...]
    for g in range(SGU_GROUPS):
        sl = slice(g * c, (g + 1) * c)
        mixed = jnp.dot(wm_ref[g], vln[:, sl], preferred_element_type=F32) + bt[:, g:g + 1]
        o_ref[:, sl] = (u_ref[:, sl].astype(F32) * mixed).astype(o_ref.dtype)


def sgu(proj, ln_g, ln_b, w_s, b_s_t, layer):
    m = proj.shape[0]
    c = SGU_CHUNK
    return pl.pallas_call(
        _sgu_kernel,
        grid=(m // c,),
        in_specs=[
            pl.BlockSpec((c, SGU_WIDTH), lambda i: (i, 0)),
            pl.BlockSpec((c, SGU_WIDTH), lambda i: (i, 1)),
            pl.BlockSpec((None, 1, SGU_WIDTH), lambda i: (layer, 0, 0)),
            pl.BlockSpec((None, 1, SGU_WIDTH), lambda i: (layer, 0, 0)),
            pl.BlockSpec((None, SGU_GROUPS, c, c), lambda i: (layer, 0, 0, 0)),
            pl.BlockSpec((None, c, SGU_GROUPS), lambda i: (layer, 0, 0)),
        ],
        out_specs=pl.BlockSpec((c, SGU_WIDTH), lambda i: (i, 0)),
        out_shape=jax.ShapeDtypeStruct((m, SGU_WIDTH), BF16),
        scratch_shapes=[pltpu.VMEM((SGU_GROUPS, c, c), BF16)],
        compiler_params=_params(1),
        name="sgu",
    )(proj, proj, ln_g, ln_b, w_s, b_s_t)


def _bdot(a, b):
    return jnp.dot(a.astype(BF16), b.astype(BF16), preferred_element_type=F32)


def _bdot_nt(a, b):
    return lax.dot_general(a.astype(BF16), b.astype(BF16), (((1,), (1,)), ((), ())),
                           preferred_element_type=F32)


def _gdn_kernel(q_ref, k_ref, v_ref, z_ref, sm_ref, cw_ref, alog_ref, dtb_ref, nw_ref,
                o_ref, s_ref, prev_ref, *, nb):
    c = DN_CHUNK
    hd = DN_HEAD_DIM
    w = DN_WIDTH
    pw = 2 * hd
    n_pairs = DN_HEADS // 2

    @pl.when(pl.program_id(1) == 0)
    def _():
        s_ref[...] = jnp.zeros(s_ref.shape, F32)
        prev_ref[...] = jnp.zeros(prev_ref.shape, F32)

    row_w = lax.broadcasted_iota(jnp.int32, (c, w), 0)
    row_s = lax.broadcasted_iota(jnp.int32, (c, 128), 0)
    lane = lax.broadcasted_iota(jnp.int32, (c, 128), 1)
    left = lane < c
    jmod = jnp.where(left, lane, lane - c)
    tri = row_s >= jmod
    strict = row_s > jmod
    eye = (row_s == jmod).astype(F32)
    r256 = lax.broadcasted_iota(jnp.int32, (pw, pw), 0)
    c256 = lax.broadcasted_iota(jnp.int32, (pw, pw), 1)
    bd_mask = (r256 < hd) == (c256 < hd)
    zeros_h = jnp.zeros((c, hd), F32)
    nw = nw_ref[...]
    scale = hd ** -0.5

    def bcast2(col_a, col_b):
        return jnp.concatenate([jnp.broadcast_to(col_a, (c, hd)),
                                jnp.broadcast_to(col_b, (c, hd))], axis=1)

    def blockdiag_rows(x):
        return jnp.concatenate([jnp.where(left, x, 0.0), jnp.where(left, 0.0, x)], axis=0)

    def blockdiag_heads(x):
        return jnp.concatenate(
            [jnp.concatenate([x[:, :hd], zeros_h], axis=1),
             jnp.concatenate([zeros_h, x[:, hd:]], axis=1)], axis=0)

    def inv_norm(x):
        ss_a = jnp.sum(x[:, :hd] * x[:, :hd], axis=-1, keepdims=True)
        ss_b = jnp.sum(x[:, hd:] * x[:, hd:], axis=-1, keepdims=True)
        return bcast2(lax.rsqrt(ss_a + NORM_EPS), lax.rsqrt(ss_b + NORM_EPS))

    chains = []
    for b in range(nb):
        def conv_silu(x_ref, sec, b=b):
            x = x_ref[b].astype(F32)
            prev = prev_ref[b * 3 + sec]
            cw = cw_ref[:, sec * w:(sec + 1) * w]
            acc = cw[DN_CONV - 1:DN_CONV, :] * x
            for s in range(1, DN_CONV):
                shifted = jnp.where(row_w < s, pltpu.roll(prev, s, 0), pltpu.roll(x, s, 0))
                acc = acc + cw[DN_CONV - 1 - s:DN_CONV - s, :] * shifted
            prev_ref[b * 3 + sec] = x
            return _silu(acc)

        q_all = conv_silu(q_ref, 0)
        k_all = conv_silu(k_ref, 1)
        v_all = conv_silu(v_ref, 2)

        sm = sm_ref[b]
        beta_all = _sigmoid(sm)
        xa = sm + dtb_ref[...]
        softplus = jnp.maximum(xa, 0.0) + jnp.log(1.0 + jnp.exp(-jnp.abs(xa)))
        g = -jnp.exp(alog_ref[...]) * softplus
        s = 1
        while s < c:
            g = g + jnp.where(row_s >= s, pltpu.roll(g, s, 0), 0.0)
            s *= 2
        g_t = jnp.concatenate([g, pltpu.roll(g, 127, 1)], axis=0).T

        for p in range(n_pairs):
            a = 2 * p
            sl = slice(p * pw, (p + 1) * pw)
            q = q_all[:, sl]
            k = k_all[:, sl]
            v = v_all[:, sl]
            q = q * (inv_norm(q) * scale)
            k = k * inv_norm(k)
            gc_a = g[:, DN_HEADS + a:DN_HEADS + a + 1]
            gc_b = g[:, DN_HEADS + a + 1:DN_HEADS + a + 2]
            gl_a = gc_a[c - 1:c, :]
            gl_b = gc_b[c - 1:c, :]
            grow = g_t[DN_HEADS + a:DN_HEADS + a + 1, :]
            gcol = jnp.where(left, gc_a, gc_b)
            decay = jnp.exp(jnp.where(tri, gcol - grow, -1e30))
            beta = bcast2(beta_all[:, a:a + 1], beta_all[:, a + 1:a + 2])
            egc = bcast2(jnp.exp(gc_a), jnp.exp(gc_b))
            kb = k * beta
            chains.append(dict(
                b=b, p=p, q=q, k=k, kb=kb, vb=v * beta, kbg=kb * egc, qg=q * egc,
                kd=k * bcast2(jnp.exp(gl_a - gc_a), jnp.exp(gl_b - gc_b)),
                dl=jnp.concatenate([jnp.broadcast_to(jnp.exp(gl_a), (1, hd)),
                                    jnp.broadcast_to(jnp.exp(gl_b), (1, hd))], axis=1),
                decay=decay))

    for ch in chains:
        kq = _bdot_nt(jnp.concatenate([ch["kb"], ch["q"]], axis=0), blockdiag_heads(ch["k"]))
        ch["bk"] = -jnp.where(strict, kq[:c] * ch["decay"], 0.0)
        ch["attn"] = jnp.where(tri, kq[c:] * ch["decay"], 0.0)
        ch["qm"] = eye + ch["bk"]
    for ch in chains:
        ch["bk"] = _bdot(ch["bk"], blockdiag_rows(ch["bk"]))
    for lvl in range(1, 6):
        for ch in chains:
            if lvl < 5:
                r = _bdot(jnp.concatenate([ch["qm"], ch["bk"]], axis=0), blockdiag_rows(ch["bk"]))
                ch["qm"] = ch["qm"] + r[:c]
                ch["bk"] = r[c:]
            else:
                ch["qm"] = ch["qm"] + _bdot(ch["qm"], blockdiag_rows(ch["bk"]))
    for ch in chains:
        vb, kbg = ch["vb"], ch["kbg"]
        rhs = jnp.concatenate(
            [jnp.concatenate([vb[:, :hd], kbg[:, :hd], zeros_h, zeros_h], axis=1),
             jnp.concatenate([zeros_h, zeros_h, vb[:, hd:], kbg[:, hd:]], axis=1)], axis=0)
        sol = _bdot(ch["qm"], rhs)
        ch["u"] = jnp.concatenate([sol[:, :hd], sol[:, 2 * hd:3 * hd]], axis=1)
        ch["w"] = jnp.concatenate([sol[:, hd:2 * hd], sol[:, 3 * hd:]], axis=1)
    for ch in chains:
        ch["state"] = s_ref[ch["b"] * n_pairs + ch["p"]]
        r2 = _bdot(jnp.concatenate([ch["w"], ch["qg"]], axis=0), ch["state"])
        ch["vn"] = ch["u"] - r2[:c]
        ch["o"] = r2[c:]
    for ch in chains:
        ch["o"] = ch["o"] + _bdot(ch["attn"], blockdiag_heads(ch["vn"]))
    for ch in chains:
        upd = ch["state"] * ch["dl"] + _bdot(ch["kd"].T, ch["vn"])
        s_ref[ch["b"] * n_pairs + ch["p"]] = jnp.where(bd_mask, upd, 0.0)
    for ch in chains:
        b, p = ch["b"], ch["p"]
        for half in range(2):
            o = ch["o"][:, half * hd:(half + 1) * hd]
            sl = slice(p * pw + half * hd, p * pw + (half + 1) * hd)
            o = o * lax.rsqrt(jnp.mean(o * o, axis=-1, keepdims=True) + NORM_EPS)
            o = o * nw * _silu(z_ref[b, :, sl].astype(F32))
            o_ref[b, :, sl] = o.astype(o_ref.dtype)


def gdn(proj, small, conv_w, alog_row, dtb_row, norm_w, layer, bsz, seq, nb=1):
    m = proj.shape[0]
    c = DN_CHUNK
    n = seq // c
    w = DN_WIDTH
    qkv0 = 0
    proj3 = proj.reshape(bsz, seq, proj.shape[-1])
    small3 = small.reshape(bsz, seq, small.shape[-1])
    pw = 2 * DN_HEAD_DIM
    out = pl.pallas_call(
        functools.partial(_gdn_kernel, nb=nb),
        grid=(bsz // nb, n),
        in_specs=[
            pl.BlockSpec((nb, c, w), lambda b, t: (b, t, qkv0)),
            pl.BlockSpec((nb, c, w), lambda b, t: (b, t, qkv0 + 1)),
            pl.BlockSpec((nb, c, w), lambda b, t: (b, t, qkv0 + 2)),
            pl.BlockSpec((nb, c, w), lambda b, t: (b, t, qkv0 + 3)),
            pl.BlockSpec((nb, c, 128), lambda b, t: (b, t, 0)),
            pl.BlockSpec((None, DN_CONV, 3 * w), lambda b, t: (layer, 0, 0)),
            pl.BlockSpec((None, 1, 128), lambda b, t: (layer, 0, 0)),
            pl.BlockSpec((None, 1, 128), lambda b, t: (layer, 0, 0)),
            pl.BlockSpec((None, 1, DN_HEAD_DIM), lambda b, t: (layer, 0, 0)),
        ],
        out_specs=pl.BlockSpec((nb, c, w), lambda b, t: (b, t, 0)),
        out_shape=jax.ShapeDtypeStruct((bsz, seq, w), BF16),
        scratch_shapes=[pltpu.VMEM((nb * (DN_HEADS // 2), pw, pw), F32),
                        pltpu.VMEM((nb * 3, c, w), F32)],
        compiler_params=_params(2),
        name="gdn",
    )(proj3, proj3, proj3, proj3, small3, conv_w, alog_row, dtb_row, norm_w)
    return out.reshape(m, w)


def _route_kernel(h_ref, g_ref, r_ref, hn_ref, ti_ref, tw_ref):
    x = h_ref[...]
    ms = jnp.mean(x * x, axis=-1, keepdims=True)
    hn = x * lax.rsqrt(ms + NORM_EPS) * g_ref[...]
    hn_ref[...] = hn
    logits = jnp.dot(hn, r_ref[...], preferred_element_type=F32,
                     precision=lax.Precision.HIGHEST)
    lane = lax.broadcasted_iota(jnp.int32, logits.shape, 1)
    neg = jnp.float32(-3.0e38)
    logits = jnp.where(lane < N_EXPERTS, logits, neg)
    m1 = jnp.max(logits, axis=-1, keepdims=True)
    i1 = jnp.min(jnp.where(logits == m1, lane, 128), axis=-1, keepdims=True)
    rest = jnp.where(lane == i1, neg, logits)
    m2 = jnp.max(rest, axis=-1, keepdims=True)
    i2 = jnp.min(jnp.where(rest == m2, lane, 128), axis=-1, keepdims=True)
    e2 = jnp.exp(m2 - m1)
    w1 = 1.0 / (1.0 + e2)
    w2 = e2 / (1.0 + e2)
    ti_ref[...] = jnp.where(lane == 0, i1, jnp.where(lane == 1, i2, 0))
    tw_ref[...] = jnp.where(lane == 0, w1, jnp.where(lane == 1, w2, 0.0))


def route(h, g, layer, router_pad, j, tm=256):
    m, d = h.shape
    g = g.reshape(g.shape[0], 1, d)
    return pl.pallas_call(
        _route_kernel,
        grid=(m // tm,),
        in_specs=[pl.BlockSpec((tm, d), lambda i: (i, 0)),
                  pl.BlockSpec((None, 1, d), lambda i: (layer, 0, 0)),
                  pl.BlockSpec((None, d, 128), lambda i: (j, 0, 0))],
        out_specs=[pl.BlockSpec((tm, d), lambda i: (i, 0)),
                   pl.BlockSpec((tm, 128), lambda i: (i, 0)),
                   pl.BlockSpec((tm, 128), lambda i: (i, 0))],
        out_shape=[jax.ShapeDtypeStruct((m, d), F32),
                   jax.ShapeDtypeStruct((m, 128), jnp.int32),
                   jax.ShapeDtypeStruct((m, 128), F32)],
        compiler_params=_params(1),
        name="route",
    )(h, g, router_pad)


def _row_copy(src_ref, src_row, dst_ref, dst_row, sem):
    return pltpu.make_async_copy(src_ref.at[pl.ds(src_row, 1)], dst_ref.at[pl.ds(dst_row, 1)], sem)


def _moe_gather_kernel(pos_ref, tstart_ref, tend_ref, hn_ref, xs_ref, fill_sem, row_sem):
    step = pl.program_id(0)
    tm = MOE_TM

    n_tiles = xs_ref.shape[0] // tm
    n_active = tend_ref[N_EXPERTS - 1]

    def fill_tile(t):
        dst = pl.multiple_of(t * tm, tm)
        return pltpu.make_async_copy(hn_ref, xs_ref.at[pl.ds(dst, tm)], fill_sem)

    def fills(act):
        for e in range(N_EXPERTS):
            @pl.when(tend_ref[e] > tstart_ref[e])
            def _():
                act(fill_tile(tend_ref[e] - 1))
        for t in range(N_EXPERTS):
            @pl.when(n_active + t < n_tiles)
            def _():
                act(fill_tile(n_active + t))

    @pl.when(step == 0)
    def _():
        fills(lambda cp: cp.start())
        fills(lambda cp: cp.wait())

    base = step * tm * TOP_K

    def issue(t, carry):
        for slot in range(TOP_K):
            _row_copy(hn_ref, t, xs_ref, pos_ref[base + t * TOP_K + slot], row_sem).start(
                priority=slot)
        return carry

    lax.fori_loop(0, tm, issue, 0, unroll=4)

    def drain(i, carry):
        _row_copy(hn_ref, 0, xs_ref, 0, row_sem).wait()
        return carry

    lax.fori_loop(0, tm * TOP_K, drain, 0)


def moe_gather(hn, pos, tile_start, tile_end, rows):
    m, d = hn.shape
    tm = MOE_TM
    return pl.pallas_call(
        _moe_gather_kernel,
        grid_spec=pltpu.PrefetchScalarGridSpec(
            num_scalar_prefetch=3,
            grid=(m // tm,),
            in_specs=[pl.BlockSpec((tm, d), lambda i, pos, ts, te: (i, 0))],
            out_specs=pl.BlockSpec(memory_space=pl.ANY),
            scratch_shapes=[pltpu.SemaphoreType.DMA(()), pltpu.SemaphoreType.DMA(())],
        ),
        out_shape=jax.ShapeDtypeStruct((rows, d), F32),
        compiler_params=_params(1),
        name="moe_gather",
    )(pos, tile_start, tile_end, hn)


def _moe_group_kernel(ts_ref, te_ref, x_ref, *refs, n_w, compute, mults, k_split):
    w_refs = refs[:n_w]
    o_ref = refs[n_w]
    wb_refs = refs[n_w + 1:2 * n_w + 1]
    xbuf, obuf, xsem, osem = refs[2 * n_w + 1:]
    tm = MOE_TM
    tn = obuf.shape[2]
    n_tiles = o_ref.shape[0] // tm
    e = pl.program_id(1)
    kh = pl.program_id(2)
    col = pl.multiple_of(pl.program_id(0) * tn, tn)
    t0 = ts_ref[e]
    n = te_ref[e] - t0
    kblk = w_refs[0].shape[0]

    @pl.when(n > 0)
    def _():
        for w_ref, wb_ref in zip(w_refs, wb_refs):
            if k_split == 1:
                _cast_rows(w_ref, wb_ref)
            else:
                _cast_rows(w_ref, wb_ref.at[pl.ds(pl.multiple_of(kh * kblk, kblk), kblk)])

    big = mults[0]
    n_big = n // big
    rem = n - n_big * big

    def step_info(s):
        mult = jnp.where(s < n_big, big, 0)
        off = s * big
        idx = n_big
        first = n_big * big
        for mlt in mults[1:]:
            present = ((rem // mlt) % 2).astype(jnp.int32)
            here = jnp.logical_and(present == 1, s == idx)
            mult = jnp.where(here, mlt, mult)
            off = jnp.where(here, first, off)
            idx = idx + present
            first = first + present * mlt
        return mult, off

    n_steps = n_big
    for mlt in mults[1:]:
        n_steps = n_steps + (rem // mlt) % 2

    def for_mult(mult, fn):
        for mlt in mults:
            @pl.when(mult == mlt)
            def _():
                fn(mlt)

    def x_copy(off, mlt, slot):
        row = pl.multiple_of((t0 + off) * tm, tm)
        return pltpu.make_async_copy(x_ref.at[pl.ds(row, mlt * tm)],
                                     xbuf.at[slot, pl.ds(0, mlt * tm)], xsem.at[slot])

    def o_copy(tile, mlt, slot):
        row = pl.multiple_of(tile * tm, tm)
        return pltpu.make_async_copy(
            obuf.at[slot, pl.ds(0, mlt * tm)],
            o_ref.at[pl.ds(row, mlt * tm), pl.ds(col, tn)], osem.at[slot])

    @pl.when(jnp.logical_and(n > 0, kh == k_split - 1))
    def _():
        m0, off0 = step_info(0)
        for_mult(m0, lambda mlt: x_copy(off0, mlt, 0).start(priority=1))

        def body(s, carry):
            slot = lax.rem(s, 2)
            ms, offs = step_info(s)
            for_mult(ms, lambda mlt: x_copy(offs, mlt, slot).wait())

            @pl.when(s + 1 < n_steps)
            def _():
                mn, offn = step_info(s + 1)
                for_mult(mn, lambda mlt: x_copy(offn, mlt, 1 - slot).start(priority=1))

            @pl.when(s >= 2)
            def _():
                mp, offp = step_info(s - 2)
                for_mult(mp, lambda mlt: o_copy(t0 + offp, mlt, slot).wait())

            def run(mlt):
                rows = mlt * tm
                obuf[slot, :rows] = compute(
                    xbuf[slot, :rows], [wb[...] for wb in wb_refs]).astype(obuf.dtype)
                o_copy(t0 + offs, mlt, slot).start()

            for_mult(ms, run)
            return carry

        lax.fori_loop(0, n_steps, body, 0)

        @pl.when(n_steps >= 2)
        def _():
            mp, offp = step_info(n_steps - 2)
            for_mult(mp, lambda mlt: o_copy(t0 + offp, mlt, lax.rem(n_steps, 2)).wait())

        ml, offl = step_info(n_steps - 1)
        for_mult(ml, lambda mlt: o_copy(t0 + offl, mlt, lax.rem(n_steps - 1, 2)).wait())

    @pl.when(jnp.logical_and(e == N_EXPERTS - 1, kh == k_split - 1))
    def _():
        n_active = te_ref[N_EXPERTS - 1]
        obuf[0, :tm] = jnp.zeros((tm, tn), obuf.dtype)
        for k in range(N_EXPERTS):
            @pl.when(n_active + k < n_tiles)
            def _():
                o_copy(n_active + k, 1, 0).start()
        for k in range(N_EXPERTS):
            @pl.when(n_active + k < n_tiles)
            def _():
                o_copy(n_active + k, 1, 0).wait()


def _moe_up_compute(x, wbs):
    xb = x.astype(BF16)
    a = jnp.dot(xb, wbs[0], preferred_element_type=F32)
    b = jnp.dot(xb, wbs[1], preferred_element_type=F32)
    return _silu(a) * b


def _moe_down_compute(x, wbs):
    return jnp.dot(x, wbs[0], preferred_element_type=F32)


def moe_group_mm(x, ws, j, tile_start, tile_end, compute, n_out, out_dtype, tn, mults, k_split,
                 name):
    rows, k = x.shape
    tm = MOE_TM
    w_specs = [pl.BlockSpec((None, None, k // k_split, tn),
                            lambda c, e, kh, ts, te: (j, e, kh, c)) for _ in ws]
    return pl.pallas_call(
        functools.partial(_moe_group_kernel, n_w=len(ws), compute=compute, mults=mults,
                          k_split=k_split),
        grid_spec=pltpu.PrefetchScalarGridSpec(
            num_scalar_prefetch=2,
            grid=(n_out // tn, N_EXPERTS, k_split),
            in_specs=[pl.BlockSpec(memory_space=pl.ANY)] + w_specs,
            out_specs=pl.BlockSpec(memory_space=pl.ANY),
            scratch_shapes=[pltpu.VMEM((k, tn), BF16) for _ in ws] + [
                pltpu.VMEM((2, mults[0] * tm, k), x.dtype),
                pltpu.VMEM((2, mults[0] * tm, tn), out_dtype),
                pltpu.SemaphoreType.DMA((2,)),
                pltpu.SemaphoreType.DMA((2,)),
            ],
        ),
        out_shape=jax.ShapeDtypeStruct((rows, n_out), out_dtype),
        compiler_params=_params(3),
        name=name,
    )(tile_start, tile_end, x, *ws)


def moe_experts(xs, tile_start, tile_end, w_gate, w_up, w_down, j):
    d = xs.shape[1]
    f = w_gate.shape[-1]
    hmid = moe_group_mm(xs, [w_gate, w_up], j, tile_start, tile_end, _moe_up_compute,
                        f, BF16, 1024, (2, 1), 2, "moe_up")
    return moe_group_mm(hmid, [w_down], j, tile_start, tile_end, _moe_down_compute,
                        d, F32, 512, (2, 1), 2, "moe_down")


def _moe_combine_kernel(pos_ref, ys_ref, h_ref, tw_ref, g_ref, ho_ref, hn_ref, buf_ref, sem):
    tm = MOE_COMBINE_TM
    base = pl.program_id(0) * tm * TOP_K

    def issue(i, carry):
        for slot in range(TOP_K):
            _row_copy(ys_ref, pos_ref[base + i * TOP_K + slot], buf_ref.at[slot], i, sem).start(
                priority=slot)
        return carry

    lax.fori_loop(0, tm, issue, 0, unroll=4)

    def drain(i, carry):
        _row_copy(ys_ref, 0, buf_ref.at[0], 0, sem).wait()
        return carry

    lax.fori_loop(0, tm * TOP_K, drain, 0)

    tw = tw_ref[...]
    x = h_ref[...] + tw[:, 0:1] * buf_ref[0] + tw[:, 1:2] * buf_ref[1]
    ho_ref[...] = x
    ms = jnp.mean(x * x, axis=-1, keepdims=True)
    hn_ref[...] = (x * lax.rsqrt(ms + NORM_EPS) * g_ref[...]).astype(hn_ref.dtype)


def moe_combine(ys, pos, h, tw, norm_g, layer):
    m, d = h.shape
    tm = MOE_COMBINE_TM
    norm_g = norm_g.reshape(norm_g.shape[0], 1, d)
    return pl.pallas_call(
        _moe_combine_kernel,
        grid_spec=pltpu.PrefetchScalarGridSpec(
            num_scalar_prefetch=1,
            grid=(m // tm,),
            in_specs=[
                pl.BlockSpec(memory_space=pl.ANY),
                pl.BlockSpec((tm, d), lambda i, pos: (i, 0)),
                pl.BlockSpec((tm, 128), lambda i, pos: (i, 0)),
                pl.BlockSpec((None, 1, d), lambda i, pos: (layer, 0, 0)),
            ],
            out_specs=[pl.BlockSpec((tm, d), lambda i, pos: (i, 0)),
                       pl.BlockSpec((tm, d), lambda i, pos: (i, 0))],
            scratch_shapes=[pltpu.VMEM((TOP_K, tm, d), F32), pltpu.SemaphoreType.DMA(())],
        ),
        out_shape=[jax.ShapeDtypeStruct((m, d), F32), jax.ShapeDtypeStruct((m, d), BF16)],
        compiler_params=_params(1),
        name="moe_combine",
    )(pos, ys, h, tw, norm_g)


def moe_layer(h, norm_g, layer, router_pad, w_gate, w_up, w_down, j, next_norm_g):
    m, d = h.shape
    tm = MOE_TM
    hn, ti, tw = route(h, norm_g, layer, router_pad, j)
    top_i = ti[:, :TOP_K].reshape(-1)
    n_pairs = m * TOP_K
    n_tiles = n_pairs // tm + N_EXPERTS
    onehot = (top_i[:, None] == jnp.arange(N_EXPERTS, dtype=jnp.int32)[None, :]).astype(jnp.int32)
    csum = jnp.cumsum(onehot, axis=0)
    rank = jnp.sum((csum - onehot) * onehot, axis=1)
    counts = csum[-1]
    tiles_per = (counts + tm - 1) // tm
    tile_end = jnp.cumsum(tiles_per).astype(jnp.int32)
    tile_start = (tile_end - tiles_per).astype(jnp.int32)
    pos = (jnp.sum(onehot * tile_start[None, :], axis=1) * tm + rank).astype(jnp.int32)

    xs = moe_gather(hn, pos, tile_start, tile_end, n_tiles * tm)
    ys = moe_experts(xs, tile_start, tile_end, w_gate, w_up, w_down, j)
    return moe_combine(ys, pos, h, tw, next_norm_g, layer)


def kernel(x, p, norm_mix, w_in, sgu_ln_g, sgu_ln_b, sgu_w, sgu_b, dn_conv_w, dn_a_log,
           dn_dt_bias, dn_norm_w, w_branch, w_out, norm_ffn, ffn_w_gate, ffn_w_up,
           ffn_w_down, moe_router, moe_w_gate, moe_w_up, moe_w_down, norm_ple,
           ple_w_gate, ple_b_gate, ple_w_proj, norm_final):
    bsz, seq, d = x.shape
    depth = p.shape[0]
    m = bsz * seq
    h = x.reshape(m, d)
    p2 = p.reshape(depth, m, p.shape[-1])
    main_w = 2 * SGU_WIDTH + 4 * DN_WIDTH
    n_small = 2 * DN_HEADS
    w_in_t = jnp.swapaxes(w_in, 1, 2)
    pad_h = 128 - n_small
    alog_row = jnp.pad(dn_a_log, ((0, 0), (DN_HEADS, pad_h))).reshape(depth, 1, 128)
    dtb_row = jnp.pad(dn_dt_bias, ((0, 0), (DN_HEADS, pad_h))).reshape(depth, 1, 128)
    ln_g = sgu_ln_g.reshape(depth, 1, SGU_WIDTH)
    ln_b = sgu_ln_b.reshape(depth, 1, SGU_WIDTH)
    sgu_b_t = jnp.swapaxes(sgu_b, 1, 2)
    nw = dn_norm_w.reshape(depth, 1, DN_HEAD_DIM)
    router_pad = jnp.pad(moe_router, ((0, 0), (0, 0), (0, 128 - N_EXPERTS)))
    ple_b = ple_b_gate.reshape(depth, 1, d)

    for i in range(depth):
        hn = rmsnorm(h, norm_mix, i, BF16)
        uv = fused_mm([(hn, 0)], [(w_in_t, (i,), 0, 0)], [], _ep_gelu, 2 * SGU_WIDTH, BF16,
                      1024, 1024, "proj_uv", w_t=True)
        qkvz = fused_mm([(hn, 0)], [(w_in_t, (i,), (2 * SGU_WIDTH) // 1024, 0)], [], _ep_plain,
                        4 * DN_WIDTH, BF16, 1024, 1024, "proj_qkvz", w_t=True)
        gates = shifted_sigmoid_mm(hn, w_in_t, i, main_w, n_small, 2 * d, 1024, 1024, "proj_gates")
        small = fused_mm([(hn, 0)], [(w_in_t, (i,), main_w // 128, 0)], [], _ep_plain, 128, F32,
                         1024, 128, "proj_small", w_t=True)
        y_a = sgu(uv, ln_g, ln_b, sgu_w, sgu_b_t, i)
        y_b = gdn(qkvz, small, dn_conv_w, alog_row, dtb_row, nw, i, bsz, seq, nb=2)
        merged = fused_mm([(y_a, 0), (y_b, 0)],
                          [(w_branch, (i, 0), 0, 0), (w_branch, (i, 1), 0, 1)],
                          [(gates, 0, "tile", 0), (gates, 0, "tile", d // 512)],
                          _ep_merge, d, BF16, 1024, 512, "merge")
        h = fused_mm([(merged, 0)], [(w_out, (i,), 0, 0)], [(h, 0, "tile", 0)],
                     _ep_residual, d, F32, 1024, 512, "out_proj")
        j = i // 2
        if i % 2 == 0:
            hn = rmsnorm(h, norm_ffn, i, BF16)
            hmid = fused_mm([(hn, 0)], [(ffn_w_gate, (j,), 0, 0), (ffn_w_up, (j,), 0, 0)], [],
                            _ep_swiglu, ffn_w_gate.shape[-1], BF16, 1024, 512, "ffn_up")
            h = fused_mm([(hmid, 0)], [(ffn_w_down, (j,), 0, 0)], [(h, 0, "tile", 0)],
                         _ep_residual, d, F32, 512, 512, "ffn_down")
            hn = rmsnorm(h, norm_ple, i, BF16)
        else:
            h, hn = moe_layer(h, norm_ffn, i, router_pad, moe_w_gate, moe_w_up, moe_w_down, j,
                              norm_ple)
        h = fused_mm([(hn, 0), (p2, i)],
                     [(ple_w_gate, (i,), 0, 0), (ple_w_proj, (i,), 0, 1)],
                     [(h, 0, "tile", 0), (ple_b, i, "row", 0)],
                     _ep_ple, d, F32, 1024, 512, "ple")
    out = rmsnorm(h, norm_final, 0, F32)
    return out.reshape(bsz, seq, d)
```

```python
import functools
import math

import jax
import jax.numpy as jnp
from jax import lax
from jax.experimental import pallas as pl
from jax.experimental.pallas import tpu as pltpu

F32 = jnp.float32
BF16 = jnp.bfloat16

D_MODEL = 2048
SGU_CHUNK = 128
SGU_GROUPS = 8
SGU_WIDTH = 1024
DN_HEADS = 8
DN_HEAD_DIM = 128
DN_WIDTH = 1024
DN_CONV = 4
DN_CHUNK = 64
N_EXPERTS = 8
TOP_K = 2
NORM_EPS = 1e-6

V7X_VMEM_LIMIT = 56 * 1024 * 1024
CAST_ROWS = 256
MOE_TM = 256
MOE_COMBINE_TM = 256


def _params(n_axes):
    return pltpu.CompilerParams(
        dimension_semantics=("arbitrary",) * n_axes,
        vmem_limit_bytes=V7X_VMEM_LIMIT)


def _sigmoid(x):
    return 1.0 / (1.0 + jnp.exp(-x))


def _silu(x):
    return x * _sigmoid(x)


def _gelu_tanh(x):
    c = math.sqrt(2.0 / math.pi)
    return 0.5 * x * (1.0 + jnp.tanh(c * (x + 0.044715 * (x * x * x))))


def _cast_rows(w_ref, wb_ref):
    k = w_ref.shape[0]
    ch = min(CAST_ROWS, k)

    def body(c, carry):
        r = pl.multiple_of(c * ch, ch)
        wb_ref[pl.ds(r, ch), :] = w_ref[pl.ds(r, ch), :].astype(BF16)
        return carry

    lax.fori_loop(0, k // ch, body, 0)


def _cast_rows_transposed(wt_ref, wb_ref):
    tn = wt_ref.shape[0]
    ch = min(CAST_ROWS, tn)
    for c in range(tn // ch):
        wb_ref[:, c * ch:(c + 1) * ch] = wt_ref[c * ch:(c + 1) * ch, :].T.astype(BF16)


def _rmsnorm_kernel(h_ref, g_ref, o_ref):
    x = h_ref[...]
    ms = jnp.mean(x * x, axis=-1, keepdims=True)
    o_ref[...] = (x * lax.rsqrt(ms + NORM_EPS) * g_ref[...]).astype(o_ref.dtype)


def rmsnorm(h, g, layer, out_dtype, tm=512):
    m, d = h.shape
    if g.ndim == 1:
        g = g.reshape(1, 1, d)
        layer = 0
    else:
        g = g.reshape(g.shape[0], 1, d)
    return pl.pallas_call(
        _rmsnorm_kernel,
        grid=(m // tm,),
        in_specs=[pl.BlockSpec((tm, d), lambda i: (i, 0)),
                  pl.BlockSpec((None, 1, d), lambda i: (layer, 0, 0))],
        out_specs=pl.BlockSpec((tm, d), lambda i: (i, 0)),
        out_shape=jax.ShapeDtypeStruct((m, d), out_dtype),
        compiler_params=_params(1),
        name="rmsnorm",
    )(h, g)


def _fused_mm_kernel(*refs, n_x, n_w, w_x, n_e, epilogue, w_t):
    x_refs = refs[:n_x]
    w_refs = refs[n_x:n_x + n_w]
    e_refs = refs[n_x + n_w:n_x + n_w + n_e]
    o_ref = refs[n_x + n_w + n_e]
    wb_refs = refs[n_x + n_w + n_e + 1:]

    @pl.when(pl.program_id(1) == 0)
    def _():
        for w_ref, wb_ref in zip(w_refs, wb_refs):
            (_cast_rows_transposed if w_t else _cast_rows)(w_ref, wb_ref)

    xs = [x_ref[...].astype(BF16) for x_ref in x_refs]
    accs = [jnp.dot(xs[w_x[l]], wb_refs[l][...], preferred_element_type=F32)
            for l in range(n_w)]
    o_ref[...] = epilogue(accs, [e[...] for e in e_refs], pl.program_id(0)).astype(o_ref.dtype)


def fused_mm(xs, ws, extras, epilogue, n_out, out_dtype, tm, tn, name, w_t=False):
    m = xs[0][0].shape[-2]
    grid = (n_out // tn, m // tm)
    in_specs, args = [], []
    for arr, lead in xs:
        k = arr.shape[-1]
        if arr.ndim == 2:
            in_specs.append(pl.BlockSpec((tm, k), lambda j, i: (i, 0)))
        else:
            in_specs.append(pl.BlockSpec((None, tm, k), lambda j, i, lead=lead: (lead, i, 0)))
        args.append(arr)
    scratch = []
    for arr, lead, off, _ in ws:
        k = arr.shape[-1] if w_t else arr.shape[-2]
        nlead = arr.ndim - 2
        lead = tuple(lead) if nlead else ()
        if w_t:
            in_specs.append(pl.BlockSpec(
                (None,) * nlead + (tn, k),
                lambda j, i, lead=lead, off=off: lead + (j + off, 0)))
        else:
            in_specs.append(pl.BlockSpec(
                (None,) * nlead + (k, tn),
                lambda j, i, lead=lead, off=off: lead + (0, j + off)))
        args.append(arr)
        scratch.append(pltpu.VMEM((k, tn), BF16))
    for arr, lead, kind, off in extras:
        if kind == "tile":
            if arr.ndim == 2:
                in_specs.append(pl.BlockSpec((tm, tn), lambda j, i, off=off: (i, j + off)))
            else:
                in_specs.append(pl.BlockSpec(
                    (None, tm, tn), lambda j, i, lead=lead, off=off: (lead, i, j + off)))
        else:
            in_specs.append(pl.BlockSpec(
                (None, 1, tn), lambda j, i, lead=lead, off=off: (lead, 0, j + off)))
        args.append(arr)
    kern = functools.partial(
        _fused_mm_kernel, n_x=len(xs), n_w=len(ws), w_x=tuple(w[3] for w in ws),
        n_e=len(extras), epilogue=epilogue, w_t=w_t)
    return pl.pallas_call(
        kern,
        grid=grid,
        in_specs=in_specs,
        out_specs=pl.BlockSpec((tm, tn), lambda j, i: (i, j)),
        out_shape=jax.ShapeDtypeStruct((m, n_out), out_dtype),
        scratch_shapes=scratch,
        compiler_params=_params(2),
        name=name,
    )(*args)


def _shifted_mm_kernel(x_ref, wa_ref, wn_ref, o_ref, wb_ref, *, shift):
    tn = wa_ref.shape[0]

    @pl.when(pl.program_id(1) == 0)
    def _():
        for c in range(tn // CAST_ROWS):
            lo = shift + c * CAST_ROWS
            hi = lo + CAST_ROWS
            if hi <= tn:
                rows = wa_ref[lo:hi, :]
            else:
                rows = jnp.concatenate([wa_ref[lo:tn, :], wn_ref[0:hi - tn, :]], axis=0)
            wb_ref[:, c * CAST_ROWS:(c + 1) * CAST_ROWS] = rows.T.astype(BF16)

    acc = jnp.dot(x_ref[...], wb_ref[...], preferred_element_type=F32)
    o_ref[...] = _sigmoid(acc).astype(o_ref.dtype)


def shifted_sigmoid_mm(x, wt, layer, row0, shift, n_out, tm, tn, name):
    m, k = x.shape
    base = row0 // tn
    return pl.pallas_call(
        functools.partial(_shifted_mm_kernel, shift=shift),
        grid=(n_out // tn, m // tm),
        in_specs=[
            pl.BlockSpec((tm, k), lambda j, i: (i, 0)),
            pl.BlockSpec((None, tn, k), lambda j, i: (layer, base + j, 0)),
            pl.BlockSpec((None, shift, k), lambda j, i: (layer, (base + j + 1) * (tn // shift), 0)),
        ],
        out_specs=pl.BlockSpec((tm, tn), lambda j, i: (i, j)),
        out_shape=jax.ShapeDtypeStruct((m, n_out), BF16),
        scratch_shapes=[pltpu.VMEM((k, tn), BF16)],
        compiler_params=_params(2),
        name=name,
    )(x, wt, wt)


def _ep_gelu(accs, extras, j):
    return _gelu_tanh(accs[0])


def _ep_sigmoid(accs, extras, j):
    return _sigmoid(accs[0])


def _ep_plain(accs, extras, j):
    return accs[0]


def _ep_merge(accs, extras, j):
    return extras[0].astype(F32) * accs[0] + extras[1].astype(F32) * accs[1]


def _ep_residual(accs, extras, j):
    return extras[0] + accs[0]


def _ep_swiglu(accs, extras, j):
    return _silu(accs[0]) * accs[1]


def _ep_ple(accs, extras, j):
    return extras[0] + _sigmoid(accs[0] + extras[1]) * accs[1]


def _sgu_kernel(u_ref, v_ref, lng_ref, lnb_ref, w_ref, bt_ref, o_ref, wm_ref):
    c = SGU_CHUNK

    @pl.when(pl.program_id(0) == 0)
    def _():
        ti = lax.broadcasted_iota(jnp.int32, (c, c), 0)
        si = lax.broadcasted_iota(jnp.int32, (c, c), 1)
        for g in range(SGU_GROUPS):
            wm_ref[g] = jnp.where(si <= ti, w_ref[g], 0.0).astype(BF16)

    v = v_ref[...].astype(F32)
    mu = jnp.mean(v, axis=-1, keepdims=True)
    vc = v - mu
    var = jnp.mean(vc * vc, axis=-1, keepdims=True)
    vln = (vc * lax.rsqrt(var + NORM_EPS) * lng_ref[...] + lnb_ref[...]).astype(BF16)
    bt = bt_ref[...]
    for g in range(SGU_GROUPS):
        sl = slice(g * c, (g + 1) * c)
        mixed = jnp.dot(wm_ref[g], vln[:, sl], preferred_element_type=F32) + bt[:, g:g + 1]
        o_ref[:, sl] = (u_ref[:, sl].astype(F32) * mixed).astype(o_ref.dtype)


def sgu(proj, ln_g, ln_b, w_s, b_s_t, layer):
    m = proj.shape[0]
    c = SGU_CHUNK
    return pl.pallas_call(
        _sgu_kernel,
        grid=(m // c,),
        in_specs=[
            pl.BlockSpec((c, SGU_WIDTH), lambda i: (i, 0)),
            pl.BlockSpec((c, SGU_WIDTH), lambda i: (i, 1)),
            pl.BlockSpec((None, 1, SGU_WIDTH), lambda i: (layer, 0, 0)),
            pl.BlockSpec((None, 1, SGU_WIDTH), lambda i: (layer, 0, 0)),
            pl.BlockSpec((None, SGU_GROUPS, c, c), lambda i: (layer, 0, 0, 0)),
            pl.BlockSpec((None, c, SGU_GROUPS), lambda i: (layer, 0, 0)),
        ],
        out_specs=pl.BlockSpec((c, SGU_WIDTH), lambda i: (i, 0)),
        out_shape=jax.ShapeDtypeStruct((m, SGU_WIDTH), BF16),
        scratch_shapes=[pltpu.VMEM((SGU_GROUPS, c, c), BF16)],
        compiler_params=_params(1),
        name="sgu",
    )(proj, proj, ln_g, ln_b, w_s, b_s_t)


def _bdot(a, b):
    return jnp.dot(a.astype(BF16), b.astype(BF16), preferred_element_type=F32)


def _bdot_nt(a, b):
    return lax.dot_general(a.astype(BF16), b.astype(BF16), (((1,), (1,)), ((), ())),
                           preferred_element_type=F32)


def _gdn_kernel(q_ref, k_ref, v_ref, z_ref, sm_ref, cw_ref, alog_ref, dtb_ref, nw_ref,
                o_ref, s_ref, prev_ref, *, nb):
    c = DN_CHUNK
    hd = DN_HEAD_DIM
    w = DN_WIDTH
    pw = 2 * hd
    n_pairs = DN_HEADS // 2

    @pl.when(pl.program_id(1) == 0)
    def _():
        s_ref[...] = jnp.zeros(s_ref.shape, F32)
        prev_ref[...] = jnp.zeros(prev_ref.shape, F32)

    row_w = lax.broadcasted_iota(jnp.int32, (c, w), 0)
    row_s = lax.broadcasted_iota(jnp.int32, (c, 128), 0)
    lane = lax.broadcasted_iota(jnp.int32, (c, 128), 1)
    left = lane < c
    jmod = jnp.where(left, lane, lane - c)
    tri = row_s >= jmod
    strict = row_s > jmod
    eye = (row_s == jmod).astype(F32)
    r256 = lax.broadcasted_iota(jnp.int32, (pw, pw), 0)
    c256 = lax.broadcasted_iota(jnp.int32, (pw, pw), 1)
    bd_mask = (r256 < hd) == (c256 < hd)
    zeros_h = jnp.zeros((c, hd), F32)
    nw = nw_ref[...]
    scale = hd ** -0.5

    def bcast2(col_a, col_b):
        return jnp.concatenate([jnp.broadcast_to(col_a, (c, hd)),
                                jnp.broadcast_to(col_b, (c, hd))], axis=1)

    def blockdiag_rows(x):
        return jnp.concatenate([jnp.where(left, x, 0.0), jnp.where(left, 0.0, x)], axis=0)

    def blockdiag_heads(x):
        return jnp.concatenate(
            [jnp.concatenate([x[:, :hd], zeros_h], axis=1),
             jnp.concatenate([zeros_h, x[:, hd:]], axis=1)], axis=0)

    def inv_norm(x):
        ss_a = jnp.sum(x[:, :hd] * x[:, :hd], axis=-1, keepdims=True)
        ss_b = jnp.sum(x[:, hd:] * x[:, hd:], axis=-1, keepdims=True)
        return bcast2(lax.rsqrt(ss_a + NORM_EPS), lax.rsqrt(ss_b + NORM_EPS))

    chains = []
    for b in range(nb):
        def conv_silu(x_ref, sec, b=b):
            x = x_ref[b].astype(F32)
            prev = prev_ref[b * 3 + sec]
            cw = cw_ref[:, sec * w:(sec + 1) * w]
            acc = cw[DN_CONV - 1:DN_CONV, :] * x
            for s in range(1, DN_CONV):
                shifted = jnp.where(row_w < s, pltpu.roll(prev, s, 0), pltpu.roll(x, s, 0))
                acc = acc + cw[DN_CONV - 1 - s:DN_CONV - s, :] * shifted
            prev_ref[b * 3 + sec] = x
            return _silu(acc)

        q_all = conv_silu(q_ref, 0)
        k_all = conv_silu(k_ref, 1)
        v_all = conv_silu(v_ref, 2)

        sm = sm_ref[b]
        beta_all = _sigmoid(sm)
        xa = sm + dtb_ref[...]
        softplus = jnp.maximum(xa, 0.0) + jnp.log(1.0 + jnp.exp(-jnp.abs(xa)))
        g = -jnp.exp(alog_ref[...]) * softplus
        s = 1
        while s < c:
            g = g + jnp.where(row_s >= s, pltpu.roll(g, s, 0), 0.0)
            s *= 2
        g_t = jnp.concatenate([g, pltpu.roll(g, 127, 1)], axis=0).T

        for p in range(n_pairs):
            a = 2 * p
            sl = slice(p * pw, (p + 1) * pw)
            q = q_all[:, sl]
            k = k_all[:, sl]
            v = v_all[:, sl]
            q = q * (inv_norm(q) * scale)
            k = k * inv_norm(k)
            gc_a = g[:, DN_HEADS + a:DN_HEADS + a + 1]
            gc_b = g[:, DN_HEADS + a + 1:DN_HEADS + a + 2]
            gl_a = gc_a[c - 1:c, :]
            gl_b = gc_b[c - 1:c, :]
            grow = g_t[DN_HEADS + a:DN_HEADS + a + 1, :]
            gcol = jnp.where(left, gc_a, gc_b)
            decay = jnp.exp(jnp.where(tri, gcol - grow, -1e30))
            beta = bcast2(beta_all[:, a:a + 1], beta_all[:, a + 1:a + 2])
            egc = bcast2(jnp.exp(gc_a), jnp.exp(gc_b))
            kb = k * beta
            chains.append(dict(
                b=b, p=p, q=q, k=k, kb=kb, vb=v * beta, kbg=kb * egc, qg=q * egc,
                kd=k * bcast2(jnp.exp(gl_a - gc_a), jnp.exp(gl_b - gc_b)),
                dl=jnp.concatenate([jnp.broadcast_to(jnp.exp(gl_a), (1, hd)),
                                    jnp.broadcast_to(jnp.exp(gl_b), (1, hd))], axis=1),
                decay=decay))

    for ch in chains:
        kq = _bdot_nt(jnp.concatenate([ch["kb"], ch["q"]], axis=0), blockdiag_heads(ch["k"]))
        ch["bk"] = -jnp.where(strict, kq[:c] * ch["decay"], 0.0)
        ch["attn"] = jnp.where(tri, kq[c:] * ch["decay"], 0.0)
        ch["qm"] = eye + ch["bk"]
    for ch in chains:
        ch["bk"] = _bdot(ch["bk"], blockdiag_rows(ch["bk"]))
    for lvl in range(1, 6):
        for ch in chains:
            if lvl < 5:
                r = _bdot(jnp.concatenate([ch["qm"], ch["bk"]], axis=0), blockdiag_rows(ch["bk"]))
                ch["qm"] = ch["qm"] + r[:c]
                ch["bk"] = r[c:]
            else:
                ch["qm"] = ch["qm"] + _bdot(ch["qm"], blockdiag_rows(ch["bk"]))
    for ch in chains:
        vb, kbg = ch["vb"], ch["kbg"]
        rhs = jnp.concatenate(
            [jnp.concatenate([vb[:, :hd], kbg[:, :hd], zeros_h, zeros_h], axis=1),
             jnp.concatenate([zeros_h, zeros_h, vb[:, hd:], kbg[:, hd:]], axis=1)], axis=0)
        sol = _bdot(ch["qm"], rhs)
        ch["u"] = jnp.concatenate([sol[:, :hd], sol[:, 2 * hd:3 * hd]], axis=1)
        ch["w"] = jnp.concatenate([sol[:, hd:2 * hd], sol[:, 3 * hd:]], axis=1)
    for ch in chains:
        ch["state"] = s_ref[ch["b"] * n_pairs + ch["p"]]
        r2 = _bdot(jnp.concatenate([ch["w"], ch["qg"]], axis=0), ch["state"])
        ch["vn"] = ch["u"] - r2[:c]
        ch["o"] = r2[c:]
    for ch in chains:
        ch["o"] = ch["o"] + _bdot(ch["attn"], blockdiag_heads(ch["vn"]))
    for ch in chains:
        upd = ch["state"] * ch["dl"] + _bdot(ch["kd"].T, ch["vn"])
        s_ref[ch["b"] * n_pairs + ch["p"]] = jnp.where(bd_mask, upd, 0.0)
    for ch in chains:
        b, p = ch["b"], ch["p"]
        for half in range(2):
            o = ch["o"][:, half * hd:(half + 1) * hd]
            sl = slice(p * pw + half * hd, p * pw + (half + 1) * hd)
            o = o * lax.rsqrt(jnp.mean(o * o, axis=-1, keepdims=True) + NORM_EPS)
            o = o * nw * _silu(z_ref[b, :, sl].astype(F32))
            o_ref[b, :, sl] = o.astype(o_ref.dtype)


def gdn(proj, small, conv_w, alog_row, dtb_row, norm_w, layer, bsz, seq, nb=1):
    m = proj.shape[0]
    c = DN_CHUNK
    n = seq // c
    w = DN_WIDTH
    qkv0 = 0
    proj3 = proj.reshape(bsz, seq, proj.shape[-1])
    small3 = small.reshape(bsz, seq, small.shape[-1])
    pw = 2 * DN_HEAD_DIM
    out = pl.pallas_call(
        functools.partial(_gdn_kernel, nb=nb),
        grid=(bsz // nb, n),
        in_specs=[
            pl.BlockSpec((nb, c, w), lambda b, t: (b, t, qkv0)),
            pl.BlockSpec((nb, c, w), lambda b, t: (b, t, qkv0 + 1)),
            pl.BlockSpec((nb, c, w), lambda b, t: (b, t, qkv0 + 2)),
            pl.BlockSpec((nb, c, w), lambda b, t: (b, t, qkv0 + 3)),
            pl.BlockSpec((nb, c, 128), lambda b, t: (b, t, 0)),
            pl.BlockSpec((None, DN_CONV, 3 * w), lambda b, t: (layer, 0, 0)),
            pl.BlockSpec((None, 1, 128), lambda b, t: (layer, 0, 0)),
            pl.BlockSpec((None, 1, 128), lambda b, t: (layer, 0, 0)),
            pl.BlockSpec((None, 1, DN_HEAD_DIM), lambda b, t: (layer, 0, 0)),
        ],
        out_specs=pl.BlockSpec((nb, c, w), lambda b, t: (b, t, 0)),
        out_shape=jax.ShapeDtypeStruct((bsz, seq, w), BF16),
        scratch_shapes=[pltpu.VMEM((nb * (DN_HEADS // 2), pw, pw), F32),
                        pltpu.VMEM((nb * 3, c, w), F32)],
        compiler_params=_params(2),
        name="gdn",
    )(proj3, proj3, proj3, proj3, small3, conv_w, alog_row, dtb_row, norm_w)
    return out.reshape(m, w)


def _route_kernel(h_ref, g_ref, r_ref, hn_ref, ti_ref, tw_ref):
    x = h_ref[...]
    ms = jnp.mean(x * x, axis=-1, keepdims=True)
    hn = x * lax.rsqrt(ms + NORM_EPS) * g_ref[...]
    hn_ref[...] = hn
    logits = jnp.dot(hn, r_ref[...], preferred_element_type=F32,
                     precision=lax.Precision.HIGHEST)
    lane = lax.broadcasted_iota(jnp.int32, logits.shape, 1)
    neg = jnp.float32(-3.0e38)
    logits = jnp.where(lane < N_EXPERTS, logits, neg)
    m1 = jnp.max(logits, axis=-1, keepdims=True)
    i1 = jnp.min(jnp.where(logits == m1, lane, 128), axis=-1, keepdims=True)
    rest = jnp.where(lane == i1, neg, logits)
    m2 = jnp.max(rest, axis=-1, keepdims=True)
    i2 = jnp.min(jnp.where(rest == m2, lane, 128), axis=-1, keepdims=True)
    e2 = jnp.exp(m2 - m1)
    w1 = 1.0 / (1.0 + e2)
    w2 = e2 / (1.0 + e2)
    ti_ref[...] = jnp.where(lane == 0, i1, jnp.where(lane == 1, i2, 0))
    tw_ref[...] = jnp.where(lane == 0, w1, jnp.where(lane == 1, w2, 0.0))


def route(h, g, layer, router_pad, j, tm=256):
    m, d = h.shape
    g = g.reshape(g.shape[0], 1, d)
    return pl.pallas_call(
        _route_kernel,
        grid=(m // tm,),
        in_specs=[pl.BlockSpec((tm, d), lambda i: (i, 0)),
                  pl.BlockSpec((None, 1, d), lambda i: (layer, 0, 0)),
                  pl.BlockSpec((None, d, 128), lambda i: (j, 0, 0))],
        out_specs=[pl.BlockSpec((tm, d), lambda i: (i, 0)),
                   pl.BlockSpec((tm, 128), lambda i: (i, 0)),
                   pl.BlockSpec((tm, 128), lambda i: (i, 0))],
        out_shape=[jax.ShapeDtypeStruct((m, d), F32),
                   jax.ShapeDtypeStruct((m, 128), jnp.int32),
                   jax.ShapeDtypeStruct((m, 128), F32)],
        compiler_params=_params(1),
        name="route",
    )(h, g, router_pad)


def _row_copy(src_ref, src_row, dst_ref, dst_row, sem):
    return pltpu.make_async_copy(src_ref.at[pl.ds(src_row, 1)], dst_ref.at[pl.ds(dst_row, 1)], sem)


def _moe_gather_kernel(pos_ref, tstart_ref, tend_ref, hn_ref, xs_ref, fill_sem, row_sem):
    step = pl.program_id(0)
    tm = MOE_TM

    n_tiles = xs_ref.shape[0] // tm
    n_active = tend_ref[N_EXPERTS - 1]

    def fill_tile(t):
        dst = pl.multiple_of(t * tm, tm)
        return pltpu.make_async_copy(hn_ref, xs_ref.at[pl.ds(dst, tm)], fill_sem)

    def fills(act):
        for e in range(N_EXPERTS):
            @pl.when(tend_ref[e] > tstart_ref[e])
            def _():
                act(fill_tile(tend_ref[e] - 1))
        for t in range(N_EXPERTS):
            @pl.when(n_active + t < n_tiles)
            def _():
                act(fill_tile(n_active + t))

    @pl.when(step == 0)
    def _():
        fills(lambda cp: cp.start())
        fills(lambda cp: cp.wait())

    base = step * tm * TOP_K

    def issue(t, carry):
        for slot in range(TOP_K):
            _row_copy(hn_ref, t, xs_ref, pos_ref[base + t * TOP_K + slot], row_sem).start(
                priority=slot)
        return carry

    lax.fori_loop(0, tm, issue, 0, unroll=4)

    for _ in range(TOP_K):
        pltpu.make_async_copy(hn_ref, xs_ref.at[pl.ds(0, tm)], row_sem).wait()


def moe_gather(hn, pos, tile_start, tile_end, rows):
    m, d = hn.shape
    tm = MOE_TM
    return pl.pallas_call(
        _moe_gather_kernel,
        grid_spec=pltpu.PrefetchScalarGridSpec(
            num_scalar_prefetch=3,
            grid=(m // tm,),
            in_specs=[pl.BlockSpec((tm, d), lambda i, pos, ts, te: (i, 0))],
            out_specs=pl.BlockSpec(memory_space=pl.ANY),
            scratch_shapes=[pltpu.SemaphoreType.DMA(()), pltpu.SemaphoreType.DMA(())],
        ),
        out_shape=jax.ShapeDtypeStruct((rows, d), F32),
        compiler_params=_params(1),
        name="moe_gather",
    )(pos, tile_start, tile_end, hn)


def _moe_group_kernel(ts_ref, te_ref, x_ref, *refs, n_w, compute, mults, k_split):
    w_refs = refs[:n_w]
    o_ref = refs[n_w]
    wb_refs = refs[n_w + 1:2 * n_w + 1]
    xbuf, obuf, xsem, osem = refs[2 * n_w + 1:]
    tm = MOE_TM
    tn = obuf.shape[2]
    n_tiles = o_ref.shape[0] // tm
    e = pl.program_id(1)
    kh = pl.program_id(2)
    col = pl.multiple_of(pl.program_id(0) * tn, tn)
    t0 = ts_ref[e]
    n = te_ref[e] - t0
    kblk = w_refs[0].shape[0]

    @pl.when(n > 0)
    def _():
        for w_ref, wb_ref in zip(w_refs, wb_refs):
            if k_split == 1:
                _cast_rows(w_ref, wb_ref)
            else:
                _cast_rows(w_ref, wb_ref.at[pl.ds(pl.multiple_of(kh * kblk, kblk), kblk)])

    big = mults[0]
    n_big = n // big
    rem = n - n_big * big

    def step_info(s):
        mult = jnp.where(s < n_big, big, 0)
        off = s * big
        idx = n_big
        first = n_big * big
        for mlt in mults[1:]:
            present = ((rem // mlt) % 2).astype(jnp.int32)
            here = jnp.logical_and(present == 1, s == idx)
            mult = jnp.where(here, mlt, mult)
            off = jnp.where(here, first, off)
            idx = idx + present
            first = first + present * mlt
        return mult, off

    n_steps = n_big
    for mlt in mults[1:]:
        n_steps = n_steps + (rem // mlt) % 2

    def for_mult(mult, fn):
        for mlt in mults:
            @pl.when(mult == mlt)
            def _():
                fn(mlt)

    def x_copy(off, mlt, slot):
        row = pl.multiple_of((t0 + off) * tm, tm)
        return pltpu.make_async_copy(x_ref.at[pl.ds(row, mlt * tm)],
                                     xbuf.at[slot, pl.ds(0, mlt * tm)], xsem.at[slot])

    def o_copy(tile, mlt, slot):
        row = pl.multiple_of(tile * tm, tm)
        return pltpu.make_async_copy(
            obuf.at[slot, pl.ds(0, mlt * tm)],
            o_ref.at[pl.ds(row, mlt * tm), pl.ds(col, tn)], osem.at[slot])

    @pl.when(jnp.logical_and(n > 0, kh == k_split - 1))
    def _():
        m0, off0 = step_info(0)
        for_mult(m0, lambda mlt: x_copy(off0, mlt, 0).start(priority=1))

        def body(s, carry):
            slot = lax.rem(s, 2)
            ms, offs = step_info(s)
            for_mult(ms, lambda mlt: x_copy(offs, mlt, slot).wait())

            @pl.when(s + 1 < n_steps)
            def _():
                mn, offn = step_info(s + 1)
                for_mult(mn, lambda mlt: x_copy(offn, mlt, 1 - slot).start(priority=1))

            @pl.when(s >= 2)
            def _():
                mp, offp = step_info(s - 2)
                for_mult(mp, lambda mlt: o_copy(t0 + offp, mlt, slot).wait())

            def run(mlt):
                rows = mlt * tm
                obuf[slot, :rows] = compute(
                    xbuf[slot, :rows], [wb[...] for wb in wb_refs]).astype(obuf.dtype)
                o_copy(t0 + offs, mlt, slot).start()

            for_mult(ms, run)
            return carry

        lax.fori_loop(0, n_steps, body, 0)

        @pl.when(n_steps >= 2)
        def _():
            mp, offp = step_info(n_steps - 2)
            for_mult(mp, lambda mlt: o_copy(t0 + offp, mlt, lax.rem(n_steps, 2)).wait())

        ml, offl = step_info(n_steps - 1)
        for_mult(ml, lambda mlt: o_copy(t0 + offl, mlt, lax.rem(n_steps - 1, 2)).wait())

    @pl.when(jnp.logical_and(e == N_EXPERTS - 1, kh == k_split - 1))
    def _():
        n_active = te_ref[N_EXPERTS - 1]
        obuf[0, :tm] = jnp.zeros((tm, tn), obuf.dtype)
        for k in range(N_EXPERTS):
            @pl.when(n_active + k < n_tiles)
            def _():
                o_copy(n_active + k, 1, 0).start()
        for k in range(N_EXPERTS):
            @pl.when(n_active + k < n_tiles)
            def _():
                o_copy(n_active + k, 1, 0).wait()


def _moe_up_compute(x, wbs):
    xb = x.astype(BF16)
    a = jnp.dot(xb, wbs[0], preferred_element_type=F32)
    b = jnp.dot(xb, wbs[1], preferred_element_type=F32)
    return _silu(a) * b


def _moe_down_compute(x, wbs):
    return jnp.dot(x, wbs[0], preferred_element_type=F32)


def moe_group_mm(x, ws, j, tile_start, tile_end, compute, n_out, out_dtype, tn, mults, k_split,
                 name):
    rows, k = x.shape
    tm = MOE_TM
    w_specs = [pl.BlockSpec((None, None, k // k_split, tn),
                            lambda c, e, kh, ts, te: (j, e, kh, c)) for _ in ws]
    return pl.pallas_call(
        functools.partial(_moe_group_kernel, n_w=len(ws), compute=compute, mults=mults,
                          k_split=k_split),
        grid_spec=pltpu.PrefetchScalarGridSpec(
            num_scalar_prefetch=2,
            grid=(n_out // tn, N_EXPERTS, k_split),
            in_specs=[pl.BlockSpec(memory_space=pl.ANY)] + w_specs,
            out_specs=pl.BlockSpec(memory_space=pl.ANY),
            scratch_shapes=[pltpu.VMEM((k, tn), BF16) for _ in ws] + [
                pltpu.VMEM((2, mults[0] * tm, k), x.dtype),
                pltpu.VMEM((2, mults[0] * tm, tn), out_dtype),
                pltpu.SemaphoreType.DMA((2,)),
                pltpu.SemaphoreType.DMA((2,)),
            ],
        ),
        out_shape=jax.ShapeDtypeStruct((rows, n_out), out_dtype),
        compiler_params=_params(3),
        name=name,
    )(tile_start, tile_end, x, *ws)


def moe_experts(xs, tile_start, tile_end, w_gate, w_up, w_down, j):
    d = xs.shape[1]
    f = w_gate.shape[-1]
    hmid = moe_group_mm(xs, [w_gate, w_up], j, tile_start, tile_end, _moe_up_compute,
                        f, BF16, 1024, (2, 1), 2, "moe_up")
    return moe_group_mm(hmid, [w_down], j, tile_start, tile_end, _moe_down_compute,
                        d, F32, 512, (2, 1), 2, "moe_down")


def _moe_combine_kernel(pos_ref, ys_ref, h_ref, tw_ref, g_ref, ho_ref, hn_ref, buf_ref, sem):
    tm = MOE_COMBINE_TM
    base = pl.program_id(0) * tm * TOP_K

    def issue(i, carry):
        for slot in range(TOP_K):
            _row_copy(ys_ref, pos_ref[base + i * TOP_K + slot], buf_ref.at[slot], i, sem).start(
                priority=slot)
        return carry

    lax.fori_loop(0, tm, issue, 0, unroll=4)

    for slot in range(TOP_K):
        pltpu.make_async_copy(ys_ref.at[pl.ds(0, tm)], buf_ref.at[slot], sem).wait()

    tw = tw_ref[...]
    x = h_ref[...] + tw[:, 0:1] * buf_ref[0] + tw[:, 1:2] * buf_ref[1]
    ho_ref[...] = x
    ms = jnp.mean(x * x, axis=-1, keepdims=True)
    hn_ref[...] = (x * lax.rsqrt(ms + NORM_EPS) * g_ref[...]).astype(hn_ref.dtype)


def moe_combine(ys, pos, h, tw, norm_g, layer):
    m, d = h.shape
    tm = MOE_COMBINE_TM
    norm_g = norm_g.reshape(norm_g.shape[0], 1, d)
    return pl.pallas_call(
        _moe_combine_kernel,
        grid_spec=pltpu.PrefetchScalarGridSpec(
            num_scalar_prefetch=1,
            grid=(m // tm,),
            in_specs=[
                pl.BlockSpec(memory_space=pl.ANY),
                pl.BlockSpec((tm, d), lambda i, pos: (i, 0)),
                pl.BlockSpec((tm, 128), lambda i, pos: (i, 0)),
                pl.BlockSpec((None, 1, d), lambda i, pos: (layer, 0, 0)),
            ],
            out_specs=[pl.BlockSpec((tm, d), lambda i, pos: (i, 0)),
                       pl.BlockSpec((tm, d), lambda i, pos: (i, 0))],
            scratch_shapes=[pltpu.VMEM((TOP_K, tm, d), F32), pltpu.SemaphoreType.DMA(())],
        ),
        out_shape=[jax.ShapeDtypeStruct((m, d), F32), jax.ShapeDtypeStruct((m, d), BF16)],
        compiler_params=_params(1),
        name="moe_combine",
    )(pos, ys, h, tw, norm_g)


def moe_layer(h, norm_g, layer, router_pad, w_gate, w_up, w_down, j, next_norm_g):
    m, d = h.shape
    tm = MOE_TM
    hn, ti, tw = route(h, norm_g, layer, router_pad, j)
    top_i = ti[:, :TOP_K].reshape(-1)
    n_pairs = m * TOP_K
    n_tiles = n_pairs // tm + N_EXPERTS
    onehot = (top_i[:, None] == jnp.arange(N_EXPERTS, dtype=jnp.int32)[None, :]).astype(jnp.int32)
    csum = jnp.cumsum(onehot, axis=0)
    rank = jnp.sum((csum - onehot) * onehot, axis=1)
    counts = csum[-1]
    tiles_per = (counts + tm - 1) // tm
    tile_end = jnp.cumsum(tiles_per).astype(jnp.int32)
    tile_start = (tile_end - tiles_per).astype(jnp.int32)
    pos = (jnp.sum(onehot * tile_start[None, :], axis=1) * tm + rank).astype(jnp.int32)

    xs = moe_gather(hn, pos, tile_start, tile_end, n_tiles * tm)
    ys = moe_experts(xs, tile_start, tile_end, w_gate, w_up, w_down, j)
    return moe_combine(ys, pos, h, tw, next_norm_g, layer)


def kernel(x, p, norm_mix, w_in, sgu_ln_g, sgu_ln_b, sgu_w, sgu_b, dn_conv_w, dn_a_log,
           dn_dt_bias, dn_norm_w, w_branch, w_out, norm_ffn, ffn_w_gate, ffn_w_up,
           ffn_w_down, moe_router, moe_w_gate, moe_w_up, moe_w_down, norm_ple,
           ple_w_gate, ple_b_gate, ple_w_proj, norm_final):
    bsz, seq, d = x.shape
    depth = p.shape[0]
    m = bsz * seq
    h = x.reshape(m, d)
    p2 = p.reshape(depth, m, p.shape[-1])
    main_w = 2 * SGU_WIDTH + 4 * DN_WIDTH
    n_small = 2 * DN_HEADS
    w_in_t = jnp.swapaxes(w_in, 1, 2)
    pad_h = 128 - n_small
    alog_row = jnp.pad(dn_a_log, ((0, 0), (DN_HEADS, pad_h))).reshape(depth, 1, 128)
    dtb_row = jnp.pad(dn_dt_bias, ((0, 0), (DN_HEADS, pad_h))).reshape(depth, 1, 128)
    ln_g = sgu_ln_g.reshape(depth, 1, SGU_WIDTH)
    ln_b = sgu_ln_b.reshape(depth, 1, SGU_WIDTH)
    sgu_b_t = jnp.swapaxes(sgu_b, 1, 2)
    nw = dn_norm_w.reshape(depth, 1, DN_HEAD_DIM)
    router_pad = jnp.pad(moe_router, ((0, 0), (0, 0), (0, 128 - N_EXPERTS)))
    ple_b = ple_b_gate.reshape(depth, 1, d)

    for i in range(depth):
        hn = rmsnorm(h, norm_mix, i, BF16)
        uv = fused_mm([(hn, 0)], [(w_in_t, (i,), 0, 0)], [], _ep_gelu, 2 * SGU_WIDTH, BF16,
                      1024, 1024, "proj_uv", w_t=True)
        qkvz = fused_mm([(hn, 0)], [(w_in_t, (i,), (2 * SGU_WIDTH) // 1024, 0)], [], _ep_plain,
                        4 * DN_WIDTH, BF16, 1024, 1024, "proj_qkvz", w_t=True)
        gates = shifted_sigmoid_mm(hn, w_in_t, i, main_w, n_small, 2 * d, 1024, 1024, "proj_gates")
        small = fused_mm([(hn, 0)], [(w_in_t, (i,), main_w // 128, 0)], [], _ep_plain, 128, F32,
                         1024, 128, "proj_small", w_t=True)
        y_a = sgu(uv, ln_g, ln_b, sgu_w, sgu_b_t, i)
        y_b = gdn(qkvz, small, dn_conv_w, alog_row, dtb_row, nw, i, bsz, seq, nb=4)
        merged = fused_mm([(y_a, 0), (y_b, 0)],
                          [(w_branch, (i, 0), 0, 0), (w_branch, (i, 1), 0, 1)],
                          [(gates, 0, "tile", 0), (gates, 0, "tile", d // 512)],
                          _ep_merge, d, BF16, 1024, 512, "merge")
        h = fused_mm([(merged, 0)], [(w_out, (i,), 0, 0)], [(h, 0, "tile", 0)],
                     _ep_residual, d, F32, 1024, 512, "out_proj")
        j = i // 2
        if i % 2 == 0:
            hn = rmsnorm(h, norm_ffn, i, BF16)
            hmid = fused_mm([(hn, 0)], [(ffn_w_gate, (j,), 0, 0), (ffn_w_up, (j,), 0, 0)], [],
                            _ep_swiglu, ffn_w_gate.shape[-1], BF16, 1024, 512, "ffn_up")
            h = fused_mm([(hmid, 0)], [(ffn_w_down, (j,), 0, 0)], [(h, 0, "tile", 0)],
                         _ep_residual, d, F32, 512, 512, "ffn_down")
            hn = rmsnorm(h, norm_ple, i, BF16)
        else:
            h, hn = moe_layer(h, norm_ffn, i, router_pad, moe_w_gate, moe_w_up, moe_w_down, j,
                              norm_ple)
        h = fused_mm([(hn, 0), (p2, i)],
                     [(ple_w_gate, (i,), 0, 0), (ple_w_proj, (i,), 0, 1)],
                     [(h, 0, "tile", 0), (ple_b, i, "row", 0)],
                     _ep_ple, d, F32, 1024, 512, "ple")
    out = rmsnorm(h, norm_final, 0, F32)
    return out.reshape(bsz, seq, d)
```

```python
import functools
import math

import jax
import jax.numpy as jnp
from jax import lax
from jax.experimental import pallas as pl
from jax.experimental.pallas import tpu as pltpu

F32 = jnp.float32
BF16 = jnp.bfloat16

D_MODEL = 2048
SGU_CHUNK = 128
SGU_GROUPS = 8
SGU_WIDTH = 1024
DN_HEADS = 8
DN_HEAD_DIM = 128
DN_WIDTH = 1024
DN_CONV = 4
DN_CHUNK = 64
N_EXPERTS = 8
TOP_K = 2
NORM_EPS = 1e-6

V7X_VMEM_LIMIT = 56 * 1024 * 1024
CAST_ROWS = 256
MOE_TM = 256
MOE_COMBINE_TM = 256


def _params(n_axes):
    return pltpu.CompilerParams(
        dimension_semantics=("arbitrary",) * n_axes,
        vmem_limit_bytes=V7X_VMEM_LIMIT)


def _sigmoid(x):
    return 1.0 / (1.0 + jnp.exp(-x))


def _silu(x):
    return x * _sigmoid(x)


def _gelu_tanh(x):
    c = math.sqrt(2.0 / math.pi)
    return 0.5 * x * (1.0 + jnp.tanh(c * (x + 0.044715 * (x * x * x))))


def _cast_rows(w_ref, wb_ref):
    k = w_ref.shape[0]
    ch = min(CAST_ROWS, k)

    def body(c, carry):
        r = pl.multiple_of(c * ch, ch)
        wb_ref[pl.ds(r, ch), :] = w_ref[pl.ds(r, ch), :].astype(BF16)
        return carry

    lax.fori_loop(0, k // ch, body, 0)


def _cast_rows_transposed(wt_ref, wb_ref):
    tn = wt_ref.shape[0]
    ch = min(CAST_ROWS, tn)
    for c in range(tn // ch):
        wb_ref[:, c * ch:(c + 1) * ch] = wt_ref[c * ch:(c + 1) * ch, :].T.astype(BF16)


def _rmsnorm_kernel(h_ref, g_ref, o_ref):
    x = h_ref[...]
    ms = jnp.mean(x * x, axis=-1, keepdims=True)
    o_ref[...] = (x * lax.rsqrt(ms + NORM_EPS) * g_ref[...]).astype(o_ref.dtype)


def rmsnorm(h, g, layer, out_dtype, tm=512):
    m, d = h.shape
    if g.ndim == 1:
        g = g.reshape(1, 1, d)
        layer = 0
    else:
        g = g.reshape(g.shape[0], 1, d)
    return pl.pallas_call(
        _rmsnorm_kernel,
        grid=(m // tm,),
        in_specs=[pl.BlockSpec((tm, d), lambda i: (i, 0)),
                  pl.BlockSpec((None, 1, d), lambda i: (layer, 0, 0))],
        out_specs=pl.BlockSpec((tm, d), lambda i: (i, 0)),
        out_shape=jax.ShapeDtypeStruct((m, d), out_dtype),
        compiler_params=_params(1),
        name="rmsnorm",
    )(h, g)


def _fused_mm_kernel(*refs, n_x, n_w, w_x, n_e, epilogue, w_t):
    x_refs = refs[:n_x]
    w_refs = refs[n_x:n_x + n_w]
    e_refs = refs[n_x + n_w:n_x + n_w + n_e]
    o_ref = refs[n_x + n_w + n_e]
    wb_refs = refs[n_x + n_w + n_e + 1:]

    @pl.when(pl.program_id(1) == 0)
    def _():
        for w_ref, wb_ref in zip(w_refs, wb_refs):
            (_cast_rows_transposed if w_t else _cast_rows)(w_ref, wb_ref)

    xs = [x_ref[...].astype(BF16) for x_ref in x_refs]
    accs = [jnp.dot(xs[w_x[l]], wb_refs[l][...], preferred_element_type=F32)
            for l in range(n_w)]
    o_ref[...] = epilogue(accs, [e[...] for e in e_refs], pl.program_id(0)).astype(o_ref.dtype)


def fused_mm(xs, ws, extras, epilogue, n_out, out_dtype, tm, tn, name, w_t=False):
    m = xs[0][0].shape[-2]
    grid = (n_out // tn, m // tm)
    in_specs, args = [], []
    for arr, lead in xs:
        k = arr.shape[-1]
        if arr.ndim == 2:
            in_specs.append(pl.BlockSpec((tm, k), lambda j, i: (i, 0)))
        else:
            in_specs.append(pl.BlockSpec((None, tm, k), lambda j, i, lead=lead: (lead, i, 0)))
        args.append(arr)
    scratch = []
    for arr, lead, off, _ in ws:
        k = arr.shape[-1] if w_t else arr.shape[-2]
        nlead = arr.ndim - 2
        lead = tuple(lead) if nlead else ()
        if w_t:
            in_specs.append(pl.BlockSpec(
                (None,) * nlead + (tn, k),
                lambda j, i, lead=lead, off=off: lead + (j + off, 0)))
        else:
            in_specs.append(pl.BlockSpec(
                (None,) * nlead + (k, tn),
                lambda j, i, lead=lead, off=off: lead + (0, j + off)))
        args.append(arr)
        scratch.append(pltpu.VMEM((k, tn), BF16))
    for arr, lead, kind, off in extras:
        if kind == "tile":
            if arr.ndim == 2:
                in_specs.append(pl.BlockSpec((tm, tn), lambda j, i, off=off: (i, j + off)))
            else:
                in_specs.append(pl.BlockSpec(
                    (None, tm, tn), lambda j, i, lead=lead, off=off: (lead, i, j + off)))
        else:
            in_specs.append(pl.BlockSpec(
                (None, 1, tn), lambda j, i, lead=lead, off=off: (lead, 0, j + off)))
        args.append(arr)
    kern = functools.partial(
        _fused_mm_kernel, n_x=len(xs), n_w=len(ws), w_x=tuple(w[3] for w in ws),
        n_e=len(extras), epilogue=epilogue, w_t=w_t)
    return pl.pallas_call(
        kern,
        grid=grid,
        in_specs=in_specs,
        out_specs=pl.BlockSpec((tm, tn), lambda j, i: (i, j)),
        out_shape=jax.ShapeDtypeStruct((m, n_out), out_dtype),
        scratch_shapes=scratch,
        compiler_params=_params(2),
        name=name,
    )(*args)


def _shifted_mm_kernel(x_ref, wa_ref, wn_ref, o_ref, wb_ref, *, shift):
    tn = wa_ref.shape[0]

    @pl.when(pl.program_id(1) == 0)
    def _():
        for c in range(tn // CAST_ROWS):
            lo = shift + c * CAST_ROWS
            hi = lo + CAST_ROWS
            if hi <= tn:
                rows = wa_ref[lo:hi, :]
            else:
                rows = jnp.concatenate([wa_ref[lo:tn, :], wn_ref[0:hi - tn, :]], axis=0)
            wb_ref[:, c * CAST_ROWS:(c + 1) * CAST_ROWS] = rows.T.astype(BF16)

    acc = jnp.dot(x_ref[...], wb_ref[...], preferred_element_type=F32)
    o_ref[...] = _sigmoid(acc).astype(o_ref.dtype)


def shifted_sigmoid_mm(x, wt, layer, row0, shift, n_out, tm, tn, name):
    m, k = x.shape
    base = row0 // tn
    return pl.pallas_call(
        functools.partial(_shifted_mm_kernel, shift=shift),
        grid=(n_out // tn, m // tm),
        in_specs=[
            pl.BlockSpec((tm, k), lambda j, i: (i, 0)),
            pl.BlockSpec((None, tn, k), lambda j, i: (layer, base + j, 0)),
            pl.BlockSpec((None, shift, k), lambda j, i: (layer, (base + j + 1) * (tn // shift), 0)),
        ],
        out_specs=pl.BlockSpec((tm, tn), lambda j, i: (i, j)),
        out_shape=jax.ShapeDtypeStruct((m, n_out), BF16),
        scratch_shapes=[pltpu.VMEM((k, tn), BF16)],
        compiler_params=_params(2),
        name=name,
    )(x, wt, wt)


def _ep_gelu(accs, extras, j):
    return _gelu_tanh(accs[0])


def _ep_sigmoid(accs, extras, j):
    return _sigmoid(accs[0])


def _ep_plain(accs, extras, j):
    return accs[0]


def _ep_merge(accs, extras, j):
    return extras[0].astype(F32) * accs[0] + extras[1].astype(F32) * accs[1]


def _ep_residual(accs, extras, j):
    return extras[0] + accs[0]


def _ep_swiglu(accs, extras, j):
    return _silu(accs[0]) * accs[1]


def _ep_ple(accs, extras, j):
    return extras[0] + _sigmoid(accs[0] + extras[1]) * accs[1]


def _sgu_kernel(u_ref, v_ref, lng_ref, lnb_ref, w_ref, bt_ref, o_ref, wm_ref):
    c = SGU_CHUNK

    @pl.when(pl.program_id(0) == 0)
    def _():
        ti = lax.broadcasted_iota(jnp.int32, (c, c), 0)
        si = lax.broadcasted_iota(jnp.int32, (c, c), 1)
        for g in range(SGU_GROUPS):
            wm_ref[g] = jnp.where(si <= ti, w_ref[g], 0.0).astype(BF16)

    v = v_ref[...].astype(F32)
    mu = jnp.mean(v, axis=-1, keepdims=True)
    vc = v - mu
    var = jnp.mean(vc * vc, axis=-1, keepdims=True)
    vln = (vc * lax.rsqrt(var + NORM_EPS) * lng_ref[...] + lnb_ref[...]).astype(BF16)
    bt = bt_ref[...]
    for g in range(SGU_GROUPS):
        sl = slice(g * c, (g + 1) * c)
        mixed = jnp.dot(wm_ref[g], vln[:, sl], preferred_element_type=F32) + bt[:, g:g + 1]
        o_ref[:, sl] = (u_ref[:, sl].astype(F32) * mixed).astype(o_ref.dtype)


def sgu(proj, ln_g, ln_b, w_s, b_s_t, layer):
    m = proj.shape[0]
    c = SGU_CHUNK
    return pl.pallas_call(
        _sgu_kernel,
        grid=(m // c,),
        in_specs=[
            pl.BlockSpec((c, SGU_WIDTH), lambda i: (i, 0)),
            pl.BlockSpec((c, SGU_WIDTH), lambda i: (i, 1)),
            pl.BlockSpec((None, 1, SGU_WIDTH), lambda i: (layer, 0, 0)),
            pl.BlockSpec((None, 1, SGU_WIDTH), lambda i: (layer, 0, 0)),
            pl.BlockSpec((None, SGU_GROUPS, c, c), lambda i: (layer, 0, 0, 0)),
            pl.BlockSpec((None, c, SGU_GROUPS), lambda i: (layer, 0, 0)),
        ],
        out_specs=pl.BlockSpec((c, SGU_WIDTH), lambda i: (i, 0)),
        out_shape=jax.ShapeDtypeStruct((m, SGU_WIDTH), BF16),
        scratch_shapes=[pltpu.VMEM((SGU_GROUPS, c, c), BF16)],
        compiler_params=_params(1),
        name="sgu",
    )(proj, proj, ln_g, ln_b, w_s, b_s_t)


def _bdot(a, b):
    return jnp.dot(a.astype(BF16), b.astype(BF16), preferred_element_type=F32)


def _bdot_nt(a, b):
    return lax.dot_general(a.astype(BF16), b.astype(BF16), (((1,), (1,)), ((), ())),
                           preferred_element_type=F32)


def _gdn_kernel(q_ref, k_ref, v_ref, z_ref, sm_ref, cw_ref, alog_ref, dtb_ref, nw_ref,
                o_ref, s_ref, prev_ref, *, nb):
    c = DN_CHUNK
    hd = DN_HEAD_DIM
    w = DN_WIDTH
    pw = 2 * hd
    n_pairs = DN_HEADS // 2

    @pl.when(pl.program_id(1) == 0)
    def _():
        s_ref[...] = jnp.zeros(s_ref.shape, F32)
        prev_ref[...] = jnp.zeros(prev_ref.shape, F32)

    row_w = lax.broadcasted_iota(jnp.int32, (c, w), 0)
    row_s = lax.broadcasted_iota(jnp.int32, (c, 128), 0)
    lane = lax.broadcasted_iota(jnp.int32, (c, 128), 1)
    left = lane < c
    jmod = jnp.where(left, lane, lane - c)
    tri = row_s >= jmod
    strict = row_s > jmod
    eye = (row_s == jmod).astype(F32)
    r256 = lax.broadcasted_iota(jnp.int32, (pw, pw), 0)
    c256 = lax.broadcasted_iota(jnp.int32, (pw, pw), 1)
    bd_mask = (r256 < hd) == (c256 < hd)
    zeros_h = jnp.zeros((c, hd), F32)
    nw = nw_ref[...]
    scale = hd ** -0.5

    def bcast2(col_a, col_b):
        return jnp.concatenate([jnp.broadcast_to(col_a, (c, hd)),
                                jnp.broadcast_to(col_b, (c, hd))], axis=1)

    def blockdiag_rows(x):
        return jnp.concatenate([jnp.where(left, x, 0.0), jnp.where(left, 0.0, x)], axis=0)

    def blockdiag_heads(x):
        return jnp.concatenate(
            [jnp.concatenate([x[:, :hd], zeros_h], axis=1),
             jnp.concatenate([zeros_h, x[:, hd:]], axis=1)], axis=0)

    def inv_norm(x):
        ss_a = jnp.sum(x[:, :hd] * x[:, :hd], axis=-1, keepdims=True)
        ss_b = jnp.sum(x[:, hd:] * x[:, hd:], axis=-1, keepdims=True)
        return bcast2(lax.rsqrt(ss_a + NORM_EPS), lax.rsqrt(ss_b + NORM_EPS))

    chains = []
    for b in range(nb):
        def conv_silu(x_ref, sec, b=b):
            x = x_ref[b].astype(F32)
            cw = cw_ref[:, sec * w:(sec + 1) * w]
            acc = cw[DN_CONV - 1:DN_CONV, :] * x
            for s in range(1, DN_CONV):
                slot = (b * 3 + sec) * (DN_CONV - 1) + (s - 1)
                rx = pltpu.roll(x, s, 0)
                shifted = jnp.where(row_w < s, prev_ref[slot], rx)
                prev_ref[slot] = rx
                acc = acc + cw[DN_CONV - 1 - s:DN_CONV - s, :] * shifted
            return _silu(acc)

        q_all = conv_silu(q_ref, 0)
        k_all = conv_silu(k_ref, 1)
        v_all = conv_silu(v_ref, 2)

        sm = sm_ref[b]
        beta_all = _sigmoid(sm)
        xa = sm + dtb_ref[...]
        softplus = jnp.maximum(xa, 0.0) + jnp.log(1.0 + jnp.exp(-jnp.abs(xa)))
        g = -jnp.exp(alog_ref[...]) * softplus
        s = 1
        while s < c:
            g = g + jnp.where(row_s >= s, pltpu.roll(g, s, 0), 0.0)
            s *= 2
        g_t = jnp.concatenate([g, pltpu.roll(g, 127, 1)], axis=0).T

        for p in range(n_pairs):
            a = 2 * p
            sl = slice(p * pw, (p + 1) * pw)
            q = q_all[:, sl]
            k = k_all[:, sl]
            v = v_all[:, sl]
            q = q * (inv_norm(q) * scale)
            k = k * inv_norm(k)
            gc_a = g[:, DN_HEADS + a:DN_HEADS + a + 1]
            gc_b = g[:, DN_HEADS + a + 1:DN_HEADS + a + 2]
            gl_a = gc_a[c - 1:c, :]
            gl_b = gc_b[c - 1:c, :]
            grow = g_t[DN_HEADS + a:DN_HEADS + a + 1, :]
            gcol = jnp.where(left, gc_a, gc_b)
            decay = jnp.exp(jnp.where(tri, gcol - grow, -1e30))
            beta = bcast2(beta_all[:, a:a + 1], beta_all[:, a + 1:a + 2])
            egc = bcast2(jnp.exp(gc_a), jnp.exp(gc_b))
            kb = k * beta
            chains.append(dict(
                b=b, p=p, q=q, k=k, kb=kb, vb=v * beta, kbg=kb * egc, qg=q * egc,
                kd=k * bcast2(jnp.exp(gl_a - gc_a), jnp.exp(gl_b - gc_b)),
                dl=jnp.concatenate([jnp.broadcast_to(jnp.exp(gl_a), (1, hd)),
                                    jnp.broadcast_to(jnp.exp(gl_b), (1, hd))], axis=1),
                decay=decay))

    for ch in chains:
        kq = _bdot_nt(jnp.concatenate([ch["kb"], ch["q"]], axis=0), blockdiag_heads(ch["k"]))
        ch["bk"] = -jnp.where(strict, kq[:c] * ch["decay"], 0.0)
        ch["attn"] = jnp.where(tri, kq[c:] * ch["decay"], 0.0)
        ch["qm"] = eye + ch["bk"]
    for ch in chains:
        ch["bk"] = _bdot(ch["bk"], blockdiag_rows(ch["bk"]))
    for lvl in range(1, 6):
        for ch in chains:
            if lvl < 5:
                r = _bdot(jnp.concatenate([ch["qm"], ch["bk"]], axis=0), blockdiag_rows(ch["bk"]))
                ch["qm"] = ch["qm"] + r[:c]
                ch["bk"] = r[c:]
            else:
                ch["qm"] = ch["qm"] + _bdot(ch["qm"], blockdiag_rows(ch["bk"]))
    for ch in chains:
        vb, kbg = ch["vb"], ch["kbg"]
        rhs = jnp.concatenate(
            [jnp.concatenate([vb[:, :hd], kbg[:, :hd], zeros_h, zeros_h], axis=1),
             jnp.concatenate([zeros_h, zeros_h, vb[:, hd:], kbg[:, hd:]], axis=1)], axis=0)
        sol = _bdot(ch["qm"], rhs)
        ch["u"] = jnp.concatenate([sol[:, :hd], sol[:, 2 * hd:3 * hd]], axis=1)
        ch["w"] = jnp.concatenate([sol[:, hd:2 * hd], sol[:, 3 * hd:]], axis=1)
    for ch in chains:
        ch["state"] = s_ref[ch["b"] * n_pairs + ch["p"]]
        r2 = _bdot(jnp.concatenate([ch["w"], ch["qg"]], axis=0), ch["state"])
        ch["vn"] = ch["u"] - r2[:c]
        ch["o"] = r2[c:]
    for ch in chains:
        ch["o"] = ch["o"] + _bdot(ch["attn"], blockdiag_heads(ch["vn"]))
    for ch in chains:
        upd = ch["state"] * ch["dl"] + _bdot(ch["kd"].T, ch["vn"])
        s_ref[ch["b"] * n_pairs + ch["p"]] = jnp.where(bd_mask, upd, 0.0)
    for ch in chains:
        b, p = ch["b"], ch["p"]
        for half in range(2):
            o = ch["o"][:, half * hd:(half + 1) * hd]
            sl = slice(p * pw + half * hd, p * pw + (half + 1) * hd)
            o = o * lax.rsqrt(jnp.mean(o * o, axis=-1, keepdims=True) + NORM_EPS)
            o = o * nw * _silu(z_ref[b, :, sl].astype(F32))
            o_ref[b, :, sl] = o.astype(o_ref.dtype)


def gdn(proj, small, conv_w, alog_row, dtb_row, norm_w, layer, bsz, seq, nb=1):
    m = proj.shape[0]
    c = DN_CHUNK
    n = seq // c
    w = DN_WIDTH
    qkv0 = 0
    proj3 = proj.reshape(bsz, seq, proj.shape[-1])
    small3 = small.reshape(bsz, seq, small.shape[-1])
    pw = 2 * DN_HEAD_DIM
    out = pl.pallas_call(
        functools.partial(_gdn_kernel, nb=nb),
        grid=(bsz // nb, n),
        in_specs=[
            pl.BlockSpec((nb, c, w), lambda b, t: (b, t, qkv0)),
            pl.BlockSpec((nb, c, w), lambda b, t: (b, t, qkv0 + 1)),
            pl.BlockSpec((nb, c, w), lambda b, t: (b, t, qkv0 + 2)),
            pl.BlockSpec((nb, c, w), lambda b, t: (b, t, qkv0 + 3)),
            pl.BlockSpec((nb, c, 128), lambda b, t: (b, t, 0)),
            pl.BlockSpec((None, DN_CONV, 3 * w), lambda b, t: (layer, 0, 0)),
            pl.BlockSpec((None, 1, 128), lambda b, t: (layer, 0, 0)),
            pl.BlockSpec((None, 1, 128), lambda b, t: (layer, 0, 0)),
            pl.BlockSpec((None, 1, DN_HEAD_DIM), lambda b, t: (layer, 0, 0)),
        ],
        out_specs=pl.BlockSpec((nb, c, w), lambda b, t: (b, t, 0)),
        out_shape=jax.ShapeDtypeStruct((bsz, seq, w), BF16),
        scratch_shapes=[pltpu.VMEM((nb * (DN_HEADS // 2), pw, pw), F32),
                        pltpu.VMEM((nb * 3 * (DN_CONV - 1), c, w), F32)],
        compiler_params=_params(2),
        name="gdn",
    )(proj3, proj3, proj3, proj3, small3, conv_w, alog_row, dtb_row, norm_w)
    return out.reshape(m, w)


def _route_kernel(h_ref, g_ref, r_ref, hn_ref, ti_ref, tw_ref):
    x = h_ref[...]
    ms = jnp.mean(x * x, axis=-1, keepdims=True)
    hn = x * lax.rsqrt(ms + NORM_EPS) * g_ref[...]
    hn_ref[...] = hn
    logits = jnp.dot(hn, r_ref[...], preferred_element_type=F32,
                     precision=lax.Precision.HIGHEST)
    lane = lax.broadcasted_iota(jnp.int32, logits.shape, 1)
    neg = jnp.float32(-3.0e38)
    logits = jnp.where(lane < N_EXPERTS, logits, neg)
    m1 = jnp.max(logits, axis=-1, keepdims=True)
    i1 = jnp.min(jnp.where(logits == m1, lane, 128), axis=-1, keepdims=True)
    rest = jnp.where(lane == i1, neg, logits)
    m2 = jnp.max(rest, axis=-1, keepdims=True)
    i2 = jnp.min(jnp.where(rest == m2, lane, 128), axis=-1, keepdims=True)
    e2 = jnp.exp(m2 - m1)
    w1 = 1.0 / (1.0 + e2)
    w2 = e2 / (1.0 + e2)
    ti_ref[...] = jnp.where(lane == 0, i1, jnp.where(lane == 1, i2, 0))
    tw_ref[...] = jnp.where(lane == 0, w1, jnp.where(lane == 1, w2, 0.0))


def route(h, g, layer, router_pad, j, tm=256):
    m, d = h.shape
    g = g.reshape(g.shape[0], 1, d)
    return pl.pallas_call(
        _route_kernel,
        grid=(m // tm,),
        in_specs=[pl.BlockSpec((tm, d), lambda i: (i, 0)),
                  pl.BlockSpec((None, 1, d), lambda i: (layer, 0, 0)),
                  pl.BlockSpec((None, d, 128), lambda i: (j, 0, 0))],
        out_specs=[pl.BlockSpec((tm, d), lambda i: (i, 0)),
                   pl.BlockSpec((tm, 128), lambda i: (i, 0)),
                   pl.BlockSpec((tm, 128), lambda i: (i, 0))],
        out_shape=[jax.ShapeDtypeStruct((m, d), F32),
                   jax.ShapeDtypeStruct((m, 128), jnp.int32),
                   jax.ShapeDtypeStruct((m, 128), F32)],
        compiler_params=_params(1),
        name="route",
    )(h, g, router_pad)


def _row_copy(src_ref, src_row, dst_ref, dst_row, sem):
    return pltpu.make_async_copy(src_ref.at[pl.ds(src_row, 1)], dst_ref.at[pl.ds(dst_row, 1)], sem)


def _moe_gather_kernel(pos_ref, tstart_ref, tend_ref, hn_ref, xs_ref, fill_sem, row_sem):
    step = pl.program_id(0)
    tm = MOE_TM

    n_tiles = xs_ref.shape[0] // tm
    n_active = tend_ref[N_EXPERTS - 1]

    def fill_tile(t):
        dst = pl.multiple_of(t * tm, tm)
        return pltpu.make_async_copy(hn_ref, xs_ref.at[pl.ds(dst, tm)], fill_sem)

    def fills(act):
        for e in range(N_EXPERTS):
            @pl.when(tend_ref[e] > tstart_ref[e])
            def _():
                act(fill_tile(tend_ref[e] - 1))
        for t in range(N_EXPERTS):
            @pl.when(n_active + t < n_tiles)
            def _():
                act(fill_tile(n_active + t))

    @pl.when(step == 0)
    def _():
        fills(lambda cp: cp.start())
        fills(lambda cp: cp.wait())

    base = step * tm * TOP_K

    def issue(t, carry):
        for slot in range(TOP_K):
            _row_copy(hn_ref, t, xs_ref, pos_ref[base + t * TOP_K + slot], row_sem).start(
                priority=slot)
        return carry

    lax.fori_loop(0, tm, issue, 0, unroll=4)

    for _ in range(TOP_K):
        pltpu.make_async_copy(hn_ref, xs_ref.at[pl.ds(0, tm)], row_sem).wait()


def moe_gather(hn, pos, tile_start, tile_end, rows):
    m, d = hn.shape
    tm = MOE_TM
    return pl.pallas_call(
        _moe_gather_kernel,
        grid_spec=pltpu.PrefetchScalarGridSpec(
            num_scalar_prefetch=3,
            grid=(m // tm,),
            in_specs=[pl.BlockSpec((tm, d), lambda i, pos, ts, te: (i, 0))],
            out_specs=pl.BlockSpec(memory_space=pl.ANY),
            scratch_shapes=[pltpu.SemaphoreType.DMA(()), pltpu.SemaphoreType.DMA(())],
        ),
        out_shape=jax.ShapeDtypeStruct((rows, d), F32),
        compiler_params=_params(1),
        name="moe_gather",
    )(pos, tile_start, tile_end, hn)


def _moe_group_kernel(ts_ref, te_ref, x_ref, *refs, n_w, compute, mults, k_split):
    w_refs = refs[:n_w]
    o_ref = refs[n_w]
    wb_refs = refs[n_w + 1:2 * n_w + 1]
    xbuf, obuf, xsem, osem = refs[2 * n_w + 1:]
    tm = MOE_TM
    tn = obuf.shape[2]
    n_tiles = o_ref.shape[0] // tm
    e = pl.program_id(1)
    kh = pl.program_id(2)
    col = pl.multiple_of(pl.program_id(0) * tn, tn)
    t0 = ts_ref[e]
    n = te_ref[e] - t0
    kblk = w_refs[0].shape[0]

    @pl.when(n > 0)
    def _():
        for w_ref, wb_ref in zip(w_refs, wb_refs):
            if k_split == 1:
                _cast_rows(w_ref, wb_ref)
            else:
                _cast_rows(w_ref, wb_ref.at[pl.ds(pl.multiple_of(kh * kblk, kblk), kblk)])

    big = mults[0]
    n_big = n // big
    rem = n - n_big * big

    def step_info(s):
        mult = jnp.where(s < n_big, big, 0)
        off = s * big
        idx = n_big
        first = n_big * big
        for mlt in mults[1:]:
            present = ((rem // mlt) % 2).astype(jnp.int32)
            here = jnp.logical_and(present == 1, s == idx)
            mult = jnp.where(here, mlt, mult)
            off = jnp.where(here, first, off)
            idx = idx + present
            first = first + present * mlt
        return mult, off

    n_steps = n_big
    for mlt in mults[1:]:
        n_steps = n_steps + (rem // mlt) % 2

    def for_mult(mult, fn):
        for mlt in mults:
            @pl.when(mult == mlt)
            def _():
                fn(mlt)

    def x_copy(off, mlt, slot):
        row = pl.multiple_of((t0 + off) * tm, tm)
        return pltpu.make_async_copy(x_ref.at[pl.ds(row, mlt * tm)],
                                     xbuf.at[slot, pl.ds(0, mlt * tm)], xsem.at[slot])

    def o_copy(tile, mlt, slot):
        row = pl.multiple_of(tile * tm, tm)
        return pltpu.make_async_copy(
            obuf.at[slot, pl.ds(0, mlt * tm)],
            o_ref.at[pl.ds(row, mlt * tm), pl.ds(col, tn)], osem.at[slot])

    @pl.when(jnp.logical_and(n > 0, kh == k_split - 1))
    def _():
        m0, off0 = step_info(0)
        for_mult(m0, lambda mlt: x_copy(off0, mlt, 0).start(priority=1))

        def body(s, carry):
            slot = lax.rem(s, 2)
            ms, offs = step_info(s)
            for_mult(ms, lambda mlt: x_copy(offs, mlt, slot).wait())

            @pl.when(s + 1 < n_steps)
            def _():
                mn, offn = step_info(s + 1)
                for_mult(mn, lambda mlt: x_copy(offn, mlt, 1 - slot).start(priority=1))

            @pl.when(s >= 2)
            def _():
                mp, offp = step_info(s - 2)
                for_mult(mp, lambda mlt: o_copy(t0 + offp, mlt, slot).wait())

            def run(mlt):
                rows = mlt * tm
                obuf[slot, :rows] = compute(
                    xbuf[slot, :rows], [wb[...] for wb in wb_refs]).astype(obuf.dtype)
                o_copy(t0 + offs, mlt, slot).start()

            for_mult(ms, run)
            return carry

        lax.fori_loop(0, n_steps, body, 0)

        @pl.when(n_steps >= 2)
        def _():
            mp, offp = step_info(n_steps - 2)
            for_mult(mp, lambda mlt: o_copy(t0 + offp, mlt, lax.rem(n_steps, 2)).wait())

        ml, offl = step_info(n_steps - 1)
        for_mult(ml, lambda mlt: o_copy(t0 + offl, mlt, lax.rem(n_steps - 1, 2)).wait())

    @pl.when(jnp.logical_and(e == N_EXPERTS - 1, kh == k_split - 1))
    def _():
        n_active = te_ref[N_EXPERTS - 1]
        obuf[0, :tm] = jnp.zeros((tm, tn), obuf.dtype)
        for k in range(N_EXPERTS):
            @pl.when(n_active + k < n_tiles)
            def _():
                o_copy(n_active + k, 1, 0).start()
        for k in range(N_EXPERTS):
            @pl.when(n_active + k < n_tiles)
            def _():
                o_copy(n_active + k, 1, 0).wait()


def _moe_up_compute(x, wbs):
    xb = x.astype(BF16)
    a = jnp.dot(xb, wbs[0], preferred_element_type=F32)
    b = jnp.dot(xb, wbs[1], preferred_element_type=F32)
    return _silu(a) * b


def _moe_down_compute(x, wbs):
    return jnp.dot(x, wbs[0], preferred_element_type=F32)


def moe_group_mm(x, ws, j, tile_start, tile_end, compute, n_out, out_dtype, tn, mults, k_split,
                 name):
    rows, k = x.shape
    tm = MOE_TM
    w_specs = [pl.BlockSpec((None, None, k // k_split, tn),
                            lambda c, e, kh, ts, te: (j, e, kh, c)) for _ in ws]
    return pl.pallas_call(
        functools.partial(_moe_group_kernel, n_w=len(ws), compute=compute, mults=mults,
                          k_split=k_split),
        grid_spec=pltpu.PrefetchScalarGridSpec(
            num_scalar_prefetch=2,
            grid=(n_out // tn, N_EXPERTS, k_split),
            in_specs=[pl.BlockSpec(memory_space=pl.ANY)] + w_specs,
            out_specs=pl.BlockSpec(memory_space=pl.ANY),
            scratch_shapes=[pltpu.VMEM((k, tn), BF16) for _ in ws] + [
                pltpu.VMEM((2, mults[0] * tm, k), x.dtype),
                pltpu.VMEM((2, mults[0] * tm, tn), out_dtype),
                pltpu.SemaphoreType.DMA((2,)),
                pltpu.SemaphoreType.DMA((2,)),
            ],
        ),
        out_shape=jax.ShapeDtypeStruct((rows, n_out), out_dtype),
        compiler_params=_params(3),
        name=name,
    )(tile_start, tile_end, x, *ws)


def moe_experts(xs, tile_start, tile_end, w_gate, w_up, w_down, j):
    d = xs.shape[1]
    f = w_gate.shape[-1]
    hmid = moe_group_mm(xs, [w_gate, w_up], j, tile_start, tile_end, _moe_up_compute,
                        f, BF16, 1024, (2, 1), 2, "moe_up")
    return moe_group_mm(hmid, [w_down], j, tile_start, tile_end, _moe_down_compute,
                        d, F32, 512, (2, 1), 2, "moe_down")


def _moe_combine_kernel(pos_ref, ys_ref, h_ref, tw_ref, g_ref, ho_ref, hn_ref, buf_ref, sem):
    tm = MOE_COMBINE_TM
    base = pl.program_id(0) * tm * TOP_K

    def issue(i, carry):
        for slot in range(TOP_K):
            _row_copy(ys_ref, pos_ref[base + i * TOP_K + slot], buf_ref.at[slot], i, sem).start(
                priority=slot)
        return carry

    lax.fori_loop(0, tm, issue, 0, unroll=4)

    for slot in range(TOP_K):
        pltpu.make_async_copy(ys_ref.at[pl.ds(0, tm)], buf_ref.at[slot], sem).wait()

    tw = tw_ref[...]
    x = h_ref[...] + tw[:, 0:1] * buf_ref[0] + tw[:, 1:2] * buf_ref[1]
    ho_ref[...] = x
    ms = jnp.mean(x * x, axis=-1, keepdims=True)
    hn_ref[...] = (x * lax.rsqrt(ms + NORM_EPS) * g_ref[...]).astype(hn_ref.dtype)


def moe_combine(ys, pos, h, tw, norm_g, layer):
    m, d = h.shape
    tm = MOE_COMBINE_TM
    norm_g = norm_g.reshape(norm_g.shape[0], 1, d)
    return pl.pallas_call(
        _moe_combine_kernel,
        grid_spec=pltpu.PrefetchScalarGridSpec(
            num_scalar_prefetch=1,
            grid=(m // tm,),
            in_specs=[
                pl.BlockSpec(memory_space=pl.ANY),
                pl.BlockSpec((tm, d), lambda i, pos: (i, 0)),
                pl.BlockSpec((tm, 128), lambda i, pos: (i, 0)),
                pl.BlockSpec((None, 1, d), lambda i, pos: (layer, 0, 0)),
            ],
            out_specs=[pl.BlockSpec((tm, d), lambda i, pos: (i, 0)),
                       pl.BlockSpec((tm, d), lambda i, pos: (i, 0))],
            scratch_shapes=[pltpu.VMEM((TOP_K, tm, d), F32), pltpu.SemaphoreType.DMA(())],
        ),
        out_shape=[jax.ShapeDtypeStruct((m, d), F32), jax.ShapeDtypeStruct((m, d), BF16)],
        compiler_params=_params(1),
        name="moe_combine",
    )(pos, ys, h, tw, norm_g)


def moe_layer(h, norm_g, layer, router_pad, w_gate, w_up, w_down, j, next_norm_g):
    m, d = h.shape
    tm = MOE_TM
    hn, ti, tw = route(h, norm_g, layer, router_pad, j)
    top_i = ti[:, :TOP_K].reshape(-1)
    n_pairs = m * TOP_K
    n_tiles = n_pairs // tm + N_EXPERTS
    onehot = (top_i[:, None] == jnp.arange(N_EXPERTS, dtype=jnp.int32)[None, :]).astype(jnp.int32)
    csum = jnp.cumsum(onehot, axis=0)
    rank = jnp.sum((csum - onehot) * onehot, axis=1)
    counts = csum[-1]
    tiles_per = (counts + tm - 1) // tm
    tile_end = jnp.cumsum(tiles_per).astype(jnp.int32)
    tile_start = (tile_end - tiles_per).astype(jnp.int32)
    pos = (jnp.sum(onehot * tile_start[None, :], axis=1) * tm + rank).astype(jnp.int32)

    xs = moe_gather(hn, pos, tile_start, tile_end, n_tiles * tm)
    ys = moe_experts(xs, tile_start, tile_end, w_gate, w_up, w_down, j)
    return moe_combine(ys, pos, h, tw, next_norm_g, layer)


def kernel(x, p, norm_mix, w_in, sgu_ln_g, sgu_ln_b, sgu_w, sgu_b, dn_conv_w, dn_a_log,
           dn_dt_bias, dn_norm_w, w_branch, w_out, norm_ffn, ffn_w_gate, ffn_w_up,
           ffn_w_down, moe_router, moe_w_gate, moe_w_up, moe_w_down, norm_ple,
           ple_w_gate, ple_b_gate, ple_w_proj, norm_final):
    bsz, seq, d = x.shape
    depth = p.shape[0]
    m = bsz * seq
    h = x.reshape(m, d)
    p2 = p.reshape(depth, m, p.shape[-1])
    main_w = 2 * SGU_WIDTH + 4 * DN_WIDTH
    n_small = 2 * DN_HEADS
    w_in_t = jnp.swapaxes(w_in, 1, 2)
    pad_h = 128 - n_small
    alog_row = jnp.pad(dn_a_log, ((0, 0), (DN_HEADS, pad_h))).reshape(depth, 1, 128)
    dtb_row = jnp.pad(dn_dt_bias, ((0, 0), (DN_HEADS, pad_h))).reshape(depth, 1, 128)
    ln_g = sgu_ln_g.reshape(depth, 1, SGU_WIDTH)
    ln_b = sgu_ln_b.reshape(depth, 1, SGU_WIDTH)
    sgu_b_t = jnp.swapaxes(sgu_b, 1, 2)
    nw = dn_norm_w.reshape(depth, 1, DN_HEAD_DIM)
    router_pad = jnp.pad(moe_router, ((0, 0), (0, 0), (0, 128 - N_EXPERTS)))
    ple_b = ple_b_gate.reshape(depth, 1, d)

    for i in range(depth):
        hn = rmsnorm(h, norm_mix, i, BF16)
        uv = fused_mm([(hn, 0)], [(w_in_t, (i,), 0, 0)], [], _ep_gelu, 2 * SGU_WIDTH, BF16,
                      1024, 1024, "proj_uv", w_t=True)
        qkvz = fused_mm([(hn, 0)], [(w_in_t, (i,), (2 * SGU_WIDTH) // 1024, 0)], [], _ep_plain,
                        4 * DN_WIDTH, BF16, 1024, 1024, "proj_qkvz", w_t=True)
        gates = shifted_sigmoid_mm(hn, w_in_t, i, main_w, n_small, 2 * d, 1024, 1024, "proj_gates")
        small = fused_mm([(hn, 0)], [(w_in_t, (i,), main_w // 128, 0)], [], _ep_plain, 128, F32,
                         1024, 128, "proj_small", w_t=True)
        y_a = sgu(uv, ln_g, ln_b, sgu_w, sgu_b_t, i)
        y_b = gdn(qkvz, small, dn_conv_w, alog_row, dtb_row, nw, i, bsz, seq, nb=4)
        merged = fused_mm([(y_a, 0), (y_b, 0)],
                          [(w_branch, (i, 0), 0, 0), (w_branch, (i, 1), 0, 1)],
                          [(gates, 0, "tile", 0), (gates, 0, "tile", d // 1024)],
                          _ep_merge, d, BF16, 1024, 1024, "merge")
        h = fused_mm([(merged, 0)], [(w_out, (i,), 0, 0)], [(h, 0, "tile", 0)],
                     _ep_residual, d, F32, 1024, 1024, "out_proj")
        j = i // 2
        if i % 2 == 0:
            hn = rmsnorm(h, norm_ffn, i, BF16)
            hmid = fused_mm([(hn, 0)], [(ffn_w_gate, (j,), 0, 0), (ffn_w_up, (j,), 0, 0)], [],
                            _ep_swiglu, ffn_w_gate.shape[-1], BF16, 1024, 512, "ffn_up")
            h = fused_mm([(hmid, 0)], [(ffn_w_down, (j,), 0, 0)], [(h, 0, "tile", 0)],
                         _ep_residual, d, F32, 512, 512, "ffn_down")
            hn = rmsnorm(h, norm_ple, i, BF16)
        else:
            h, hn = moe_layer(h, norm_ffn, i, router_pad, moe_w_gate, moe_w_up, moe_w_down, j,
                              norm_ple)
        h = fused_mm([(hn, 0), (p2, i)],
                     [(ple_w_gate, (i,), 0, 0), (ple_w_proj, (i,), 0, 1)],
                     [(h, 0, "tile", 0), (ple_b, i, "row", 0)],
                     _ep_ple, d, F32, 1024, 512, "ple")
    out = rmsnorm(h, norm_final, 0, F32)
    return out.reshape(bsz, seq, d)
```

```python
import functools
import math

import jax
import jax.numpy as jnp
from jax import lax
from jax.experimental import pallas as pl
from jax.experimental.pallas import tpu as pltpu

F32 = jnp.float32
BF16 = jnp.bfloat16

D_MODEL = 2048
SGU_CHUNK = 128
SGU_GROUPS = 8
SGU_WIDTH = 1024
DN_HEADS = 8
DN_HEAD_DIM = 128
DN_WIDTH = 1024
DN_CONV = 4
DN_CHUNK = 64
N_EXPERTS = 8
TOP_K = 2
NORM_EPS = 1e-6

V7X_VMEM_LIMIT = 56 * 1024 * 1024
CAST_ROWS = 256
MOE_TM = 256
MOE_COMBINE_TM = 512
SGU_CHUNKS_PER_STEP = 4


def _params(n_axes):
    return pltpu.CompilerParams(
        dimension_semantics=("arbitrary",) * n_axes,
        vmem_limit_bytes=V7X_VMEM_LIMIT)


def _sigmoid(x):
    return 1.0 / (1.0 + jnp.exp(-x))


def _silu(x):
    return x * _sigmoid(x)


def _gelu_tanh(x):
    c = math.sqrt(2.0 / math.pi)
    return 0.5 * x * (1.0 + jnp.tanh(c * (x + 0.044715 * (x * x * x))))


def _cast_rows(w_ref, wb_ref):
    k = w_ref.shape[0]
    ch = min(CAST_ROWS, k)

    def body(c, carry):
        r = pl.multiple_of(c * ch, ch)
        wb_ref[pl.ds(r, ch), :] = w_ref[pl.ds(r, ch), :].astype(BF16)
        return carry

    lax.fori_loop(0, k // ch, body, 0)


def _cast_rows_transposed(wt_ref, wb_ref):
    tn = wt_ref.shape[0]
    ch = min(CAST_ROWS, tn)
    for c in range(tn // ch):
        wb_ref[:, c * ch:(c + 1) * ch] = wt_ref[c * ch:(c + 1) * ch, :].T.astype(BF16)


def _rmsnorm_kernel(h_ref, g_ref, o_ref):
    x = h_ref[...]
    ms = jnp.mean(x * x, axis=-1, keepdims=True)
    o_ref[...] = (x * lax.rsqrt(ms + NORM_EPS) * g_ref[...]).astype(o_ref.dtype)


def rmsnorm(h, g, layer, out_dtype, tm=512):
    m, d = h.shape
    if g.ndim == 1:
        g = g.reshape(1, 1, d)
        layer = 0
    else:
        g = g.reshape(g.shape[0], 1, d)
    return pl.pallas_call(
        _rmsnorm_kernel,
        grid=(m // tm,),
        in_specs=[pl.BlockSpec((tm, d), lambda i: (i, 0)),
                  pl.BlockSpec((None, 1, d), lambda i: (layer, 0, 0))],
        out_specs=pl.BlockSpec((tm, d), lambda i: (i, 0)),
        out_shape=jax.ShapeDtypeStruct((m, d), out_dtype),
        compiler_params=_params(1),
        name="rmsnorm",
    )(h, g)


def _fused_mm_kernel(*refs, n_x, n_w, w_x, n_e, epilogue, w_t):
    x_refs = refs[:n_x]
    w_refs = refs[n_x:n_x + n_w]
    e_refs = refs[n_x + n_w:n_x + n_w + n_e]
    o_ref = refs[n_x + n_w + n_e]
    wb_refs = refs[n_x + n_w + n_e + 1:]

    @pl.when(pl.program_id(1) == 0)
    def _():
        for w_ref, wb_ref in zip(w_refs, wb_refs):
            (_cast_rows_transposed if w_t else _cast_rows)(w_ref, wb_ref)

    xs = [x_ref[...].astype(BF16) for x_ref in x_refs]
    accs = [jnp.dot(xs[w_x[l]], wb_refs[l][...], preferred_element_type=F32)
            for l in range(n_w)]
    o_ref[...] = epilogue(accs, [e[...] for e in e_refs], pl.program_id(0)).astype(o_ref.dtype)


def fused_mm(xs, ws, extras, epilogue, n_out, out_dtype, tm, tn, name, w_t=False):
    m = xs[0][0].shape[-2]
    grid = (n_out // tn, m // tm)
    in_specs, args = [], []
    for arr, lead in xs:
        k = arr.shape[-1]
        if arr.ndim == 2:
            in_specs.append(pl.BlockSpec((tm, k), lambda j, i: (i, 0)))
        else:
            in_specs.append(pl.BlockSpec((None, tm, k), lambda j, i, lead=lead: (lead, i, 0)))
        args.append(arr)
    scratch = []
    for arr, lead, off, _ in ws:
        k = arr.shape[-1] if w_t else arr.shape[-2]
        nlead = arr.ndim - 2
        lead = tuple(lead) if nlead else ()
        if w_t:
            in_specs.append(pl.BlockSpec(
                (None,) * nlead + (tn, k),
                lambda j, i, lead=lead, off=off: lead + (j + off, 0)))
        else:
            in_specs.append(pl.BlockSpec(
                (None,) * nlead + (k, tn),
                lambda j, i, lead=lead, off=off: lead + (0, j + off)))
        args.append(arr)
        scratch.append(pltpu.VMEM((k, tn), BF16))
    for arr, lead, kind, off in extras:
        if kind == "tile":
            if arr.ndim == 2:
                in_specs.append(pl.BlockSpec((tm, tn), lambda j, i, off=off: (i, j + off)))
            else:
                in_specs.append(pl.BlockSpec(
                    (None, tm, tn), lambda j, i, lead=lead, off=off: (lead, i, j + off)))
        else:
            in_specs.append(pl.BlockSpec(
                (None, 1, tn), lambda j, i, lead=lead, off=off: (lead, 0, j + off)))
        args.append(arr)
    kern = functools.partial(
        _fused_mm_kernel, n_x=len(xs), n_w=len(ws), w_x=tuple(w[3] for w in ws),
        n_e=len(extras), epilogue=epilogue, w_t=w_t)
    return pl.pallas_call(
        kern,
        grid=grid,
        in_specs=in_specs,
        out_specs=pl.BlockSpec((tm, tn), lambda j, i: (i, j)),
        out_shape=jax.ShapeDtypeStruct((m, n_out), out_dtype),
        scratch_shapes=scratch,
        compiler_params=_params(2),
        name=name,
    )(*args)


def _shifted_mm_kernel(x_ref, wa_ref, wn_ref, o_ref, wb_ref, *, shift):
    tn = wa_ref.shape[0]

    @pl.when(pl.program_id(1) == 0)
    def _():
        for c in range(tn // CAST_ROWS):
            lo = shift + c * CAST_ROWS
            hi = lo + CAST_ROWS
            if hi <= tn:
                rows = wa_ref[lo:hi, :]
            else:
                rows = jnp.concatenate([wa_ref[lo:tn, :], wn_ref[0:hi - tn, :]], axis=0)
            wb_ref[:, c * CAST_ROWS:(c + 1) * CAST_ROWS] = rows.T.astype(BF16)

    acc = jnp.dot(x_ref[...], wb_ref[...], preferred_element_type=F32)
    o_ref[...] = _sigmoid(acc).astype(o_ref.dtype)


def shifted_sigmoid_mm(x, wt, layer, row0, shift, n_out, tm, tn, name):
    m, k = x.shape
    base = row0 // tn
    return pl.pallas_call(
        functools.partial(_shifted_mm_kernel, shift=shift),
        grid=(n_out // tn, m // tm),
        in_specs=[
            pl.BlockSpec((tm, k), lambda j, i: (i, 0)),
            pl.BlockSpec((None, tn, k), lambda j, i: (layer, base + j, 0)),
            pl.BlockSpec((None, shift, k), lambda j, i: (layer, (base + j + 1) * (tn // shift), 0)),
        ],
        out_specs=pl.BlockSpec((tm, tn), lambda j, i: (i, j)),
        out_shape=jax.ShapeDtypeStruct((m, n_out), BF16),
        scratch_shapes=[pltpu.VMEM((k, tn), BF16)],
        compiler_params=_params(2),
        name=name,
    )(x, wt, wt)


def _ep_gelu(accs, extras, j):
    return _gelu_tanh(accs[0])


def _ep_sigmoid(accs, extras, j):
    return _sigmoid(accs[0])


def _ep_plain(accs, extras, j):
    return accs[0]


def _ep_merge(accs, extras, j):
    return extras[0].astype(F32) * accs[0] + extras[1].astype(F32) * accs[1]


def _ep_residual(accs, extras, j):
    return extras[0] + accs[0]


def _ep_swiglu(accs, extras, j):
    return _silu(accs[0]) * accs[1]


def _ep_ple(accs, extras, j):
    return extras[0] + _sigmoid(accs[0] + extras[1]) * accs[1]


def _sgu_kernel(u_ref, v_ref, lng_ref, lnb_ref, w_ref, bt_ref, o_ref, wm_ref):
    c = SGU_CHUNK

    @pl.when(pl.program_id(0) == 0)
    def _():
        ti = lax.broadcasted_iota(jnp.int32, (c, c), 0)
        si = lax.broadcasted_iota(jnp.int32, (c, c), 1)
        for g in range(SGU_GROUPS):
            wm_ref[g] = jnp.where(si <= ti, w_ref[g], 0.0).astype(BF16)

    bt = bt_ref[...]
    for ck in range(v_ref.shape[0] // c):
        rs = slice(ck * c, (ck + 1) * c)
        v = v_ref[rs, :].astype(F32)
        mu = jnp.mean(v, axis=-1, keepdims=True)
        vc = v - mu
        var = jnp.mean(vc * vc, axis=-1, keepdims=True)
        vln = (vc * lax.rsqrt(var + NORM_EPS) * lng_ref[...] + lnb_ref[...]).astype(BF16)
        for g in range(SGU_GROUPS):
            sl = slice(g * c, (g + 1) * c)
            mixed = jnp.dot(wm_ref[g], vln[:, sl], preferred_element_type=F32) + bt[:, g:g + 1]
            o_ref[rs, sl] = (u_ref[rs, sl].astype(F32) * mixed).astype(o_ref.dtype)


def sgu(proj, ln_g, ln_b, w_s, b_s_t, layer):
    m = proj.shape[0]
    c = SGU_CHUNK
    rows = SGU_CHUNKS_PER_STEP * c
    return pl.pallas_call(
        _sgu_kernel,
        grid=(m // rows,),
        in_specs=[
            pl.BlockSpec((rows, SGU_WIDTH), lambda i: (i, 0)),
            pl.BlockSpec((rows, SGU_WIDTH), lambda i: (i, 1)),
            pl.BlockSpec((None, 1, SGU_WIDTH), lambda i: (layer, 0, 0)),
            pl.BlockSpec((None, 1, SGU_WIDTH), lambda i: (layer, 0, 0)),
            pl.BlockSpec((None, SGU_GROUPS, c, c), lambda i: (layer, 0, 0, 0)),
            pl.BlockSpec((None, c, SGU_GROUPS), lambda i: (layer, 0, 0)),
        ],
        out_specs=pl.BlockSpec((rows, SGU_WIDTH), lambda i: (i, 0)),
        out_shape=jax.ShapeDtypeStruct((m, SGU_WIDTH), BF16),
        scratch_shapes=[pltpu.VMEM((SGU_GROUPS, c, c), BF16)],
        compiler_params=_params(1),
        name="sgu",
    )(proj, proj, ln_g, ln_b, w_s, b_s_t)


def _bdot(a, b):
    return jnp.dot(a.astype(BF16), b.astype(BF16), preferred_element_type=F32)


def _bdot_nt(a, b):
    return lax.dot_general(a.astype(BF16), b.astype(BF16), (((1,), (1,)), ((), ())),
                           preferred_element_type=F32)


def _gdn_kernel(q_ref, k_ref, v_ref, z_ref, sm_ref, cw_ref, alog_ref, dtb_ref, nw_ref,
                o_ref, s_ref, prev_ref, *, nb):
    c = DN_CHUNK
    hd = DN_HEAD_DIM
    w = DN_WIDTH
    pw = 2 * hd
    n_pairs = DN_HEADS // 2

    @pl.when(pl.program_id(1) == 0)
    def _():
        s_ref[...] = jnp.zeros(s_ref.shape, F32)
        prev_ref[...] = jnp.zeros(prev_ref.shape, F32)

    row_w = lax.broadcasted_iota(jnp.int32, (c, w), 0)
    row_s = lax.broadcasted_iota(jnp.int32, (c, 128), 0)
    lane = lax.broadcasted_iota(jnp.int32, (c, 128), 1)
    left = lane < c
    jmod = jnp.where(left, lane, lane - c)
    tri = row_s >= jmod
    strict = row_s > jmod
    eye = (row_s == jmod).astype(F32)
    r256 = lax.broadcasted_iota(jnp.int32, (pw, pw), 0)
    c256 = lax.broadcasted_iota(jnp.int32, (pw, pw), 1)
    bd_mask = (r256 < hd) == (c256 < hd)
    zeros_h = jnp.zeros((c, hd), F32)
    nw = nw_ref[...]
    scale = hd ** -0.5

    def bcast2(col_a, col_b):
        return jnp.concatenate([jnp.broadcast_to(col_a, (c, hd)),
                                jnp.broadcast_to(col_b, (c, hd))], axis=1)

    def blockdiag_rows(x):
        return jnp.concatenate([jnp.where(left, x, 0.0), jnp.where(left, 0.0, x)], axis=0)

    def blockdiag_heads(x):
        return jnp.concatenate(
            [jnp.concatenate([x[:, :hd], zeros_h], axis=1),
             jnp.concatenate([zeros_h, x[:, hd:]], axis=1)], axis=0)

    def inv_norm(x):
        ss_a = jnp.sum(x[:, :hd] * x[:, :hd], axis=-1, keepdims=True)
        ss_b = jnp.sum(x[:, hd:] * x[:, hd:], axis=-1, keepdims=True)
        return bcast2(lax.rsqrt(ss_a + NORM_EPS), lax.rsqrt(ss_b + NORM_EPS))

    chains = []
    for b in range(nb):
        def conv_silu(x_ref, sec, b=b):
            x = x_ref[b].astype(F32)
            cw = cw_ref[:, sec * w:(sec + 1) * w]
            acc = cw[DN_CONV - 1:DN_CONV, :] * x
            for s in range(1, DN_CONV):
                slot = (b * 3 + sec) * (DN_CONV - 1) + (s - 1)
                rx = pltpu.roll(x, s, 0)
                shifted = jnp.where(row_w < s, prev_ref[slot], rx)
                prev_ref[slot] = rx
                acc = acc + cw[DN_CONV - 1 - s:DN_CONV - s, :] * shifted
            return _silu(acc)

        q_all = conv_silu(q_ref, 0)
        k_all = conv_silu(k_ref, 1)
        v_all = conv_silu(v_ref, 2)

        sm = sm_ref[b]
        beta_all = _sigmoid(sm)
        xa = sm + dtb_ref[...]
        softplus = jnp.maximum(xa, 0.0) + jnp.log(1.0 + jnp.exp(-jnp.abs(xa)))
        g = -jnp.exp(alog_ref[...]) * softplus
        s = 1
        while s < c:
            g = g + jnp.where(row_s >= s, pltpu.roll(g, s, 0), 0.0)
            s *= 2
        g_t = jnp.concatenate([g, pltpu.roll(g, 127, 1)], axis=0).T

        for p in range(n_pairs):
            a = 2 * p
            sl = slice(p * pw, (p + 1) * pw)
            q = q_all[:, sl]
            k = k_all[:, sl]
            v = v_all[:, sl]
            q = q * (inv_norm(q) * scale)
            k = k * inv_norm(k)
            gc_a = g[:, DN_HEADS + a:DN_HEADS + a + 1]
            gc_b = g[:, DN_HEADS + a + 1:DN_HEADS + a + 2]
            gl_a = gc_a[c - 1:c, :]
            gl_b = gc_b[c - 1:c, :]
            grow = g_t[DN_HEADS + a:DN_HEADS + a + 1, :]
            gcol = jnp.where(left, gc_a, gc_b)
            decay = jnp.exp(jnp.where(tri, gcol - grow, -1e30))
            beta = bcast2(beta_all[:, a:a + 1], beta_all[:, a + 1:a + 2])
            egc = bcast2(jnp.exp(gc_a), jnp.exp(gc_b))
            kb = k * beta
            chains.append(dict(
                b=b, p=p, q=q, k=k, kb=kb, vb=v * beta, kbg=kb * egc, qg=q * egc,
                kd=k * bcast2(jnp.exp(gl_a - gc_a), jnp.exp(gl_b - gc_b)),
                dl=jnp.concatenate([jnp.broadcast_to(jnp.exp(gl_a), (1, hd)),
                                    jnp.broadcast_to(jnp.exp(gl_b), (1, hd))], axis=1),
                decay=decay))

    for ch in chains:
        kq = _bdot_nt(jnp.concatenate([ch["kb"], ch["q"]], axis=0), blockdiag_heads(ch["k"]))
        ch["bk"] = -jnp.where(strict, kq[:c] * ch["decay"], 0.0)
        ch["attn"] = jnp.where(tri, kq[c:] * ch["decay"], 0.0)
        ch["qm"] = eye + ch["bk"]
    for ch in chains:
        ch["bk"] = _bdot(ch["bk"], blockdiag_rows(ch["bk"]))
    for lvl in range(1, 6):
        for ch in chains:
            if lvl < 5:
                r = _bdot(jnp.concatenate([ch["qm"], ch["bk"]], axis=0), blockdiag_rows(ch["bk"]))
                ch["qm"] = ch["qm"] + r[:c]
                ch["bk"] = r[c:]
            else:
                ch["qm"] = ch["qm"] + _bdot(ch["qm"], blockdiag_rows(ch["bk"]))
    for ch in chains:
        vb, kbg = ch["vb"], ch["kbg"]
        rhs = jnp.concatenate(
            [jnp.concatenate([vb[:, :hd], kbg[:, :hd], zeros_h, zeros_h], axis=1),
             jnp.concatenate([zeros_h, zeros_h, vb[:, hd:], kbg[:, hd:]], axis=1)], axis=0)
        sol = _bdot(ch["qm"], rhs)
        ch["u"] = jnp.concatenate([sol[:, :hd], sol[:, 2 * hd:3 * hd]], axis=1)
        ch["w"] = jnp.concatenate([sol[:, hd:2 * hd], sol[:, 3 * hd:]], axis=1)
    for ch in chains:
        ch["state"] = s_ref[ch["b"] * n_pairs + ch["p"]]
        r2 = _bdot(jnp.concatenate([ch["w"], ch["qg"]], axis=0), ch["state"])
        ch["vn"] = ch["u"] - r2[:c]
        ch["o"] = r2[c:]
    for ch in chains:
        ch["o"] = ch["o"] + _bdot(ch["attn"], blockdiag_heads(ch["vn"]))
    for ch in chains:
        upd = ch["state"] * ch["dl"] + _bdot(ch["kd"].T, ch["vn"])
        s_ref[ch["b"] * n_pairs + ch["p"]] = jnp.where(bd_mask, upd, 0.0)
    for ch in chains:
        b, p = ch["b"], ch["p"]
        for half in range(2):
            o = ch["o"][:, half * hd:(half + 1) * hd]
            sl = slice(p * pw + half * hd, p * pw + (half + 1) * hd)
            o = o * lax.rsqrt(jnp.mean(o * o, axis=-1, keepdims=True) + NORM_EPS)
            o = o * nw * _silu(z_ref[b, :, sl].astype(F32))
            o_ref[b, :, sl] = o.astype(o_ref.dtype)


def gdn(proj, small, conv_w, alog_row, dtb_row, norm_w, layer, bsz, seq, nb=1):
    m = proj.shape[0]
    c = DN_CHUNK
    n = seq // c
    w = DN_WIDTH
    qkv0 = 0
    proj3 = proj.reshape(bsz, seq, proj.shape[-1])
    small3 = small.reshape(bsz, seq, small.shape[-1])
    pw = 2 * DN_HEAD_DIM
    out = pl.pallas_call(
        functools.partial(_gdn_kernel, nb=nb),
        grid=(bsz // nb, n),
        in_specs=[
            pl.BlockSpec((nb, c, w), lambda b, t: (b, t, qkv0)),
            pl.BlockSpec((nb, c, w), lambda b, t: (b, t, qkv0 + 1)),
            pl.BlockSpec((nb, c, w), lambda b, t: (b, t, qkv0 + 2)),
            pl.BlockSpec((nb, c, w), lambda b, t: (b, t, qkv0 + 3)),
            pl.BlockSpec((nb, c, 128), lambda b, t: (b, t, 0)),
            pl.BlockSpec((None, DN_CONV, 3 * w), lambda b, t: (layer, 0, 0)),
            pl.BlockSpec((None, 1, 128), lambda b, t: (layer, 0, 0)),
            pl.BlockSpec((None, 1, 128), lambda b, t: (layer, 0, 0)),
            pl.BlockSpec((None, 1, DN_HEAD_DIM), lambda b, t: (layer, 0, 0)),
        ],
        out_specs=pl.BlockSpec((nb, c, w), lambda b, t: (b, t, 0)),
        out_shape=jax.ShapeDtypeStruct((bsz, seq, w), BF16),
        scratch_shapes=[pltpu.VMEM((nb * (DN_HEADS // 2), pw, pw), F32),
                        pltpu.VMEM((nb * 3 * (DN_CONV - 1), c, w), F32)],
        compiler_params=_params(2),
        name="gdn",
    )(proj3, proj3, proj3, proj3, small3, conv_w, alog_row, dtb_row, norm_w)
    return out.reshape(m, w)


def _route_kernel(h_ref, g_ref, r_ref, hn_ref, ti_ref, tw_ref):
    x = h_ref[...]
    ms = jnp.mean(x * x, axis=-1, keepdims=True)
    hn = x * lax.rsqrt(ms + NORM_EPS) * g_ref[...]
    hn_ref[...] = hn
    logits = jnp.dot(hn, r_ref[...], preferred_element_type=F32,
                     precision=lax.Precision.HIGHEST)
    lane = lax.broadcasted_iota(jnp.int32, logits.shape, 1)
    neg = jnp.float32(-3.0e38)
    logits = jnp.where(lane < N_EXPERTS, logits, neg)
    m1 = jnp.max(logits, axis=-1, keepdims=True)
    i1 = jnp.min(jnp.where(logits == m1, lane, 128), axis=-1, keepdims=True)
    rest = jnp.where(lane == i1, neg, logits)
    m2 = jnp.max(rest, axis=-1, keepdims=True)
    i2 = jnp.min(jnp.where(rest == m2, lane, 128), axis=-1, keepdims=True)
    e2 = jnp.exp(m2 - m1)
    w1 = 1.0 / (1.0 + e2)
    w2 = e2 / (1.0 + e2)
    ti_ref[...] = jnp.where(lane == 0, i1, jnp.where(lane == 1, i2, 0))
    tw_ref[...] = jnp.where(lane == 0, w1, jnp.where(lane == 1, w2, 0.0))


def route(h, g, layer, router_pad, j, tm=512):
    m, d = h.shape
    g = g.reshape(g.shape[0], 1, d)
    return pl.pallas_call(
        _route_kernel,
        grid=(m // tm,),
        in_specs=[pl.BlockSpec((tm, d), lambda i: (i, 0)),
                  pl.BlockSpec((None, 1, d), lambda i: (layer, 0, 0)),
                  pl.BlockSpec((None, d, 128), lambda i: (j, 0, 0))],
        out_specs=[pl.BlockSpec((tm, d), lambda i: (i, 0)),
                   pl.BlockSpec((tm, 128), lambda i: (i, 0)),
                   pl.BlockSpec((tm, 128), lambda i: (i, 0))],
        out_shape=[jax.ShapeDtypeStruct((m, d), F32),
                   jax.ShapeDtypeStruct((m, 128), jnp.int32),
                   jax.ShapeDtypeStruct((m, 128), F32)],
        compiler_params=_params(1),
        name="route",
    )(h, g, router_pad)


def _row_copy(src_ref, src_row, dst_ref, dst_row, sem):
    return pltpu.make_async_copy(src_ref.at[pl.ds(src_row, 1)], dst_ref.at[pl.ds(dst_row, 1)], sem)


def _moe_gather_kernel(pos_ref, tstart_ref, tend_ref, hn_ref, xs_ref, fill_sem, row_sem):
    step = pl.program_id(0)
    tm = MOE_TM

    n_tiles = xs_ref.shape[0] // tm
    n_active = tend_ref[N_EXPERTS - 1]

    def fill_tile(t):
        dst = pl.multiple_of(t * tm, tm)
        return pltpu.make_async_copy(hn_ref, xs_ref.at[pl.ds(dst, tm)], fill_sem)

    def fills(act):
        for e in range(N_EXPERTS):
            @pl.when(tend_ref[e] > tstart_ref[e])
            def _():
                act(fill_tile(tend_ref[e] - 1))
        for t in range(N_EXPERTS):
            @pl.when(n_active + t < n_tiles)
            def _():
                act(fill_tile(n_active + t))

    @pl.when(step == 0)
    def _():
        fills(lambda cp: cp.start())
        fills(lambda cp: cp.wait())

    base = step * tm * TOP_K

    def issue(t, carry):
        for slot in range(TOP_K):
            _row_copy(hn_ref, t, xs_ref, pos_ref[base + t * TOP_K + slot], row_sem).start(
                priority=slot)
        return carry

    lax.fori_loop(0, tm, issue, 0, unroll=4)

    for _ in range(TOP_K):
        pltpu.make_async_copy(hn_ref, xs_ref.at[pl.ds(0, tm)], row_sem).wait()


def moe_gather(hn, pos, tile_start, tile_end, rows):
    m, d = hn.shape
    tm = MOE_TM
    return pl.pallas_call(
        _moe_gather_kernel,
        grid_spec=pltpu.PrefetchScalarGridSpec(
            num_scalar_prefetch=3,
            grid=(m // tm,),
            in_specs=[pl.BlockSpec((tm, d), lambda i, pos, ts, te: (i, 0))],
            out_specs=pl.BlockSpec(memory_space=pl.ANY),
            scratch_shapes=[pltpu.SemaphoreType.DMA(()), pltpu.SemaphoreType.DMA(())],
        ),
        out_shape=jax.ShapeDtypeStruct((rows, d), F32),
        compiler_params=_params(1),
        name="moe_gather",
    )(pos, tile_start, tile_end, hn)


def _moe_group_kernel(ts_ref, te_ref, x_ref, *refs, n_w, compute, mults, k_split):
    w_refs = refs[:n_w]
    o_ref = refs[n_w]
    wb_refs = refs[n_w + 1:2 * n_w + 1]
    xbuf, obuf, xsem, osem = refs[2 * n_w + 1:]
    tm = MOE_TM
    tn = obuf.shape[2]
    n_tiles = o_ref.shape[0] // tm
    e = pl.program_id(1)
    kh = pl.program_id(2)
    col = pl.multiple_of(pl.program_id(0) * tn, tn)
    t0 = ts_ref[e]
    n = te_ref[e] - t0
    kblk = w_refs[0].shape[0]

    @pl.when(n > 0)
    def _():
        for w_ref, wb_ref in zip(w_refs, wb_refs):
            if k_split == 1:
                _cast_rows(w_ref, wb_ref)
            else:
                _cast_rows(w_ref, wb_ref.at[pl.ds(pl.multiple_of(kh * kblk, kblk), kblk)])

    big = mults[0]
    n_big = n // big
    rem = n - n_big * big

    def step_info(s):
        mult = jnp.where(s < n_big, big, 0)
        off = s * big
        idx = n_big
        first = n_big * big
        for mlt in mults[1:]:
            present = ((rem // mlt) % 2).astype(jnp.int32)
            here = jnp.logical_and(present == 1, s == idx)
            mult = jnp.where(here, mlt, mult)
            off = jnp.where(here, first, off)
            idx = idx + present
            first = first + present * mlt
        return mult, off

    n_steps = n_big
    for mlt in mults[1:]:
        n_steps = n_steps + (rem // mlt) % 2

    def for_mult(mult, fn):
        for mlt in mults:
            @pl.when(mult == mlt)
            def _():
                fn(mlt)

    def x_copy(off, mlt, slot):
        row = pl.multiple_of((t0 + off) * tm, tm)
        return pltpu.make_async_copy(x_ref.at[pl.ds(row, mlt * tm)],
                                     xbuf.at[slot, pl.ds(0, mlt * tm)], xsem.at[slot])

    def o_copy(tile, mlt, slot):
        row = pl.multiple_of(tile * tm, tm)
        return pltpu.make_async_copy(
            obuf.at[slot, pl.ds(0, mlt * tm)],
            o_ref.at[pl.ds(row, mlt * tm), pl.ds(col, tn)], osem.at[slot])

    @pl.when(jnp.logical_and(n > 0, kh == k_split - 1))
    def _():
        m0, off0 = step_info(0)
        for_mult(m0, lambda mlt: x_copy(off0, mlt, 0).start(priority=1))

        def body(s, carry):
            slot = lax.rem(s, 2)
            ms, offs = step_info(s)
            for_mult(ms, lambda mlt: x_copy(offs, mlt, slot).wait())

            @pl.when(s + 1 < n_steps)
            def _():
                mn, offn = step_info(s + 1)
                for_mult(mn, lambda mlt: x_copy(offn, mlt, 1 - slot).start(priority=1))

            @pl.when(s >= 2)
            def _():
                mp, offp = step_info(s - 2)
                for_mult(mp, lambda mlt: o_copy(t0 + offp, mlt, slot).wait())

            def run(mlt):
                rows = mlt * tm
                obuf[slot, :rows] = compute(
                    xbuf[slot, :rows], [wb[...] for wb in wb_refs]).astype(obuf.dtype)
                o_copy(t0 + offs, mlt, slot).start()

            for_mult(ms, run)
            return carry

        lax.fori_loop(0, n_steps, body, 0)

        @pl.when(n_steps >= 2)
        def _():
            mp, offp = step_info(n_steps - 2)
            for_mult(mp, lambda mlt: o_copy(t0 + offp, mlt, lax.rem(n_steps, 2)).wait())

        ml, offl = step_info(n_steps - 1)
        for_mult(ml, lambda mlt: o_copy(t0 + offl, mlt, lax.rem(n_steps - 1, 2)).wait())

    @pl.when(jnp.logical_and(e == N_EXPERTS - 1, kh == k_split - 1))
    def _():
        n_active = te_ref[N_EXPERTS - 1]
        obuf[0, :tm] = jnp.zeros((tm, tn), obuf.dtype)
        for k in range(N_EXPERTS):
            @pl.when(n_active + k < n_tiles)
            def _():
                o_copy(n_active + k, 1, 0).start()
        for k in range(N_EXPERTS):
            @pl.when(n_active + k < n_tiles)
            def _():
                o_copy(n_active + k, 1, 0).wait()


def _moe_up_compute(x, wbs):
    xb = x.astype(BF16)
    a = jnp.dot(xb, wbs[0], preferred_element_type=F32)
    b = jnp.dot(xb, wbs[1], preferred_element_type=F32)
    return _silu(a) * b


def _moe_down_compute(x, wbs):
    return jnp.dot(x, wbs[0], preferred_element_type=F32)


def moe_group_mm(x, ws, j, tile_start, tile_end, compute, n_out, out_dtype, tn, mults, k_split,
                 name):
    rows, k = x.shape
    tm = MOE_TM
    w_specs = [pl.BlockSpec((None, None, k // k_split, tn),
                            lambda c, e, kh, ts, te: (j, e, kh, c)) for _ in ws]
    return pl.pallas_call(
        functools.partial(_moe_group_kernel, n_w=len(ws), compute=compute, mults=mults,
                          k_split=k_split),
        grid_spec=pltpu.PrefetchScalarGridSpec(
            num_scalar_prefetch=2,
            grid=(n_out // tn, N_EXPERTS, k_split),
            in_specs=[pl.BlockSpec(memory_space=pl.ANY)] + w_specs,
            out_specs=pl.BlockSpec(memory_space=pl.ANY),
            scratch_shapes=[pltpu.VMEM((k, tn), BF16) for _ in ws] + [
                pltpu.VMEM((2, mults[0] * tm, k), x.dtype),
                pltpu.VMEM((2, mults[0] * tm, tn), out_dtype),
                pltpu.SemaphoreType.DMA((2,)),
                pltpu.SemaphoreType.DMA((2,)),
            ],
        ),
        out_shape=jax.ShapeDtypeStruct((rows, n_out), out_dtype),
        compiler_params=_params(3),
        name=name,
    )(tile_start, tile_end, x, *ws)


def moe_experts(xs, tile_start, tile_end, w_gate, w_up, w_down, j):
    d = xs.shape[1]
    f = w_gate.shape[-1]
    hmid = moe_group_mm(xs, [w_gate, w_up], j, tile_start, tile_end, _moe_up_compute,
                        f, BF16, 1024, (2, 1), 2, "moe_up")
    return moe_group_mm(hmid, [w_down], j, tile_start, tile_end, _moe_down_compute,
                        d, F32, 512, (2, 1), 2, "moe_down")


def _moe_combine_kernel(pos_ref, ys_ref, h_ref, tw_ref, g_ref, ho_ref, hn_ref, buf_ref, sem):
    tm = MOE_COMBINE_TM
    base = pl.program_id(0) * tm * TOP_K

    def issue(i, carry):
        for slot in range(TOP_K):
            _row_copy(ys_ref, pos_ref[base + i * TOP_K + slot], buf_ref.at[slot], i, sem).start(
                priority=slot)
        return carry

    lax.fori_loop(0, tm, issue, 0, unroll=4)

    for slot in range(TOP_K):
        pltpu.make_async_copy(ys_ref.at[pl.ds(0, tm)], buf_ref.at[slot], sem).wait()

    tw = tw_ref[...]
    x = h_ref[...] + tw[:, 0:1] * buf_ref[0] + tw[:, 1:2] * buf_ref[1]
    ho_ref[...] = x
    ms = jnp.mean(x * x, axis=-1, keepdims=True)
    hn_ref[...] = (x * lax.rsqrt(ms + NORM_EPS) * g_ref[...]).astype(hn_ref.dtype)


def moe_combine(ys, pos, h, tw, norm_g, layer):
    m, d = h.shape
    tm = MOE_COMBINE_TM
    norm_g = norm_g.reshape(norm_g.shape[0], 1, d)
    return pl.pallas_call(
        _moe_combine_kernel,
        grid_spec=pltpu.PrefetchScalarGridSpec(
            num_scalar_prefetch=1,
            grid=(m // tm,),
            in_specs=[
                pl.BlockSpec(memory_space=pl.ANY),
                pl.BlockSpec((tm, d), lambda i, pos: (i, 0)),
                pl.BlockSpec((tm, 128), lambda i, pos: (i, 0)),
                pl.BlockSpec((None, 1, d), lambda i, pos: (layer, 0, 0)),
            ],
            out_specs=[pl.BlockSpec((tm, d), lambda i, pos: (i, 0)),
                       pl.BlockSpec((tm, d), lambda i, pos: (i, 0))],
            scratch_shapes=[pltpu.VMEM((TOP_K, tm, d), F32), pltpu.SemaphoreType.DMA(())],
        ),
        out_shape=[jax.ShapeDtypeStruct((m, d), F32), jax.ShapeDtypeStruct((m, d), BF16)],
        compiler_params=_params(1),
        name="moe_combine",
    )(pos, ys, h, tw, norm_g)


def moe_layer(h, norm_g, layer, router_pad, w_gate, w_up, w_down, j, next_norm_g):
    m, d = h.shape
    tm = MOE_TM
    hn, ti, tw = route(h, norm_g, layer, router_pad, j)
    top_i = ti[:, :TOP_K].reshape(-1)
    n_pairs = m * TOP_K
    n_tiles = n_pairs // tm + N_EXPERTS
    onehot = (top_i[:, None] == jnp.arange(N_EXPERTS, dtype=jnp.int32)[None, :]).astype(jnp.int32)
    csum = jnp.cumsum(onehot, axis=0)
    rank = jnp.sum((csum - onehot) * onehot, axis=1)
    counts = csum[-1]
    tiles_per = (counts + tm - 1) // tm
    tile_end = jnp.cumsum(tiles_per).astype(jnp.int32)
    tile_start = (tile_end - tiles_per).astype(jnp.int32)
    pos = (jnp.sum(onehot * tile_start[None, :], axis=1) * tm + rank).astype(jnp.int32)

    xs = moe_gather(hn, pos, tile_start, tile_end, n_tiles * tm)
    ys = moe_experts(xs, tile_start, tile_end, w_gate, w_up, w_down, j)
    return moe_combine(ys, pos, h, tw, next_norm_g, layer)


def kernel(x, p, norm_mix, w_in, sgu_ln_g, sgu_ln_b, sgu_w, sgu_b, dn_conv_w, dn_a_log,
           dn_dt_bias, dn_norm_w, w_branch, w_out, norm_ffn, ffn_w_gate, ffn_w_up,
           ffn_w_down, moe_router, moe_w_gate, moe_w_up, moe_w_down, norm_ple,
           ple_w_gate, ple_b_gate, ple_w_proj, norm_final):
    bsz, seq, d = x.shape
    depth = p.shape[0]
    m = bsz * seq
    h = x.reshape(m, d)
    p2 = p.reshape(depth, m, p.shape[-1])
    main_w = 2 * SGU_WIDTH + 4 * DN_WIDTH
    n_small = 2 * DN_HEADS
    w_in_t = jnp.swapaxes(w_in, 1, 2)
    pad_h = 128 - n_small
    alog_row = jnp.pad(dn_a_log, ((0, 0), (DN_HEADS, pad_h))).reshape(depth, 1, 128)
    dtb_row = jnp.pad(dn_dt_bias, ((0, 0), (DN_HEADS, pad_h))).reshape(depth, 1, 128)
    ln_g = sgu_ln_g.reshape(depth, 1, SGU_WIDTH)
    ln_b = sgu_ln_b.reshape(depth, 1, SGU_WIDTH)
    sgu_b_t = jnp.swapaxes(sgu_b, 1, 2)
    nw = dn_norm_w.reshape(depth, 1, DN_HEAD_DIM)
    router_pad = jnp.pad(moe_router, ((0, 0), (0, 0), (0, 128 - N_EXPERTS)))
    ple_b = ple_b_gate.reshape(depth, 1, d)

    for i in range(depth):
        hn = rmsnorm(h, norm_mix, i, BF16)
        uv = fused_mm([(hn, 0)], [(w_in_t, (i,), 0, 0)], [], _ep_gelu, 2 * SGU_WIDTH, BF16,
                      1024, 1024, "proj_uv", w_t=True)
        qkvz = fused_mm([(hn, 0)], [(w_in_t, (i,), (2 * SGU_WIDTH) // 1024, 0)], [], _ep_plain,
                        4 * DN_WIDTH, BF16, 1024, 1024, "proj_qkvz", w_t=True)
        gates = shifted_sigmoid_mm(hn, w_in_t, i, main_w, n_small, 2 * d, 1024, 1024, "proj_gates")
        small = fused_mm([(hn, 0)], [(w_in_t, (i,), main_w // 128, 0)], [], _ep_plain, 128, F32,
                         1024, 128, "proj_small", w_t=True)
        y_a = sgu(uv, ln_g, ln_b, sgu_w, sgu_b_t, i)
        y_b = gdn(qkvz, small, dn_conv_w, alog_row, dtb_row, nw, i, bsz, seq, nb=4)
        merged = fused_mm([(y_a, 0), (y_b, 0)],
                          [(w_branch, (i, 0), 0, 0), (w_branch, (i, 1), 0, 1)],
                          [(gates, 0, "tile", 0), (gates, 0, "tile", d // 1024)],
                          _ep_merge, d, BF16, 1024, 1024, "merge")
        h = fused_mm([(merged, 0)], [(w_out, (i,), 0, 0)], [(h, 0, "tile", 0)],
                     _ep_residual, d, F32, 1024, 1024, "out_proj")
        j = i // 2
        if i % 2 == 0:
            hn = rmsnorm(h, norm_ffn, i, BF16)
            hmid = fused_mm([(hn, 0)], [(ffn_w_gate, (j,), 0, 0), (ffn_w_up, (j,), 0, 0)], [],
                            _ep_swiglu, ffn_w_gate.shape[-1], BF16, 1024, 512, "ffn_up")
            h = fused_mm([(hmid, 0)], [(ffn_w_down, (j,), 0, 0)], [(h, 0, "tile", 0)],
                         _ep_residual, d, F32, 512, 512, "ffn_down")
            hn = rmsnorm(h, norm_ple, i, BF16)
        else:
            h, hn = moe_layer(h, norm_ffn, i, router_pad, moe_w_gate, moe_w_up, moe_w_down, j,
                              norm_ple)
        h = fused_mm([(hn, 0), (p2, i)],
                     [(ple_w_gate, (i,), 0, 0), (ple_w_proj, (i,), 0, 1)],
                     [(h, 0, "tile", 0), (ple_b, i, "row", 0)],
                     _ep_ple, d, F32, 512, 1024, "ple")
    out = rmsnorm(h, norm_final, 0, F32)
    return out.reshape(bsz, seq, d)
```

```python
import functools
import math

import jax
import jax.numpy as jnp
from jax import lax
from jax.experimental import pallas as pl
from jax.experimental.pallas import tpu as pltpu

F32 = jnp.float32
BF16 = jnp.bfloat16

D_MODEL = 2048
SGU_CHUNK = 128
SGU_GROUPS = 8
SGU_WIDTH = 1024
DN_HEADS = 8
DN_HEAD_DIM = 128
DN_WIDTH = 1024
DN_CONV = 4
DN_CHUNK = 64
N_EXPERTS = 8
TOP_K = 2
NORM_EPS = 1e-6

V7X_VMEM_LIMIT = 56 * 1024 * 1024
CAST_ROWS = 256
MOE_TM = 256
MOE_COMBINE_TM = 512
SGU_CHUNKS_PER_STEP = 4


def _params(n_axes):
    return pltpu.CompilerParams(
        dimension_semantics=("arbitrary",) * n_axes,
        vmem_limit_bytes=V7X_VMEM_LIMIT)


def _sigmoid(x):
    return 1.0 / (1.0 + jnp.exp(-x))


def _silu(x):
    return x * _sigmoid(x)


def _gelu_tanh(x):
    c = math.sqrt(2.0 / math.pi)
    return 0.5 * x * (1.0 + jnp.tanh(c * (x + 0.044715 * (x * x * x))))


def _cast_rows(w_ref, wb_ref):
    k = w_ref.shape[0]
    ch = min(CAST_ROWS, k)

    def body(c, carry):
        r = pl.multiple_of(c * ch, ch)
        wb_ref[pl.ds(r, ch), :] = w_ref[pl.ds(r, ch), :].astype(BF16)
        return carry

    lax.fori_loop(0, k // ch, body, 0)


def _cast_rows_transposed(wt_ref, wb_ref):
    tn = wt_ref.shape[0]
    ch = min(CAST_ROWS, tn)
    for c in range(tn // ch):
        wb_ref[:, c * ch:(c + 1) * ch] = wt_ref[c * ch:(c + 1) * ch, :].T.astype(BF16)


def _rmsnorm_kernel(h_ref, g_ref, o_ref):
    x = h_ref[...]
    ms = jnp.mean(x * x, axis=-1, keepdims=True)
    o_ref[...] = (x * lax.rsqrt(ms + NORM_EPS) * g_ref[...]).astype(o_ref.dtype)


def rmsnorm(h, g, layer, out_dtype, tm=512):
    m, d = h.shape
    if g.ndim == 1:
        g = g.reshape(1, 1, d)
        layer = 0
    else:
        g = g.reshape(g.shape[0], 1, d)
    return pl.pallas_call(
        _rmsnorm_kernel,
        grid=(m // tm,),
        in_specs=[pl.BlockSpec((tm, d), lambda i: (i, 0)),
                  pl.BlockSpec((None, 1, d), lambda i: (layer, 0, 0))],
        out_specs=pl.BlockSpec((tm, d), lambda i: (i, 0)),
        out_shape=jax.ShapeDtypeStruct((m, d), out_dtype),
        compiler_params=_params(1),
        name="rmsnorm",
    )(h, g)


def _fused_mm_kernel(*refs, n_x, n_w, w_x, n_e, epilogue, w_t):
    x_refs = refs[:n_x]
    w_refs = refs[n_x:n_x + n_w]
    e_refs = refs[n_x + n_w:n_x + n_w + n_e]
    o_ref = refs[n_x + n_w + n_e]
    wb_refs = refs[n_x + n_w + n_e + 1:]

    @pl.when(pl.program_id(1) == 0)
    def _():
        for w_ref, wb_ref in zip(w_refs, wb_refs):
            (_cast_rows_transposed if w_t else _cast_rows)(w_ref, wb_ref)

    xs = [x_ref[...].astype(BF16) for x_ref in x_refs]
    accs = [jnp.dot(xs[w_x[l]], wb_refs[l][...], preferred_element_type=F32)
            for l in range(n_w)]
    o_ref[...] = epilogue(accs, [e[...] for e in e_refs], pl.program_id(0)).astype(o_ref.dtype)


def fused_mm(xs, ws, extras, epilogue, n_out, out_dtype, tm, tn, name, w_t=False):
    m = xs[0][0].shape[-2]
    grid = (n_out // tn, m // tm)
    in_specs, args = [], []
    for arr, lead in xs:
        k = arr.shape[-1]
        if arr.ndim == 2:
            in_specs.append(pl.BlockSpec((tm, k), lambda j, i: (i, 0)))
        else:
            in_specs.append(pl.BlockSpec((None, tm, k), lambda j, i, lead=lead: (lead, i, 0)))
        args.append(arr)
    scratch = []
    for arr, lead, off, _ in ws:
        k = arr.shape[-1] if w_t else arr.shape[-2]
        nlead = arr.ndim - 2
        lead = tuple(lead) if nlead else ()
        if w_t:
            in_specs.append(pl.BlockSpec(
                (None,) * nlead + (tn, k),
                lambda j, i, lead=lead, off=off: lead + (j + off, 0)))
        else:
            in_specs.append(pl.BlockSpec(
                (None,) * nlead + (k, tn),
                lambda j, i, lead=lead, off=off: lead + (0, j + off)))
        args.append(arr)
        scratch.append(pltpu.VMEM((k, tn), BF16))
    for arr, lead, kind, off in extras:
        if kind == "tile":
            if arr.ndim == 2:
                in_specs.append(pl.BlockSpec((tm, tn), lambda j, i, off=off: (i, j + off)))
            else:
                in_specs.append(pl.BlockSpec(
                    (None, tm, tn), lambda j, i, lead=lead, off=off: (lead, i, j + off)))
        else:
            in_specs.append(pl.BlockSpec(
                (None, 1, tn), lambda j, i, lead=lead, off=off: (lead, 0, j + off)))
        args.append(arr)
    kern = functools.partial(
        _fused_mm_kernel, n_x=len(xs), n_w=len(ws), w_x=tuple(w[3] for w in ws),
        n_e=len(extras), epilogue=epilogue, w_t=w_t)
    return pl.pallas_call(
        kern,
        grid=grid,
        in_specs=in_specs,
        out_specs=pl.BlockSpec((tm, tn), lambda j, i: (i, j)),
        out_shape=jax.ShapeDtypeStruct((m, n_out), out_dtype),
        scratch_shapes=scratch,
        compiler_params=_params(2),
        name=name,
    )(*args)


def _shifted_mm_kernel(x_ref, wa_ref, wn_ref, o_ref, wb_ref, *, shift):
    tn = wa_ref.shape[0]

    @pl.when(pl.program_id(1) == 0)
    def _():
        for c in range(tn // CAST_ROWS):
            lo = shift + c * CAST_ROWS
            hi = lo + CAST_ROWS
            if hi <= tn:
                rows = wa_ref[lo:hi, :]
            else:
                rows = jnp.concatenate([wa_ref[lo:tn, :], wn_ref[0:hi - tn, :]], axis=0)
            wb_ref[:, c * CAST_ROWS:(c + 1) * CAST_ROWS] = rows.T.astype(BF16)

    acc = jnp.dot(x_ref[...], wb_ref[...], preferred_element_type=F32)
    o_ref[...] = _sigmoid(acc).astype(o_ref.dtype)


def shifted_sigmoid_mm(x, wt, layer, row0, shift, n_out, tm, tn, name):
    m, k = x.shape
    base = row0 // tn
    return pl.pallas_call(
        functools.partial(_shifted_mm_kernel, shift=shift),
        grid=(n_out // tn, m // tm),
        in_specs=[
            pl.BlockSpec((tm, k), lambda j, i: (i, 0)),
            pl.BlockSpec((None, tn, k), lambda j, i: (layer, base + j, 0)),
            pl.BlockSpec((None, shift, k), lambda j, i: (layer, (base + j + 1) * (tn // shift), 0)),
        ],
        out_specs=pl.BlockSpec((tm, tn), lambda j, i: (i, j)),
        out_shape=jax.ShapeDtypeStruct((m, n_out), BF16),
        scratch_shapes=[pltpu.VMEM((k, tn), BF16)],
        compiler_params=_params(2),
        name=name,
    )(x, wt, wt)


def _norm_small_kernel(h_ref, g_ref, wt_ref, hn_ref, sm_ref, wb_ref):
    @pl.when(pl.program_id(0) == 0)
    def _():
        _cast_rows_transposed(wt_ref, wb_ref)

    x = h_ref[...]
    ms = jnp.mean(x * x, axis=-1, keepdims=True)
    hn = (x * lax.rsqrt(ms + NORM_EPS) * g_ref[...]).astype(BF16)
    hn_ref[...] = hn
    sm_ref[...] = jnp.dot(hn, wb_ref[...], preferred_element_type=F32)


def norm_small(h, g, layer, wt, row_blk, tm=512):
    m, d = h.shape
    g = g.reshape(g.shape[0], 1, d)
    return pl.pallas_call(
        _norm_small_kernel,
        grid=(m // tm,),
        in_specs=[pl.BlockSpec((tm, d), lambda i: (i, 0)),
                  pl.BlockSpec((None, 1, d), lambda i: (layer, 0, 0)),
                  pl.BlockSpec((None, 128, d), lambda i: (layer, row_blk, 0))],
        out_specs=[pl.BlockSpec((tm, d), lambda i: (i, 0)),
                   pl.BlockSpec((tm, 128), lambda i: (i, 0))],
        out_shape=[jax.ShapeDtypeStruct((m, d), BF16),
                   jax.ShapeDtypeStruct((m, 128), F32)],
        scratch_shapes=[pltpu.VMEM((d, 128), BF16)],
        compiler_params=_params(1),
        name="norm_small",
    )(h, g, wt)


def _ep_gelu(accs, extras, j):
    return _gelu_tanh(accs[0])


def _ep_sigmoid(accs, extras, j):
    return _sigmoid(accs[0])


def _ep_plain(accs, extras, j):
    return accs[0]


def _ep_merge(accs, extras, j):
    return extras[0].astype(F32) * accs[0] + extras[1].astype(F32) * accs[1]


def _ep_residual(accs, extras, j):
    return extras[0] + accs[0]


def _ep_swiglu(accs, extras, j):
    return _silu(accs[0]) * accs[1]


def _ep_ple(accs, extras, j):
    return extras[0] + _sigmoid(accs[0] + extras[1]) * accs[1]


def _sgu_kernel(u_ref, v_ref, lng_ref, lnb_ref, w_ref, bt_ref, o_ref, wm_ref):
    c = SGU_CHUNK

    @pl.when(pl.program_id(0) == 0)
    def _():
        ti = lax.broadcasted_iota(jnp.int32, (c, c), 0)
        si = lax.broadcasted_iota(jnp.int32, (c, c), 1)
        for g in range(SGU_GROUPS):
            wm_ref[g] = jnp.where(si <= ti, w_ref[g], 0.0).astype(BF16)

    bt = bt_ref[...]
    for ck in range(v_ref.shape[0] // c):
        rs = slice(ck * c, (ck + 1) * c)
        v = v_ref[rs, :].astype(F32)
        mu = jnp.mean(v, axis=-1, keepdims=True)
        vc = v - mu
        var = jnp.mean(vc * vc, axis=-1, keepdims=True)
        vln = (vc * lax.rsqrt(var + NORM_EPS) * lng_ref[...] + lnb_ref[...]).astype(BF16)
        for g in range(SGU_GROUPS):
            sl = slice(g * c, (g + 1) * c)
            mixed = jnp.dot(wm_ref[g], vln[:, sl], preferred_element_type=F32) + bt[:, g:g + 1]
            o_ref[rs, sl] = (u_ref[rs, sl].astype(F32) * mixed).astype(o_ref.dtype)


def sgu(proj, ln_g, ln_b, w_s, b_s_t, layer):
    m = proj.shape[0]
    c = SGU_CHUNK
    rows = SGU_CHUNKS_PER_STEP * c
    return pl.pallas_call(
        _sgu_kernel,
        grid=(m // rows,),
        in_specs=[
            pl.BlockSpec((rows, SGU_WIDTH), lambda i: (i, 0)),
            pl.BlockSpec((rows, SGU_WIDTH), lambda i: (i, 1)),
            pl.BlockSpec((None, 1, SGU_WIDTH), lambda i: (layer, 0, 0)),
            pl.BlockSpec((None, 1, SGU_WIDTH), lambda i: (layer, 0, 0)),
            pl.BlockSpec((None, SGU_GROUPS, c, c), lambda i: (layer, 0, 0, 0)),
            pl.BlockSpec((None, c, SGU_GROUPS), lambda i: (layer, 0, 0)),
        ],
        out_specs=pl.BlockSpec((rows, SGU_WIDTH), lambda i: (i, 0)),
        out_shape=jax.ShapeDtypeStruct((m, SGU_WIDTH), BF16),
        scratch_shapes=[pltpu.VMEM((SGU_GROUPS, c, c), BF16)],
        compiler_params=_params(1),
        name="sgu",
    )(proj, proj, ln_g, ln_b, w_s, b_s_t)


def _bdot(a, b):
    return jnp.dot(a.astype(BF16), b.astype(BF16), preferred_element_type=F32)


def _bdot_nt(a, b):
    return lax.dot_general(a.astype(BF16), b.astype(BF16), (((1,), (1,)), ((), ())),
                           preferred_element_type=F32)


def _gdn_kernel(q_ref, k_ref, v_ref, z_ref, sm_ref, cw_ref, alog_ref, dtb_ref, nw_ref,
                o_ref, s_ref, prev_ref, *, nb):
    c = DN_CHUNK
    hd = DN_HEAD_DIM
    w = DN_WIDTH
    pw = 2 * hd
    n_pairs = DN_HEADS // 2

    @pl.when(pl.program_id(1) == 0)
    def _():
        s_ref[...] = jnp.zeros(s_ref.shape, F32)
        prev_ref[...] = jnp.zeros(prev_ref.shape, F32)

    row_w = lax.broadcasted_iota(jnp.int32, (c, w), 0)
    row_s = lax.broadcasted_iota(jnp.int32, (c, 128), 0)
    lane = lax.broadcasted_iota(jnp.int32, (c, 128), 1)
    left = lane < c
    jmod = jnp.where(left, lane, lane - c)
    tri = row_s >= jmod
    strict = row_s > jmod
    eye = (row_s == jmod).astype(F32)
    r256 = lax.broadcasted_iota(jnp.int32, (pw, pw), 0)
    c256 = lax.broadcasted_iota(jnp.int32, (pw, pw), 1)
    bd_mask = (r256 < hd) == (c256 < hd)
    zeros_h = jnp.zeros((c, hd), F32)
    nw = nw_ref[...]
    scale = hd ** -0.5

    def bcast2(col_a, col_b):
        return jnp.concatenate([jnp.broadcast_to(col_a, (c, hd)),
                                jnp.broadcast_to(col_b, (c, hd))], axis=1)

    def blockdiag_rows(x):
        return jnp.concatenate([jnp.where(left, x, 0.0), jnp.where(left, 0.0, x)], axis=0)

    def blockdiag_heads(x):
        return jnp.concatenate(
            [jnp.concatenate([x[:, :hd], zeros_h], axis=1),
             jnp.concatenate([zeros_h, x[:, hd:]], axis=1)], axis=0)

    def inv_norm(x):
        ss_a = jnp.sum(x[:, :hd] * x[:, :hd], axis=-1, keepdims=True)
        ss_b = jnp.sum(x[:, hd:] * x[:, hd:], axis=-1, keepdims=True)
        return bcast2(lax.rsqrt(ss_a + NORM_EPS), lax.rsqrt(ss_b + NORM_EPS))

    chains = []
    for b in range(nb):
        def conv_silu(x_ref, sec, b=b):
            x = x_ref[b].astype(F32)
            cw = cw_ref[:, sec * w:(sec + 1) * w]
            acc = cw[DN_CONV - 1:DN_CONV, :] * x
            for s in range(1, DN_CONV):
                slot = (b * 3 + sec) * (DN_CONV - 1) + (s - 1)
                rx = pltpu.roll(x, s, 0)
                shifted = jnp.where(row_w < s, prev_ref[slot], rx)
                prev_ref[slot] = rx
                acc = acc + cw[DN_CONV - 1 - s:DN_CONV - s, :] * shifted
            return _silu(acc)

        q_all = conv_silu(q_ref, 0)
        k_all = conv_silu(k_ref, 1)
        v_all = conv_silu(v_ref, 2)

        sm = sm_ref[b]
        beta_all = _sigmoid(sm)
        xa = sm + dtb_ref[...]
        softplus = jnp.maximum(xa, 0.0) + jnp.log(1.0 + jnp.exp(-jnp.abs(xa)))
        g = -jnp.exp(alog_ref[...]) * softplus
        s = 1
        while s < c:
            g = g + jnp.where(row_s >= s, pltpu.roll(g, s, 0), 0.0)
            s *= 2
        g_t = jnp.concatenate([g, pltpu.roll(g, 127, 1)], axis=0).T

        for p in range(n_pairs):
            a = 2 * p
            sl = slice(p * pw, (p + 1) * pw)
            q = q_all[:, sl]
            k = k_all[:, sl]
            v = v_all[:, sl]
            q = q * (inv_norm(q) * scale)
            k = k * inv_norm(k)
            gc_a = g[:, DN_HEADS + a:DN_HEADS + a + 1]
            gc_b = g[:, DN_HEADS + a + 1:DN_HEADS + a + 2]
            gl_a = gc_a[c - 1:c, :]
            gl_b = gc_b[c - 1:c, :]
            grow = g_t[DN_HEADS + a:DN_HEADS + a + 1, :]
            gcol = jnp.where(left, gc_a, gc_b)
            decay = jnp.exp(jnp.where(tri, gcol - grow, -1e30))
            beta = bcast2(beta_all[:, a:a + 1], beta_all[:, a + 1:a + 2])
            egc = bcast2(jnp.exp(gc_a), jnp.exp(gc_b))
            kb = k * beta
            chains.append(dict(
                b=b, p=p, q=q, k=k, kb=kb, vb=v * beta, kbg=kb * egc, qg=q * egc,
                kd=k * bcast2(jnp.exp(gl_a - gc_a), jnp.exp(gl_b - gc_b)),
                dl=jnp.concatenate([jnp.broadcast_to(jnp.exp(gl_a), (1, hd)),
                                    jnp.broadcast_to(jnp.exp(gl_b), (1, hd))], axis=1),
                decay=decay))

    for ch in chains:
        kq = _bdot_nt(jnp.concatenate([ch["kb"], ch["q"]], axis=0), blockdiag_heads(ch["k"]))
        ch["bk"] = -jnp.where(strict, kq[:c] * ch["decay"], 0.0)
        ch["attn"] = jnp.where(tri, kq[c:] * ch["decay"], 0.0)
        ch["qm"] = eye + ch["bk"]
    for ch in chains:
        ch["bk"] = _bdot(ch["bk"], blockdiag_rows(ch["bk"]))
    for lvl in range(1, 6):
        for ch in chains:
            if lvl < 5:
                r = _bdot(jnp.concatenate([ch["qm"], ch["bk"]], axis=0), blockdiag_rows(ch["bk"]))
                ch["qm"] = ch["qm"] + r[:c]
                ch["bk"] = r[c:]
            else:
                ch["qm"] = ch["qm"] + _bdot(ch["qm"], blockdiag_rows(ch["bk"]))
    for ch in chains:
        vb, kbg = ch["vb"], ch["kbg"]
        rhs = jnp.concatenate(
            [jnp.concatenate([vb[:, :hd], kbg[:, :hd], zeros_h, zeros_h], axis=1),
             jnp.concatenate([zeros_h, zeros_h, vb[:, hd:], kbg[:, hd:]], axis=1)], axis=0)
        sol = _bdot(ch["qm"], rhs)
        ch["u"] = jnp.concatenate([sol[:, :hd], sol[:, 2 * hd:3 * hd]], axis=1)
        ch["w"] = jnp.concatenate([sol[:, hd:2 * hd], sol[:, 3 * hd:]], axis=1)
    for ch in chains:
        ch["state"] = s_ref[ch["b"] * n_pairs + ch["p"]]
        r2 = _bdot(jnp.concatenate([ch["w"], ch["qg"]], axis=0), ch["state"])
        ch["vn"] = ch["u"] - r2[:c]
        ch["o"] = r2[c:]
    for ch in chains:
        ch["o"] = ch["o"] + _bdot(ch["attn"], blockdiag_heads(ch["vn"]))
    for ch in chains:
        upd = ch["state"] * ch["dl"] + _bdot(ch["kd"].T, ch["vn"])
        s_ref[ch["b"] * n_pairs + ch["p"]] = jnp.where(bd_mask, upd, 0.0)
    for ch in chains:
        b, p = ch["b"], ch["p"]
        for half in range(2):
            o = ch["o"][:, half * hd:(half + 1) * hd]
            sl = slice(p * pw + half * hd, p * pw + (half + 1) * hd)
            o = o * lax.rsqrt(jnp.mean(o * o, axis=-1, keepdims=True) + NORM_EPS)
            o = o * nw * _silu(z_ref[b, :, sl].astype(F32))
            o_ref[b, :, sl] = o.astype(o_ref.dtype)


def gdn(proj, small, conv_w, alog_row, dtb_row, norm_w, layer, bsz, seq, nb=1):
    m = proj.shape[0]
    c = DN_CHUNK
    n = seq // c
    w = DN_WIDTH
    qkv0 = 0
    proj3 = proj.reshape(bsz, seq, proj.shape[-1])
    small3 = small.reshape(bsz, seq, small.shape[-1])
    pw = 2 * DN_HEAD_DIM
    out = pl.pallas_call(
        functools.partial(_gdn_kernel, nb=nb),
        grid=(bsz // nb, n),
        in_specs=[
            pl.BlockSpec((nb, c, w), lambda b, t: (b, t, qkv0)),
            pl.BlockSpec((nb, c, w), lambda b, t: (b, t, qkv0 + 1)),
            pl.BlockSpec((nb, c, w), lambda b, t: (b, t, qkv0 + 2)),
            pl.BlockSpec((nb, c, w), lambda b, t: (b, t, qkv0 + 3)),
            pl.BlockSpec((nb, c, 128), lambda b, t: (b, t, 0)),
            pl.BlockSpec((None, DN_CONV, 3 * w), lambda b, t: (layer, 0, 0)),
            pl.BlockSpec((None, 1, 128), lambda b, t: (layer, 0, 0)),
            pl.BlockSpec((None, 1, 128), lambda b, t: (layer, 0, 0)),
            pl.BlockSpec((None, 1, DN_HEAD_DIM), lambda b, t: (layer, 0, 0)),
        ],
        out_specs=pl.BlockSpec((nb, c, w), lambda b, t: (b, t, 0)),
        out_shape=jax.ShapeDtypeStruct((bsz, seq, w), BF16),
        scratch_shapes=[pltpu.VMEM((nb * (DN_HEADS // 2), pw, pw), F32),
                        pltpu.VMEM((nb * 3 * (DN_CONV - 1), c, w), F32)],
        compiler_params=_params(2),
        name="gdn",
    )(proj3, proj3, proj3, proj3, small3, conv_w, alog_row, dtb_row, norm_w)
    return out.reshape(m, w)


def _route_kernel(h_ref, g_ref, r_ref, hn_ref, ti_ref, tw_ref):
    x = h_ref[...]
    ms = jnp.mean(x * x, axis=-1, keepdims=True)
    hn = x * lax.rsqrt(ms + NORM_EPS) * g_ref[...]
    hn_ref[...] = hn
    logits = jnp.dot(hn, r_ref[...], preferred_element_type=F32,
                     precision=lax.Precision.HIGHEST)
    lane = lax.broadcasted_iota(jnp.int32, logits.shape, 1)
    neg = jnp.float32(-3.0e38)
    logits = jnp.where(lane < N_EXPERTS, logits, neg)
    m1 = jnp.max(logits, axis=-1, keepdims=True)
    i1 = jnp.min(jnp.where(logits == m1, lane, 128), axis=-1, keepdims=True)
    rest = jnp.where(lane == i1, neg, logits)
    m2 = jnp.max(rest, axis=-1, keepdims=True)
    i2 = jnp.min(jnp.where(rest == m2, lane, 128), axis=-1, keepdims=True)
    e2 = jnp.exp(m2 - m1)
    w1 = 1.0 / (1.0 + e2)
    w2 = e2 / (1.0 + e2)
    ti_ref[...] = jnp.where(lane == 0, i1, jnp.where(lane == 1, i2, 0))
    tw_ref[...] = jnp.where(lane == 0, w1, jnp.where(lane == 1, w2, 0.0))


def route(h, g, layer, router_pad, j, tm=512):
    m, d = h.shape
    g = g.reshape(g.shape[0], 1, d)
    return pl.pallas_call(
        _route_kernel,
        grid=(m // tm,),
        in_specs=[pl.BlockSpec((tm, d), lambda i: (i, 0)),
                  pl.BlockSpec((None, 1, d), lambda i: (layer, 0, 0)),
                  pl.BlockSpec((None, d, 128), lambda i: (j, 0, 0))],
        out_specs=[pl.BlockSpec((tm, d), lambda i: (i, 0)),
                   pl.BlockSpec((tm, 128), lambda i: (i, 0)),
                   pl.BlockSpec((tm, 128), lambda i: (i, 0))],
        out_shape=[jax.ShapeDtypeStruct((m, d), F32),
                   jax.ShapeDtypeStruct((m, 128), jnp.int32),
                   jax.ShapeDtypeStruct((m, 128), F32)],
        compiler_params=_params(1),
        name="route",
    )(h, g, router_pad)


def _row_copy(src_ref, src_row, dst_ref, dst_row, sem):
    return pltpu.make_async_copy(src_ref.at[pl.ds(src_row, 1)], dst_ref.at[pl.ds(dst_row, 1)], sem)


def _moe_gather_kernel(pos_ref, tstart_ref, tend_ref, hn_ref, xs_ref, fill_sem, row_sem):
    step = pl.program_id(0)
    tm = MOE_TM

    n_tiles = xs_ref.shape[0] // tm
    n_active = tend_ref[N_EXPERTS - 1]

    def fill_tile(t):
        dst = pl.multiple_of(t * tm, tm)
        return pltpu.make_async_copy(hn_ref, xs_ref.at[pl.ds(dst, tm)], fill_sem)

    def fills(act):
        for e in range(N_EXPERTS):
            @pl.when(tend_ref[e] > tstart_ref[e])
            def _():
                act(fill_tile(tend_ref[e] - 1))
        for t in range(N_EXPERTS):
            @pl.when(n_active + t < n_tiles)
            def _():
                act(fill_tile(n_active + t))

    @pl.when(step == 0)
    def _():
        fills(lambda cp: cp.start())
        fills(lambda cp: cp.wait())

    base = step * tm * TOP_K

    def issue(t, carry):
        for slot in range(TOP_K):
            _row_copy(hn_ref, t, xs_ref, pos_ref[base + t * TOP_K + slot], row_sem).start(
                priority=slot)
        return carry

    lax.fori_loop(0, tm, issue, 0, unroll=4)

    for _ in range(TOP_K):
        pltpu.make_async_copy(hn_ref, xs_ref.at[pl.ds(0, tm)], row_sem).wait()


def moe_gather(hn, pos, tile_start, tile_end, rows):
    m, d = hn.shape
    tm = MOE_TM
    return pl.pallas_call(
        _moe_gather_kernel,
        grid_spec=pltpu.PrefetchScalarGridSpec(
            num_scalar_prefetch=3,
            grid=(m // tm,),
            in_specs=[pl.BlockSpec((tm, d), lambda i, pos, ts, te: (i, 0))],
            out_specs=pl.BlockSpec(memory_space=pl.ANY),
            scratch_shapes=[pltpu.SemaphoreType.DMA(()), pltpu.SemaphoreType.DMA(())],
        ),
        out_shape=jax.ShapeDtypeStruct((rows, d), F32),
        compiler_params=_params(1),
        name="moe_gather",
    )(pos, tile_start, tile_end, hn)


def _moe_group_kernel(ts_ref, te_ref, x_ref, *refs, n_w, compute, mults, k_split):
    w_refs = refs[:n_w]
    o_ref = refs[n_w]
    wb_refs = refs[n_w + 1:2 * n_w + 1]
    xbuf, obuf, xsem, osem = refs[2 * n_w + 1:]
    tm = MOE_TM
    tn = obuf.shape[2]
    n_tiles = o_ref.shape[0] // tm
    e = pl.program_id(1)
    kh = pl.program_id(2)
    col = pl.multiple_of(pl.program_id(0) * tn, tn)
    t0 = ts_ref[e]
    n = te_ref[e] - t0
    kblk = w_refs[0].shape[0]

    @pl.when(n > 0)
    def _():
        for w_ref, wb_ref in zip(w_refs, wb_refs):
            if k_split == 1:
                _cast_rows(w_ref, wb_ref)
            else:
                _cast_rows(w_ref, wb_ref.at[pl.ds(pl.multiple_of(kh * kblk, kblk), kblk)])

    big = mults[0]
    n_big = n // big
    rem = n - n_big * big

    def step_info(s):
        mult = jnp.where(s < n_big, big, 0)
        off = s * big
        idx = n_big
        first = n_big * big
        for mlt in mults[1:]:
            present = ((rem // mlt) % 2).astype(jnp.int32)
            here = jnp.logical_and(present == 1, s == idx)
            mult = jnp.where(here, mlt, mult)
            off = jnp.where(here, first, off)
            idx = idx + present
            first = first + present * mlt
        return mult, off

    n_steps = n_big
    for mlt in mults[1:]:
        n_steps = n_steps + (rem // mlt) % 2

    def for_mult(mult, fn):
        for mlt in mults:
            @pl.when(mult == mlt)
            def _():
                fn(mlt)

    def x_copy(off, mlt, slot):
        row = pl.multiple_of((t0 + off) * tm, tm)
        return pltpu.make_async_copy(x_ref.at[pl.ds(row, mlt * tm)],
                                     xbuf.at[slot, pl.ds(0, mlt * tm)], xsem.at[slot])

    def o_copy(tile, mlt, slot):
        row = pl.multiple_of(tile * tm, tm)
        return pltpu.make_async_copy(
            obuf.at[slot, pl.ds(0, mlt * tm)],
            o_ref.at[pl.ds(row, mlt * tm), pl.ds(col, tn)], osem.at[slot])

    @pl.when(jnp.logical_and(n > 0, kh == k_split - 1))
    def _():
        m0, off0 = step_info(0)
        for_mult(m0, lambda mlt: x_copy(off0, mlt, 0).start(priority=1))

        def body(s, carry):
            slot = lax.rem(s, 2)
            ms, offs = step_info(s)
            for_mult(ms, lambda mlt: x_copy(offs, mlt, slot).wait())

            @pl.when(s + 1 < n_steps)
            def _():
                mn, offn = step_info(s + 1)
                for_mult(mn, lambda mlt: x_copy(offn, mlt, 1 - slot).start(priority=1))

            @pl.when(s >= 2)
            def _():
                mp, offp = step_info(s - 2)
                for_mult(mp, lambda mlt: o_copy(t0 + offp, mlt, slot).wait())

            def run(mlt):
                rows = mlt * tm
                obuf[slot, :rows] = compute(
                    xbuf[slot, :rows], [wb[...] for wb in wb_refs]).astype(obuf.dtype)
                o_copy(t0 + offs, mlt, slot).start()

            for_mult(ms, run)
            return carry

        lax.fori_loop(0, n_steps, body, 0)

        @pl.when(n_steps >= 2)
        def _():
            mp, offp = step_info(n_steps - 2)
            for_mult(mp, lambda mlt: o_copy(t0 + offp, mlt, lax.rem(n_steps, 2)).wait())

        ml, offl = step_info(n_steps - 1)
        for_mult(ml, lambda mlt: o_copy(t0 + offl, mlt, lax.rem(n_steps - 1, 2)).wait())

    @pl.when(jnp.logical_and(e == N_EXPERTS - 1, kh == k_split - 1))
    def _():
        n_active = te_ref[N_EXPERTS - 1]
        obuf[0, :tm] = jnp.zeros((tm, tn), obuf.dtype)
        for k in range(N_EXPERTS):
            @pl.when(n_active + k < n_tiles)
            def _():
                o_copy(n_active + k, 1, 0).start()
        for k in range(N_EXPERTS):
            @pl.when(n_active + k < n_tiles)
            def _():
                o_copy(n_active + k, 1, 0).wait()


def _moe_up_compute(x, wbs):
    xb = x.astype(BF16)
    a = jnp.dot(xb, wbs[0], preferred_element_type=F32)
    b = jnp.dot(xb, wbs[1], preferred_element_type=F32)
    return _silu(a) * b


def _moe_down_compute(x, wbs):
    return jnp.dot(x, wbs[0], preferred_element_type=F32)


def moe_group_mm(x, ws, j, tile_start, tile_end, compute, n_out, out_dtype, tn, mults, k_split,
                 name):
    rows, k = x.shape
    tm = MOE_TM
    w_specs = [pl.BlockSpec((None, None, k // k_split, tn),
                            lambda c, e, kh, ts, te: (j, e, kh, c)) for _ in ws]
    return pl.pallas_call(
        functools.partial(_moe_group_kernel, n_w=len(ws), compute=compute, mults=mults,
                          k_split=k_split),
        grid_spec=pltpu.PrefetchScalarGridSpec(
            num_scalar_prefetch=2,
            grid=(n_out // tn, N_EXPERTS, k_split),
            in_specs=[pl.BlockSpec(memory_space=pl.ANY)] + w_specs,
            out_specs=pl.BlockSpec(memory_space=pl.ANY),
            scratch_shapes=[pltpu.VMEM((k, tn), BF16) for _ in ws] + [
                pltpu.VMEM((2, mults[0] * tm, k), x.dtype),
                pltpu.VMEM((2, mults[0] * tm, tn), out_dtype),
                pltpu.SemaphoreType.DMA((2,)),
                pltpu.SemaphoreType.DMA((2,)),
            ],
        ),
        out_shape=jax.ShapeDtypeStruct((rows, n_out), out_dtype),
        compiler_params=_params(3),
        name=name,
    )(tile_start, tile_end, x, *ws)


def moe_experts(xs, tile_start, tile_end, w_gate, w_up, w_down, j):
    d = xs.shape[1]
    f = w_gate.shape[-1]
    hmid = moe_group_mm(xs, [w_gate, w_up], j, tile_start, tile_end, _moe_up_compute,
                        f, BF16, 1024, (2, 1), 2, "moe_up")
    return moe_group_mm(hmid, [w_down], j, tile_start, tile_end, _moe_down_compute,
                        d, F32, 512, (2, 1), 2, "moe_down")


def _moe_combine_kernel(pos_ref, ys_ref, h_ref, tw_ref, g_ref, ho_ref, hn_ref, buf_ref, sem):
    tm = MOE_COMBINE_TM
    base = pl.program_id(0) * tm * TOP_K

    def issue(i, carry):
        for slot in range(TOP_K):
            _row_copy(ys_ref, pos_ref[base + i * TOP_K + slot], buf_ref.at[slot], i, sem).start(
                priority=slot)
        return carry

    lax.fori_loop(0, tm, issue, 0, unroll=4)

    for slot in range(TOP_K):
        pltpu.make_async_copy(ys_ref.at[pl.ds(0, tm)], buf_ref.at[slot], sem).wait()

    tw = tw_ref[...]
    x = h_ref[...] + tw[:, 0:1] * buf_ref[0] + tw[:, 1:2] * buf_ref[1]
    ho_ref[...] = x
    ms = jnp.mean(x * x, axis=-1, keepdims=True)
    hn_ref[...] = (x * lax.rsqrt(ms + NORM_EPS) * g_ref[...]).astype(hn_ref.dtype)


def moe_combine(ys, pos, h, tw, norm_g, layer):
    m, d = h.shape
    tm = MOE_COMBINE_TM
    norm_g = norm_g.reshape(norm_g.shape[0], 1, d)
    return pl.pallas_call(
        _moe_combine_kernel,
        grid_spec=pltpu.PrefetchScalarGridSpec(
            num_scalar_prefetch=1,
            grid=(m // tm,),
            in_specs=[
                pl.BlockSpec(memory_space=pl.ANY),
                pl.BlockSpec((tm, d), lambda i, pos: (i, 0)),
                pl.BlockSpec((tm, 128), lambda i, pos: (i, 0)),
                pl.BlockSpec((None, 1, d), lambda i, pos: (layer, 0, 0)),
            ],
            out_specs=[pl.BlockSpec((tm, d), lambda i, pos: (i, 0)),
                       pl.BlockSpec((tm, d), lambda i, pos: (i, 0))],
            scratch_shapes=[pltpu.VMEM((TOP_K, tm, d), F32), pltpu.SemaphoreType.DMA(())],
        ),
        out_shape=[jax.ShapeDtypeStruct((m, d), F32), jax.ShapeDtypeStruct((m, d), BF16)],
        compiler_params=_params(1),
        name="moe_combine",
    )(pos, ys, h, tw, norm_g)


def moe_layer(h, norm_g, layer, router_pad, w_gate, w_up, w_down, j, next_norm_g):
    m, d = h.shape
    tm = MOE_TM
    hn, ti, tw = route(h, norm_g, layer, router_pad, j)
    top_i = ti[:, :TOP_K].reshape(-1)
    n_pairs = m * TOP_K
    n_tiles = n_pairs // tm + N_EXPERTS
    onehot = (top_i[:, None] == jnp.arange(N_EXPERTS, dtype=jnp.int32)[None, :]).astype(jnp.int32)
    csum = jnp.cumsum(onehot, axis=0)
    rank = jnp.sum((csum - onehot) * onehot, axis=1)
    counts = csum[-1]
    tiles_per = (counts + tm - 1) // tm
    tile_end = jnp.cumsum(tiles_per).astype(jnp.int32)
    tile_start = (tile_end - tiles_per).astype(jnp.int32)
    pos = (jnp.sum(onehot * tile_start[None, :], axis=1) * tm + rank).astype(jnp.int32)

    xs = moe_gather(hn, pos, tile_start, tile_end, n_tiles * tm)
    ys = moe_experts(xs, tile_start, tile_end, w_gate, w_up, w_down, j)
    return moe_combine(ys, pos, h, tw, next_norm_g, layer)


def kernel(x, p, norm_mix, w_in, sgu_ln_g, sgu_ln_b, sgu_w, sgu_b, dn_conv_w, dn_a_log,
           dn_dt_bias, dn_norm_w, w_branch, w_out, norm_ffn, ffn_w_gate, ffn_w_up,
           ffn_w_down, moe_router, moe_w_gate, moe_w_up, moe_w_down, norm_ple,
           ple_w_gate, ple_b_gate, ple_w_proj, norm_final):
    bsz, seq, d = x.shape
    depth = p.shape[0]
    m = bsz * seq
    h = x.reshape(m, d)
    p2 = p.reshape(depth, m, p.shape[-1])
    main_w = 2 * SGU_WIDTH + 4 * DN_WIDTH
    n_small = 2 * DN_HEADS
    w_in_t = jnp.swapaxes(w_in, 1, 2)
    pad_h = 128 - n_small
    alog_row = jnp.pad(dn_a_log, ((0, 0), (DN_HEADS, pad_h))).reshape(depth, 1, 128)
    dtb_row = jnp.pad(dn_dt_bias, ((0, 0), (DN_HEADS, pad_h))).reshape(depth, 1, 128)
    ln_g = sgu_ln_g.reshape(depth, 1, SGU_WIDTH)
    ln_b = sgu_ln_b.reshape(depth, 1, SGU_WIDTH)
    sgu_b_t = jnp.swapaxes(sgu_b, 1, 2)
    nw = dn_norm_w.reshape(depth, 1, DN_HEAD_DIM)
    router_pad = jnp.pad(moe_router, ((0, 0), (0, 0), (0, 128 - N_EXPERTS)))
    ple_b = ple_b_gate.reshape(depth, 1, d)

    for i in range(depth):
        hn, small = norm_small(h, norm_mix, i, w_in_t, main_w // 128)
        uv = fused_mm([(hn, 0)], [(w_in_t, (i,), 0, 0)], [], _ep_gelu, 2 * SGU_WIDTH, BF16,
                      1024, 1024, "proj_uv", w_t=True)
        qkvz = fused_mm([(hn, 0)], [(w_in_t, (i,), (2 * SGU_WIDTH) // 1024, 0)], [], _ep_plain,
                        4 * DN_WIDTH, BF16, 1024, 1024, "proj_qkvz", w_t=True)
        gates = shifted_sigmoid_mm(hn, w_in_t, i, main_w, n_small, 2 * d, 1024, 1024, "proj_gates")
        y_a = sgu(uv, ln_g, ln_b, sgu_w, sgu_b_t, i)
        y_b = gdn(qkvz, small, dn_conv_w, alog_row, dtb_row, nw, i, bsz, seq, nb=4)
        merged = fused_mm([(y_a, 0), (y_b, 0)],
                          [(w_branch, (i, 0), 0, 0), (w_branch, (i, 1), 0, 1)],
                          [(gates, 0, "tile", 0), (gates, 0, "tile", d // 1024)],
                          _ep_merge, d, BF16, 1024, 1024, "merge")
        h = fused_mm([(merged, 0)], [(w_out, (i,), 0, 0)], [(h, 0, "tile", 0)],
                     _ep_residual, d, F32, 1024, 1024, "out_proj")
        j = i // 2
        if i % 2 == 0:
            hn = rmsnorm(h, norm_ffn, i, BF16)
            hmid = fused_mm([(hn, 0)], [(ffn_w_gate, (j,), 0, 0), (ffn_w_up, (j,), 0, 0)], [],
                            _ep_swiglu, ffn_w_gate.shape[-1], BF16, 1024, 512, "ffn_up")
            h = fused_mm([(hmid, 0)], [(ffn_w_down, (j,), 0, 0)], [(h, 0, "tile", 0)],
                         _ep_residual, d, F32, 512, 512, "ffn_down")
            hn = rmsnorm(h, norm_ple, i, BF16)
        else:
            h, hn = moe_layer(h, norm_ffn, i, router_pad, moe_w_gate, moe_w_up, moe_w_down, j,
                              norm_ple)
        h = fused_mm([(hn, 0), (p2, i)],
                     [(ple_w_gate, (i,), 0, 0), (ple_w_proj, (i,), 0, 1)],
                     [(h, 0, "tile", 0), (ple_b, i, "row", 0)],
                     _ep_ple, d, F32, 512, 1024, "ple")
    out = rmsnorm(h, norm_final, 0, F32)
    return out.reshape(bsz, seq, d)
```
